```python
import math
import jax
import jax.numpy as jnp
from jax import lax
import numpy as np

D_MODEL = 1024
BATCH = 4
SEQ = 4096
DEPTH = 2

CTX_LEN = 256
GRID_W = 64

D_FF = 2816
MACARON_WEIGHT = 0.5
N_MOD = 9
EPS = 1e-6

GLA_HEADS = 4
GLA_DK = 128
GLA_DV = 256
GLA_LOWRANK = 16
GLA_TAU = 16.0
GLA_CHUNK = 64

HY_WIDTH = 1024
HY_ORDER = 2
HY_SHORT = 3
HY_EMB = 33
HY_HIDDEN = 64
HY_FAST_DECAY = 0.3
HY_SLOW_DECAY = 1.5
HY_DECAY_TARGET = 1e-2

KEY_W = GLA_HEADS * GLA_DK
VAL_W = GLA_HEADS * GLA_DV
ALPHA_W = 2 * GLA_LOWRANK
OFF_K = 0
OFF_V = OFF_K + KEY_W
OFF_A = OFF_V + VAL_W
OFF_Q = OFF_A + ALPHA_W
OFF_R = OFF_Q + KEY_W
OFF_H = OFF_R + VAL_W
OFF_G = OFF_H + (HY_ORDER + 1) * HY_WIDTH
IN_W = OFF_G + 2 * D_MODEL

kernel_name = 'hybrid_gla_hyena_prefix_dit'


def rms_norm(x, g):
    xf = x.astype(jnp.float32)
    y = xf * lax.rsqrt(jnp.mean(xf * xf, axis=-1, keepdims=True) + EPS)
    return (y * g.astype(jnp.float32)).astype(x.dtype)


def modulate(h, shift, scale):
    return h * (1.0 + scale) + shift


def swiglu(h, w_gate, w_up, w_down):
    return (jax.nn.silu(h @ w_gate) * (h @ w_up)) @ w_down


def ffn_sublayer(x, mod, norm_g, w_gate, w_up, w_down):
    shift, scale, gate = mod
    y = swiglu(modulate(rms_norm(x, norm_g), shift, scale), w_gate, w_up, w_down)
    return x + MACARON_WEIGHT * gate * y


def gla_zero_state(batch):
    return jnp.zeros((batch, GLA_HEADS, GLA_DK, GLA_DV), jnp.float32)


def gla_inputs(p, wa_f, ba_f, wa_b, ba_b):
    bsz, L, _ = p.shape
    k = p[..., OFF_K:OFF_V].reshape(bsz, L, GLA_HEADS, GLA_DK)
    v = p[..., OFF_V:OFF_A].reshape(bsz, L, GLA_HEADS, GLA_DV)
    a = p[..., OFF_A:OFF_Q].astype(jnp.float32)
    la_f = jax.nn.log_sigmoid(a[..., :GLA_LOWRANK] @ wa_f.astype(jnp.float32) + ba_f.astype(jnp.float32)) / GLA_TAU
    la_b = jax.nn.log_sigmoid(a[..., GLA_LOWRANK:] @ wa_b.astype(jnp.float32) + ba_b.astype(jnp.float32)) / GLA_TAU
    shape = (bsz, L, GLA_HEADS, GLA_DK)
    return k, v, la_f.reshape(shape), la_b.reshape(shape)


def gla_query(p):
    bsz, L, _ = p.shape
    return p[..., OFF_Q:OFF_R].reshape(bsz, L, GLA_HEADS, GLA_DK)


def gla_chunk_states(k, v, log_a, s0):
    bsz, L, H, DK = k.shape
    n = L // GLA_CHUNK
    resh = lambda t: t.reshape(bsz, n, GLA_CHUNK, H, t.shape[-1]).astype(jnp.float32)
    kc, vc, la = resh(k), resh(v), resh(log_a)
    b = jnp.cumsum(la, axis=2)
    b_last = b[:, :, -1]
    k_end = kc * jnp.exp(b_last[:, :, None] - b)
    du = jnp.einsum('bnchk,bnchv->nbhkv', k_end, vc)
    decay = jnp.moveaxis(jnp.exp(b_last), 1, 0)

    def step(s, inp):
        d, u = inp
        return d[..., None] * s + u, s

    s_fin, s_start = lax.scan(step, s0.astype(jnp.float32), (decay, du))
    return b, kc, vc, s_start, s_fin


def gla_scan(q, k, v, log_a, s0):
    bsz, L, H, DK = q.shape
    b, kc, vc, s_start, s_fin = gla_chunk_states(k, v, log_a, s0)
    qc = q.reshape(bsz, L // GLA_CHUNK, GLA_CHUNK, H, DK).astype(jnp.float32) * (DK ** -0.5)
    q_dec = qc * jnp.exp(b)
    o_inter = jnp.einsum('bnchk,nbhkv->bnchv', q_dec, s_start)
    att = jnp.einsum('bnihk,bnjhk->bnhij', q_dec, kc * jnp.exp(-b))
    lower = jnp.tril(jnp.ones((GLA_CHUNK, GLA_CHUNK), dtype=bool))
    att = jnp.where(lower, att, 0.0)
    o_intra = jnp.einsum('bnhij,bnjhv->bnihv', att, vc)
    o = (o_inter + o_intra).reshape(bsz, L, H, v.shape[-1]).astype(v.dtype)
    return o, s_fin


def flip_seq(t):
    return jnp.flip(t, axis=1)


def gla_bidir(q, k, v, la_f, la_b, s_f0, s_b0):
    o_f, s_f = gla_scan(q, k, v, la_f, s_f0)
    o_b, s_b = gla_scan(flip_seq(q), flip_seq(k), flip_seq(v), flip_seq(la_b), s_b0)
    return o_f + flip_seq(o_b), s_f, s_b


def gla_context_states(k, v, la_f, la_b):
    s0 = gla_zero_state(k.shape[0])
    s_f = gla_chunk_states(k, v, la_f, s0)[-1]
    s_b = gla_chunk_states(flip_seq(k), flip_seq(v), flip_seq(la_b), s0)[-1]
    return s_f, s_b


def short_conv(u, w, b, rows):
    bsz, L, C = u.shape
    seqs = u.reshape(bsz * rows, L // rows, C) if rows is not None else u
    pad = HY_SHORT // 2
    y = lax.conv_general_dilated(seqs, w[:, None, :].astype(u.dtype), window_strides=(1,),
                                 padding=[(pad, pad)], dimension_numbers=('NWC', 'WIO', 'NWC'),
                                 feature_group_count=C)
    return y.reshape(bsz, L, C) + b


def hyena_filters(L, f_w1, f_b1, f_fr1, f_w2, f_b2, f_fr2, f_w3):
    bands = (HY_EMB - 1) // 2
    t = jnp.linspace(0.0, 1.0, L, dtype=jnp.float32)[:, None]
    w = (2.0 * math.pi / L) * jnp.arange(L, dtype=jnp.float32)[:, None]
    f = jnp.linspace(1e-4, bands - 1, bands, dtype=jnp.float32)[None, :]
    z = jnp.concatenate([t, jnp.cos(f * w), -jnp.sin(f * w)], axis=-1)
    h = jnp.sin(f_fr1 * (z @ f_w1 + f_b1))
    h = jnp.sin(f_fr2 * (h @ f_w2 + f_b2))
    h = (h @ f_w3).astype(jnp.float32).reshape(L, 2, HY_ORDER, HY_WIDTH)
    max_decay = math.log(HY_DECAY_TARGET) / HY_FAST_DECAY
    min_decay = math.log(HY_DECAY_TARGET) / HY_SLOW_DECAY
    deltas = jnp.abs(jnp.linspace(min_decay, max_decay, HY_WIDTH, dtype=jnp.float32))
    window = jnp.exp(-t * deltas[None, :])
    return h * window[:, None, None, :]


def long_conv(z, h_fwd, h_bwd, bias):
    bsz, L, W = z.shape
    taps = jnp.concatenate([h_fwd, jnp.zeros((1, W), h_fwd.dtype), jnp.flip(h_bwd[1:], axis=0)], axis=0)
    taps = taps / (jnp.sum(jnp.abs(taps), axis=0, keepdims=True) + EPS)
    zf = z.astype(jnp.float32)
    spec = jnp.fft.rfft(zf, n=2 * L, axis=1) * jnp.fft.rfft(taps, axis=0)[None]
    y = jnp.fft.irfft(spec, n=2 * L, axis=1)[:, :L]
    return (y + bias.astype(jnp.float32) * zf).astype(z.dtype)


def hyena(u, rows, conv_w, conv_b, f_w1, f_b1, f_fr1, f_w2, f_b2, f_fr2, f_w3, bias):
    L = u.shape[1]
    u = short_conv(u, conv_w, conv_b, rows)
    v, x1, x2 = jnp.split(u, HY_ORDER + 1, axis=-1)
    h = hyena_filters(L, f_w1, f_b1, f_fr1, f_w2, f_b2, f_fr2, f_w3)
    z = v
    for o, gate in enumerate((x1, x2)):
        z = gate * long_conv(z, h[:, 0, o], h[:, 1, o], bias[o])
    return z


def mix_output(p, o_gla, rows, gla_norm_g, w_gla_out, hy_params, w_hy_out, w_out):
    bsz, L, _ = p.shape
    o = o_gla.astype(jnp.float32)
    o = o * lax.rsqrt(jnp.mean(o * o, axis=-1, keepdims=True) + EPS)
    o = (o.reshape(bsz, L, VAL_W) * gla_norm_g.astype(jnp.float32)).astype(p.dtype)
    y_gla = (o * jax.nn.silu(p[..., OFF_R:OFF_H])) @ w_gla_out
    y_hy = hyena(p[..., OFF_H:OFF_G], rows, *hy_params) @ w_hy_out
    gates = jax.nn.sigmoid(p[..., OFF_G:IN_W])
    return (gates[..., :D_MODEL] * y_gla + gates[..., D_MODEL:] * y_hy) @ w_out


def setup_inputs(seed: int = 0) -> dict:
    key = jax.random.key(seed)
    ks = iter(jax.random.split(key, 48))
    nrm = lambda shape, s: jax.random.normal(next(ks), shape, jnp.float32) * s
    D, Lr = D_MODEL, GLA_LOWRANK
    return {
        'x': nrm((BATCH, SEQ, D), 1.0),
        'c': nrm((BATCH, D), 1.0),
        'ctx': nrm((BATCH, CTX_LEN, D), 1.0),
        'c_ctx': nrm((D,), 1.0),
        'ada_w': nrm((DEPTH, D, N_MOD * D), 0.5 * D ** -0.5),
        'ada_b': nrm((DEPTH, N_MOD * D), 0.02),
        'ffn1_norm_g': 1.0 + nrm((DEPTH, D), 0.02),
        'ffn1_w_gate': nrm((DEPTH, D, D_FF), D ** -0.5),
        'ffn1_w_up': nrm((DEPTH, D, D_FF), D ** -0.5),
        'ffn1_w_down': nrm((DEPTH, D_FF, D), D_FF ** -0.5),
        'mix_norm_g': 1.0 + nrm((DEPTH, D), 0.02),
        'w_in': nrm((DEPTH, D, IN_W), D ** -0.5),
        'gla_wa_f': nrm((DEPTH, Lr, KEY_W), Lr ** -0.5),
        'gla_ba_f': nrm((DEPTH, KEY_W), 0.1),
        'gla_wa_b': nrm((DEPTH, Lr, KEY_W), Lr ** -0.5),
        'gla_ba_b': nrm((DEPTH, KEY_W), 0.1),
        'gla_norm_g': 1.0 + nrm((DEPTH, VAL_W), 0.02),
        'w_gla_out': nrm((DEPTH, VAL_W, D), VAL_W ** -0.5),
        'hy_conv_w': nrm((DEPTH, HY_SHORT, (HY_ORDER + 1) * HY_WIDTH), HY_SHORT ** -0.5),
        'hy_conv_b': nrm((DEPTH, (HY_ORDER + 1) * HY_WIDTH), 0.02),
        'hy_f_w1': nrm((DEPTH, HY_EMB, HY_HIDDEN), HY_EMB ** -0.5),
        'hy_f_b1': nrm((DEPTH, HY_HIDDEN), 0.1),
        'hy_f_freq1': 1.0 + nrm((DEPTH, HY_HIDDEN), 0.1),
        'hy_f_w2': nrm((DEPTH, HY_HIDDEN, HY_HIDDEN), HY_HIDDEN ** -0.5),
        'hy_f_b2': nrm((DEPTH, HY_HIDDEN), 0.1),
        'hy_f_freq2': 1.0 + nrm((DEPTH, HY_HIDDEN), 0.1),
        'hy_f_w3': nrm((DEPTH, HY_HIDDEN, 2 * HY_ORDER * HY_WIDTH), HY_HIDDEN ** -0.5),
        'hy_bias': nrm((DEPTH, HY_ORDER, HY_WIDTH), 0.5),
        'w_hy_out': nrm((DEPTH, HY_WIDTH, D), HY_WIDTH ** -0.5),
        'w_out': nrm((DEPTH, D, D), D ** -0.5),
        'ffn2_norm_g': 1.0 + nrm((DEPTH, D), 0.02),
        'ffn2_w_gate': nrm((DEPTH, D, D_FF), D ** -0.5),
        'ffn2_w_up': nrm((DEPTH, D, D_FF), D ** -0.5),
        'ffn2_w_down': nrm((DEPTH, D_FF, D), D_FF ** -0.5),
        'final_norm_g': 1.0 + nrm((D,), 0.02),
    }


def reference(x, c, ctx, c_ctx, ada_w, ada_b, ffn1_norm_g, ffn1_w_gate, ffn1_w_up, ffn1_w_down,
              mix_norm_g, w_in, gla_wa_f, gla_ba_f, gla_wa_b, gla_ba_b, gla_norm_g, w_gla_out,
              hy_conv_w, hy_conv_b, hy_f_w1, hy_f_b1, hy_f_freq1, hy_f_w2, hy_f_b2, hy_f_freq2, hy_f_w3,
              hy_bias, w_hy_out, w_out, ffn2_norm_g, ffn2_w_gate, ffn2_w_up, ffn2_w_down, final_norm_g):
    rows = x.shape[1] // GRID_W
    xl, xc = x, ctx
    for i in range(DEPTH):
        last = i == DEPTH - 1
        ml = jnp.split((jax.nn.silu(c) @ ada_w[i] + ada_b[i])[:, None, :], N_MOD, axis=-1)
        mc = jnp.split((jax.nn.silu(c_ctx) @ ada_w[i] + ada_b[i])[None, None, :], N_MOD, axis=-1)
        ffn1 = (ffn1_norm_g[i], ffn1_w_gate[i], ffn1_w_up[i], ffn1_w_down[i])
        ffn2 = (ffn2_norm_g[i], ffn2_w_gate[i], ffn2_w_up[i], ffn2_w_down[i])
        gla_p = (gla_wa_f[i], gla_ba_f[i], gla_wa_b[i], gla_ba_b[i])
        hy_p = (hy_conv_w[i], hy_conv_b[i], hy_f_w1[i], hy_f_b1[i], hy_f_freq1[i], hy_f_w2[i],
                hy_f_b2[i], hy_f_freq2[i], hy_f_w3[i], hy_bias[i])

        xl = ffn_sublayer(xl, ml[0:3], *ffn1)
        xc = ffn_sublayer(xc, mc[0:3], *ffn1)

        hl = modulate(rms_norm(xl, mix_norm_g[i]), ml[3], ml[4])
        hc = modulate(rms_norm(xc, mix_norm_g[i]), mc[3], mc[4])
        pl = hl @ w_in[i]
        if last:
            kc, vc, laf_c, lab_c = gla_inputs(hc @ w_in[i][:, :OFF_Q], *gla_p)
            s_f, s_b = gla_context_states(kc, vc, laf_c, lab_c)
        else:
            pc = hc @ w_in[i]
            kc, vc, laf_c, lab_c = gla_inputs(pc, *gla_p)
            zero = gla_zero_state(xc.shape[0])
            o_c, s_f, s_b = gla_bidir(gla_query(pc), kc, vc, laf_c, lab_c, zero, zero)
            xc = xc + mc[5] * mix_output(pc, o_c, None, gla_norm_g[i], w_gla_out[i], hy_p, w_hy_out[i], w_out[i])
            xc = ffn_sublayer(xc, mc[6:9], *ffn2)
        kl, vl, laf_l, lab_l = gla_inputs(pl, *gla_p)
        o_l, _, _ = gla_bidir(gla_query(pl), kl, vl, laf_l, lab_l, s_f, s_b)
        xl = xl + ml[5] * mix_output(pl, o_l, rows, gla_norm_g[i], w_gla_out[i], hy_p, w_hy_out[i], w_out[i])

        xl = ffn_sublayer(xl, ml[6:9], *ffn2)
    return rms_norm(xl, final_norm_g)
```

```python
import functools
import math

import numpy as np
import jax
import jax.numpy as jnp
from jax import lax
from jax.experimental import pallas as pl
from jax.experimental.pallas import tpu as pltpu

F32 = jnp.float32
BF16 = jnp.bfloat16

D = 1024
NB = 4
SEQ = 4096
DEPTH = 2
CTX = 256
GRID_W = 64
D_FF = 2816
HALF_STEP = 0.5
N_MOD = 9
EPS = 1e-6

HEADS = 4
DK = 128
DV = 256
LOWRANK = 16
TAU = 16.0
CHUNK = 64
KEY_W = HEADS * DK
VAL_W = HEADS * DV

HY_W = 1024
HY_EMB = 33
HY_HID = 64
HY_FAST = 0.3
HY_SLOW = 1.5
HY_TARGET = 1e-2

OFF_K = 0
OFF_V = OFF_K + KEY_W
OFF_A = OFF_V + VAL_W
OFF_Q = OFF_A + 2 * LOWRANK
OFF_R = OFF_Q + KEY_W
OFF_H = OFF_R + VAL_W
OFF_G = OFF_H + 3 * HY_W
IN_W = OFF_G + 2 * D

R_LAT = NB * SEQ
R_CTX = NB * CTX
R_ALL = R_LAT + R_CTX

LANES = 128
VMEM_LIMIT = 56 * 1024 * 1024

FFT_N = 2 * SEQ
N1 = 128
N2 = 64
PITCH_A = 264
PITCH_D = 136
CTX_N = 2 * CTX


def _cparams(sem):
    return pltpu.CompilerParams(dimension_semantics=sem, vmem_limit_bytes=VMEM_LIMIT)


def _const_spec(shape):
    nd = len(shape)
    return pl.BlockSpec(shape, lambda *_: (0,) * nd, pipeline_mode=pl.Buffered(1))


def _split(x):
    hi = x.astype(BF16)
    lo = (x - hi.astype(F32)).astype(BF16)
    return hi, lo


def _dot(a, b):
    return jnp.dot(a, b, preferred_element_type=F32)


def _mm3(a_hi, a_lo, b):
    b_hi, b_lo = _split(b)
    return _dot(a_hi, b_hi) + _dot(a_lo, b_hi) + _dot(a_hi, b_lo)


def _mm3f(a, b):
    a_hi, a_lo = _split(a)
    return _mm3(a_hi, a_lo, b)


def _sigmoid(x):
    return 1.0 / (1.0 + jnp.exp(-x))


def _rms(x, g):
    ms = jnp.mean(x * x, axis=-1, keepdims=True)
    return x * lax.rsqrt(ms + EPS) * g


def _mods_kernel(c_ref, w_ref, b_ref, o_ref):
    c = c_ref[...]
    s = (c * _sigmoid(c)).astype(BF16)
    o_ref[...] = _dot(s, w_ref[...].astype(BF16)) + b_ref[...]


def _mods(cvec, ada_w, ada_b):
    tn = 1024
    return pl.pallas_call(
        _mods_kernel,
        grid=(DEPTH, N_MOD * D // tn),
        in_specs=[
            pl.BlockSpec((8, D), lambda l, j: (0, 0)),
            pl.BlockSpec((None, D, tn), lambda l, j: (l, 0, j)),
            pl.BlockSpec((None, 1, tn), lambda l, j: (l, 0, j)),
        ],
        out_specs=pl.BlockSpec((None, 8, tn), lambda l, j: (l, 0, j)),
        out_shape=jax.ShapeDtypeStruct((DEPTH, 8, N_MOD * D), F32),
        compiler_params=_cparams(("arbitrary", "arbitrary")),
        name="mods",
    )(cvec, ada_w, ada_b.reshape(DEPTH, 1, N_MOD * D))


def _mod_row(t, tm):
    lat_tiles = R_LAT // tm
    per_batch = SEQ // tm
    return jnp.where(t < lat_tiles, t // per_batch, NB)


def _ffn_kernel(x_ref, m_ref, g_ref, wg_ref, wu_ref, wd_ref, *rest, final):
    if final:
        fg_ref, o_ref = rest
    else:
        (o_ref,) = rest
    x = x_ref[...]
    m = m_ref[...]
    shift, scale, gate = m[:, :D], m[:, D:2 * D], m[:, 2 * D:]
    h = (_rms(x, g_ref[...]) * (1.0 + scale) + shift).astype(BF16)
    a = _dot(h, wg_ref[...])
    u = _dot(h, wu_ref[...])
    act = (a * _sigmoid(a) * u).astype(BF16)
    y = _dot(act, wd_ref[...])
    out = x + (HALF_STEP * gate) * y
    if final:
        out = _rms(out, fg_ref[...])
    o_ref[...] = out


def _ffn(x, mods, layer, slot, norm_g, wg, wu, wd, rows, final_g=None):
    tm = 512
    final = final_g is not None
    in_specs = [
        pl.BlockSpec((tm, D), lambda t: (t, 0)),
        pl.BlockSpec((None, None, 1, 3 * D), lambda t: (layer, _mod_row(t, tm), 0, slot)),
        _const_spec((1, D)),
        _const_spec((D, D_FF)),
        _const_spec((D, D_FF)),
        _const_spec((D_FF, D)),
    ]
    args = [x, mods, norm_g.reshape(1, D), wg, wu, wd]
    if final:
        in_specs.append(_const_spec((1, D)))
        args.append(final_g.reshape(1, D))
    return pl.pallas_call(
        functools.partial(_ffn_kernel, final=final),
        grid=(rows // tm,),
        in_specs=in_specs,
        out_specs=pl.BlockSpec((tm, D), lambda t: (t, 0)),
        out_shape=jax.ShapeDtypeStruct((rows, D), F32),
        compiler_params=_cparams(("arbitrary",)),
        name="ffn",
    )(*args)


def _proj_kernel(x_ref, m_ref, g_ref, wkvq_ref, wahi_ref, walo_ref, wr_ref, wh_ref, wg_ref,
                 kvq_ref, a_ref, r_ref, hy_ref, gt_ref):
    x = x_ref[...]
    m = m_ref[...]
    shift, scale = m[:, :D], m[:, D:2 * D]
    h = _rms(x, g_ref[...]) * (1.0 + scale) + shift
    hb = h.astype(BF16)
    kvq_ref[...] = _dot(hb, wkvq_ref[...]).astype(kvq_ref.dtype)
    hl = (h - hb.astype(F32)).astype(BF16)
    a_ref[...] = _dot(hb, wahi_ref[...]) + _dot(hl, wahi_ref[...]) + _dot(hb, walo_ref[...])
    r_ref[...] = _dot(hb, wr_ref[...]).astype(r_ref.dtype)
    hy_ref[...] = _dot(hb, wh_ref[...]).astype(hy_ref.dtype)
    gt_ref[...] = _dot(hb, wg_ref[...]).astype(gt_ref.dtype)


def _proj(x, mods, layer, norm_g, wkvq, wa_hi, wa_lo, wr, wh, wg):
    tm = 256
    nk = 2 * KEY_W + VAL_W
    na = 2 * LOWRANK
    row = lambda w: pl.BlockSpec((tm, w), lambda t: (t, 0))
    return pl.pallas_call(
        _proj_kernel,
        grid=(R_ALL // tm,),
        in_specs=[
            row(D),
            pl.BlockSpec((None, None, 1, 3 * D), lambda t: (layer, _mod_row(t, tm), 0, 1)),
            _const_spec((1, D)),
            _const_spec((D, nk)),
            _const_spec((D, na)),
            _const_spec((D, na)),
            _const_spec((D, VAL_W)),
            _const_spec((D, 3 * HY_W)),
            _const_spec((D, 2 * D)),
        ],
        out_specs=[row(nk), row(na), row(VAL_W), row(3 * HY_W), row(2 * D)],
        out_shape=[
            jax.ShapeDtypeStruct((R_ALL, nk), BF16),
            jax.ShapeDtypeStruct((R_ALL, na), F32),
            jax.ShapeDtypeStruct((R_ALL, VAL_W), BF16),
            jax.ShapeDtypeStruct((R_ALL, 3 * HY_W), F32),
            jax.ShapeDtypeStruct((R_ALL, 2 * D), BF16),
        ],
        compiler_params=_cparams(("arbitrary",)),
        name="proj",
    )(x, mods, norm_g.reshape(1, D), wkvq, wa_hi, wa_lo, wr, wh, wg)


GLA_BLK = 256
GLA_STEPS = 1 + SEQ // GLA_BLK


def _gla_row_block(b, d, s):
    lat = jnp.where(d == 0, s - 1, SEQ // GLA_BLK - s)
    return jnp.where(s == 0, R_LAT // GLA_BLK + b, b * (SEQ // GLA_BLK) + lat)


def _gla_body(kvq_ref, a_ref, wa_ref, ba_ref, tri_ref, ones_ref, o_ref, st_ref, qd_s, kn_s, ke_s, dec_s):
    d = pl.program_id(1)
    s = pl.program_id(2)

    @pl.when(s == 0)
    def _():
        st_ref[...] = jnp.zeros_like(st_ref)

    a = a_ref[...]
    pre = _mm3f(a, wa_ref[...]) + ba_ref[...]
    la = (jnp.minimum(pre, 0.0) - jnp.log(1.0 + jnp.exp(-jnp.abs(pre)))) * (1.0 / TAU)
    la_hi, la_lo = _split(la)
    tri = tri_ref[...]
    ones = ones_ref[...]
    bcum = _dot(tri, la_hi) + _dot(tri, la_lo)
    btot = _dot(ones, la_hi) + _dot(ones, la_lo)

    k = kvq_ref[:, :KEY_W].astype(F32)
    q = kvq_ref[:, KEY_W + VAL_W:].astype(F32)
    qd_s[...] = (q * (DK ** -0.5) * jnp.exp(bcum)).astype(BF16)
    kn_s[...] = (k * jnp.exp(-bcum)).astype(BF16)
    ke_s[...] = (k * jnp.exp(btot - bcum)).astype(BF16)
    dec_s[...] = jnp.exp(btot)

    ri = lax.broadcasted_iota(jnp.int32, (CHUNK, CHUNK), 0)
    ci = lax.broadcasted_iota(jnp.int32, (CHUNK, CHUNK), 1)
    seen = (ri - ci) * (1 - 2 * d) >= 0

    nchunk = GLA_BLK // CHUNK
    for j in range(nchunk):
        r0 = pl.multiple_of(jnp.where(d == 0, j, nchunk - 1 - j) * CHUNK, CHUNK)
        rows = pl.ds(r0, CHUNK)
        for h in range(HEADS):
            ksl = slice(h * DK, (h + 1) * DK)
            qd = qd_s[rows, ksl]
            kn = kn_s[rows, ksl]
            ke = ke_s[rows, ksl]
            v = kvq_ref[rows, KEY_W + h * DV:KEY_W + (h + 1) * DV]
            dec = dec_s[pl.ds(r0, 1), ksl]
            st = st_ref[h]
            att = lax.dot_general(qd, kn, (((1,), (1,)), ((), ())), preferred_element_type=F32)
            att = jnp.where(seen, att, 0.0).astype(BF16)
            o = _dot(att, v) + lax.dot_general(qd, st.astype(BF16), (((1,), (1,)), ((), ())),
                                               preferred_element_type=F32)
            o_ref[rows, h * DV:(h + 1) * DV] = o
            upd = lax.dot_general(v, ke, (((0,), (0,)), ((), ())), preferred_element_type=F32)
            st_ref[h] = st * dec + upd


def _gla(kvq, a, wa_cat, ba_cat, tri, ones):
    nk = 2 * KEY_W + VAL_W
    row_map = lambda b, d, s: (_gla_row_block(b, d, s), 0)
    return pl.pallas_call(
        _gla_body,
        grid=(NB, 2, GLA_STEPS),
        in_specs=[
            pl.BlockSpec((GLA_BLK, nk), row_map),
            pl.BlockSpec((GLA_BLK, 2 * LOWRANK), row_map),
            pl.BlockSpec((None, 2 * LOWRANK, KEY_W), lambda b, d, s: (d, 0, 0)),
            pl.BlockSpec((None, 1, KEY_W), lambda b, d, s: (d, 0, 0)),
            pl.BlockSpec((None, GLA_BLK, GLA_BLK), lambda b, d, s: (d, 0, 0)),
            pl.BlockSpec((GLA_BLK, GLA_BLK), lambda b, d, s: (0, 0)),
        ],
        out_specs=pl.BlockSpec((None, GLA_BLK, VAL_W), lambda b, d, s: (d, _gla_row_block(b, d, s), 0)),
        out_shape=jax.ShapeDtypeStruct((2, R_ALL, VAL_W), F32),
        scratch_shapes=[
            pltpu.VMEM((HEADS, DV, DK), F32),
            pltpu.VMEM((GLA_BLK, KEY_W), BF16),
            pltpu.VMEM((GLA_BLK, KEY_W), BF16),
            pltpu.VMEM((GLA_BLK, KEY_W), BF16),
            pltpu.VMEM((GLA_BLK, KEY_W), F32),
        ],
        compiler_params=_cparams(("arbitrary", "arbitrary", "arbitrary")),
        name="gla",
    )(kvq, a, wa_cat, ba_cat, tri, ones)


def _mix_kernel(x_ref, m_ref, of_ref, ob_ref, r_ref, z_ref, gt_ref, ng_ref, wgo_ref, who_ref, wo_ref, o_ref):
    x = x_ref[...]
    gate = m_ref[...][:, 2 * D:]
    o = of_ref[...] + ob_ref[...]
    ng = ng_ref[...]
    parts = []
    for h in range(HEADS):
        sl = slice(h * DV, (h + 1) * DV)
        oh = o[:, sl]
        ms = jnp.mean(oh * oh, axis=-1, keepdims=True)
        parts.append(oh * lax.rsqrt(ms + EPS) * ng[:, sl])
    on = jnp.concatenate(parts, axis=-1)
    r = r_ref[...].astype(F32)
    y_gla = _dot((on * (r * _sigmoid(r))).astype(BF16), wgo_ref[...])
    y_hy = _dot(z_ref[...].astype(BF16), who_ref[...])
    gt = gt_ref[...].astype(F32)
    mix = _sigmoid(gt[:, :D]) * y_gla + _sigmoid(gt[:, D:]) * y_hy
    o_ref[...] = x + gate * _dot(mix.astype(BF16), wo_ref[...])


def _mix(x, mods, layer, o2, r, z3, gt, norm_g, wgo, who, wo, rows):
    tm = 512
    row = lambda w: pl.BlockSpec((tm, w), lambda t: (t, 0))
    return pl.pallas_call(
        _mix_kernel,
        grid=(rows // tm,),
        in_specs=[
            row(D),
            pl.BlockSpec((None, None, 1, 3 * D), lambda t: (layer, _mod_row(t, tm), 0, 1)),
            pl.BlockSpec((None, tm, VAL_W), lambda t: (0, t, 0)),
            pl.BlockSpec((None, tm, VAL_W), lambda t: (1, t, 0)),
            row(VAL_W),
            row(HY_W),
            row(2 * D),
            _const_spec((1, VAL_W)),
            _const_spec((VAL_W, D)),
            _const_spec((HY_W, D)),
            _const_spec((D, D)),
        ],
        out_specs=row(D),
        out_shape=jax.ShapeDtypeStruct((rows, D), F32),
        compiler_params=_cparams(("arbitrary",)),
        name="mix",
    )(x, mods, o2, o2, r, z3, gt, norm_g.reshape(1, VAL_W), wgo, who, wo)


def _hp_dot(a, b):
    return jnp.dot(a, b, preferred_element_type=F32, precision=lax.Precision.HIGHEST)


def _filt_mlp_kernel(z_ref, w1_ref, b1_ref, f1_ref, w2_ref, b2_ref, f2_ref, o_ref):
    h = jnp.sin(f1_ref[...] * (_hp_dot(z_ref[...], w1_ref[...]) + b1_ref[...]))
    o_ref[...] = jnp.sin(f2_ref[...] * (_hp_dot(h, w2_ref[...]) + b2_ref[...]))


def _filt_mlp(feats, w1, b1, f1, w2, b2, f2):
    rows = feats.shape[0]
    tm = 512
    vec = lambda: _const_spec((1, HY_HID))
    return pl.pallas_call(
        _filt_mlp_kernel,
        grid=(rows // tm,),
        in_specs=[pl.BlockSpec((tm, HY_HID), lambda t: (t, 0)), _const_spec((HY_HID, HY_HID)), vec(), vec(),
                  _const_spec((HY_HID, HY_HID)), vec(), vec()],
        out_specs=pl.BlockSpec((tm, HY_HID), lambda t: (t, 0)),
        out_shape=jax.ShapeDtypeStruct((rows, HY_HID), F32),
        compiler_params=_cparams(("arbitrary",)),
        name="filt_mlp",
    )(feats, w1, b1.reshape(1, HY_HID), f1.reshape(1, HY_HID), w2, b2.reshape(1, HY_HID), f2.reshape(1, HY_HID))


def _taps_chunk(h2, rc, w3f, w3b, delta):
    t, fsel, bsel = rc[:, 0:1], rc[:, 1:2], rc[:, 2:3]
    win = jnp.exp(-t * delta)
    return win * (fsel * _hp_dot(h2, w3f) + bsel * _hp_dot(h2, w3b))


def _stage_b_fwd(buf_a, k1, p_ref, q_ref, tr_ref, ti_ref):
    re = buf_a[pl.ds(k1, N2, stride=PITCH_A), :]
    im = buf_a[pl.ds(N1 + k1, N2, stride=PITCH_A), :]
    x = jnp.concatenate([re, im], axis=0)
    f = p_ref[...] * tr_ref[pl.ds(k1, 1), :] + q_ref[...] * ti_ref[pl.ds(k1, 1), :]
    return _mm3f(f, x)


def _spec_lat_kernel(h2_ref, rc_ref, w3f_ref, w3b_ref, dl_ref, wt_hi, wt_lo, p_ref, q_ref, tr_ref, ti_ref,
                     o_ref, taps, buf_a):
    tb = 512
    delta = dl_ref[...]

    def fill(i, acc):
        rows = pl.ds(pl.multiple_of(i * tb, tb), tb)
        tp = _taps_chunk(h2_ref[rows, :], rc_ref[rows, :], w3f_ref[...], w3b_ref[...], delta)
        taps[rows, :] = tp
        return acc + jnp.sum(jnp.abs(tp), axis=0, keepdims=True)

    l1 = lax.fori_loop(0, FFT_N // tb, fill, jnp.zeros((1, LANES), F32))
    inv = 1.0 / (l1 + EPS)

    def stage_a(n2, c):
        rhs = taps[pl.ds(pl.multiple_of(n2 * N1, N1), N1), :]
        buf_a[pl.ds(pl.multiple_of(n2 * PITCH_A, 8), 2 * N1), :] = _mm3(wt_hi[...], wt_lo[...], rhs)
        return c

    lax.fori_loop(0, N2, stage_a, 0)

    def stage_b(k1, c):
        y = _stage_b_fwd(buf_a, k1, p_ref, q_ref, tr_ref, ti_ref)
        o_ref[pl.ds(pl.multiple_of(k1 * 2 * N2, 2 * N2), 2 * N2), :] = y * inv
        return c

    lax.fori_loop(0, N1, stage_b, 0)


def _spec_lat(h2p, rcp, w3, delta, cst):
    nct = HY_W // LANES
    return pl.pallas_call(
        _spec_lat_kernel,
        grid=(2, nct),
        in_specs=[
            _const_spec((FFT_N, HY_HID)),
            _const_spec((FFT_N, 8)),
            pl.BlockSpec((HY_HID, LANES), lambda o, c: (0, o * nct + c)),
            pl.BlockSpec((HY_HID, LANES), lambda o, c: (0, 2 * nct + o * nct + c)),
            pl.BlockSpec((1, LANES), lambda o, c: (0, c)),
            _const_spec((2 * N1, N1)), _const_spec((2 * N1, N1)),
            _const_spec((2 * N2, 2 * N2)), _const_spec((2 * N2, 2 * N2)),
            _const_spec((N1, 2 * N2)), _const_spec((N1, 2 * N2)),
        ],
        out_specs=pl.BlockSpec((None, None, 2 * FFT_N, LANES), lambda o, c: (o, c, 0, 0)),
        out_shape=jax.ShapeDtypeStruct((2, nct, 2 * FFT_N, LANES), F32),
        scratch_shapes=[pltpu.VMEM((FFT_N, LANES), F32), pltpu.VMEM((N2 * PITCH_A, LANES), F32)],
        compiler_params=_cparams(("arbitrary", "arbitrary")),
        name="spec_lat",
    )(h2p, rcp, w3, w3, delta, cst["wt_hi"], cst["wt_lo"], cst["p"], cst["q"], cst["tr2"], cst["ti2"])


def _hy_lat_kernel(ua_ref, ub_ref, h_ref, cw_ref, cb_ref, bias_ref, wa_hi, wa_lo, p_ref, q_ref, tr_ref, ti_ref,
                   g_hi, g_lo, pi_ref, qi_ref, tri_ref, tii_ref, o_ref, zin, buf_a, buf_d):
    s = pl.program_id(2)

    def conv_slices(n2):
        w0, w1, w2 = cw_ref[0:1, :], cw_ref[1:2, :], cw_ref[2:3, :]
        cb = cb_ref[...]
        nm = jnp.maximum(n2 - 1, 0)
        nx = jnp.minimum(n2 + 1, GRID_W - 1)
        outs = []
        for u in (ua_ref, ub_ref):
            cur = u[pl.ds(n2, N2, stride=GRID_W), :]
            prv = jnp.where(n2 > 0, u[pl.ds(nm, N2, stride=GRID_W), :], 0.0)
            nxt = jnp.where(n2 < GRID_W - 1, u[pl.ds(nx, N2, stride=GRID_W), :], 0.0)
            outs.append(w0 * prv + w1 * cur + w2 * nxt + cb)
        return jnp.concatenate(outs, axis=0)

    @pl.when(s == 0)
    def _():
        def body(n2, c):
            zin[n2] = conv_slices(n2)
            return c
        lax.fori_loop(0, N2, body, 0)

    @pl.when(s > 0)
    def _():
        def stage_a(n2, c):
            buf_a[pl.ds(pl.multiple_of(n2 * PITCH_A, 8), 2 * N1), :] = _mm3(wa_hi[...], wa_lo[...], zin[n2])
            return c
        lax.fori_loop(0, N2, stage_a, 0)

        def stage_b(k1, c):
            y = _stage_b_fwd(buf_a, k1, p_ref, q_ref, tr_ref, ti_ref)
            hb = h_ref[pl.ds(pl.multiple_of(k1 * 2 * N2, 2 * N2), 2 * N2), :]
            yr, yi = y[:N2], y[N2:]
            hr, hi = hb[:N2], hb[N2:]
            prod = jnp.concatenate([yr * hr - yi * hi, yr * hi + yi * hr], axis=0)
            buf_d[pl.ds(pl.multiple_of(k1 * PITCH_D, 8), 2 * N2), :] = _mm3(g_hi[...], g_lo[...], prod)
            return c
        lax.fori_loop(0, N1, stage_b, 0)

        def stage_ai(n2, c):
            re = buf_d[pl.ds(n2, N1, stride=PITCH_D), :]
            im = buf_d[pl.ds(N2 + n2, N1, stride=PITCH_D), :]
            x = jnp.concatenate([re, im], axis=0)
            m = pi_ref[...] * tri_ref[pl.ds(n2, 1), :] + qi_ref[...] * tii_ref[pl.ds(n2, 1), :]
            y = _mm3f(m, x)
            zn = conv_slices(n2) * (y + bias_ref[...] * zin[n2])
            zin[n2] = zn

            @pl.when(s == 2)
            def _():
                o_ref[0, pl.ds(n2, N2, stride=GRID_W), :] = zn[:N2]
                o_ref[1, pl.ds(n2, N2, stride=GRID_W), :] = zn[N2:]
            return c
        lax.fori_loop(0, N2, stage_ai, 0)


def _hy_lat(hy, hspec, conv_w, conv_b, bias, cst):
    nct = HY_W // LANES
    part = lambda c, p, s: s * nct + c
    order = lambda s: jnp.maximum(s - 1, 0)
    names = ["wa_hi", "wa_lo", "p", "q", "tr2", "ti2", "g_hi", "g_lo", "pi", "qi", "tri2", "tii2"]
    consts = [cst[n] for n in names]
    return pl.pallas_call(
        _hy_lat_kernel,
        grid=(nct, NB // 2, 3),
        in_specs=[
            pl.BlockSpec((SEQ, LANES), lambda c, p, s: (2 * p, part(c, p, s))),
            pl.BlockSpec((SEQ, LANES), lambda c, p, s: (2 * p + 1, part(c, p, s))),
            pl.BlockSpec((None, None, 2 * FFT_N, LANES), lambda c, p, s: (order(s), c, 0, 0),
                         pipeline_mode=pl.Buffered(1)),
            pl.BlockSpec((3, LANES), lambda c, p, s: (0, part(c, p, s))),
            pl.BlockSpec((1, LANES), lambda c, p, s: (0, part(c, p, s))),
            pl.BlockSpec((None, 1, LANES), lambda c, p, s: (order(s), 0, c)),
        ] + [_const_spec(x.shape) for x in consts],
        out_specs=pl.BlockSpec((2, SEQ, LANES), lambda c, p, s: (p, 0, c)),
        out_shape=jax.ShapeDtypeStruct((NB, SEQ, HY_W), F32),
        scratch_shapes=[
            pltpu.VMEM((N2, 2 * N2, LANES), F32),
            pltpu.VMEM((N2 * PITCH_A, LANES), F32),
            pltpu.VMEM((N1 * PITCH_D, LANES), F32),
        ],
        compiler_params=_cparams(("arbitrary", "arbitrary", "arbitrary")),
        name="hy_lat",
    )(hy, hy, hspec, conv_w, conv_b, bias, *consts)


def _spec_ctx_kernel(h2_ref, rc_ref, w3f_ref, w3b_ref, dl_ref, wt_hi, wt_lo, o_ref):
    tp = _taps_chunk(h2_ref[...], rc_ref[...], w3f_ref[...], w3b_ref[...], dl_ref[...])
    l1 = jnp.sum(jnp.abs(tp), axis=0, keepdims=True)
    o_ref[...] = _mm3(wt_hi[...], wt_lo[...], tp * (1.0 / (l1 + EPS)))


def _spec_ctx(h2, rc, w3, delta, cst):
    nct = HY_W // LANES
    return pl.pallas_call(
        _spec_ctx_kernel,
        grid=(2, nct),
        in_specs=[
            _const_spec((CTX_N, HY_HID)),
            _const_spec((CTX_N, 8)),
            pl.BlockSpec((HY_HID, LANES), lambda o, c: (0, o * nct + c)),
            pl.BlockSpec((HY_HID, LANES), lambda o, c: (0, 2 * nct + o * nct + c)),
            pl.BlockSpec((1, LANES), lambda o, c: (0, c)),
            _const_spec((2 * CTX_N, CTX_N)), _const_spec((2 * CTX_N, CTX_N)),
        ],
        out_specs=pl.BlockSpec((None, None, 2 * CTX_N, LANES), lambda o, c: (o, c, 0, 0)),
        out_shape=jax.ShapeDtypeStruct((2, nct, 2 * CTX_N, LANES), F32),
        compiler_params=_cparams(("arbitrary", "arbitrary")),
        name="spec_ctx",
    )(h2, rc, w3, w3, delta, cst["cwt_hi"], cst["cwt_lo"])


def _hy_ctx_kernel(va, vb, x1a, x1b, x2a, x2b, h_ref, cwv, cw1, cw2, cbv, cb1, cb2, bias_ref,
                   wf_hi, wf_lo, wi_hi, wi_lo, o_ref):
    row = lax.broadcasted_iota(jnp.int32, (CTX, LANES), 0)

    def sconv(u_ref, cw, cb):
        u = u_ref[...]
        prv = jnp.where(row == 0, 0.0, pltpu.roll(u, 1, 0))
        nxt = jnp.where(row == CTX - 1, 0.0, pltpu.roll(u, CTX - 1, 0))
        return cw[0:1, :] * prv + cw[1:2, :] * u + cw[2:3, :] * nxt + cb[...]

    def pair(ra, rb, cw, cb):
        return jnp.concatenate([sconv(ra, cw, cb), sconv(rb, cw, cb)], axis=0)

    z = pair(va, vb, cwv, cbv)
    gates = (pair(x1a, x1b, cw1, cb1), pair(x2a, x2b, cw2, cb2))
    for o in range(2):
        y = _mm3(wf_hi[...], wf_lo[...], z)
        hb = h_ref[o]
        yr, yi = y[:CTX_N], y[CTX_N:]
        hr, hi = hb[:CTX_N], hb[CTX_N:]
        prod = jnp.concatenate([yr * hr - yi * hi, yr * hi + yi * hr], axis=0)
        conv = _mm3(wi_hi[...], wi_lo[...], prod)
        z = gates[o] * (conv + bias_ref[o] * z)
    o_ref[0] = z[:CTX]
    o_ref[1] = z[CTX:]


def _hy_ctx(hy, hspec, conv_w, conv_b, bias, cst):
    nct = HY_W // LANES
    base = R_LAT // CTX

    def blk(j, part):
        return pl.BlockSpec((CTX, LANES), lambda c, p: (base + 2 * p + j, part * nct + c))

    cw = lambda part: pl.BlockSpec((3, LANES), lambda c, p: (0, part * nct + c))
    cb = lambda part: pl.BlockSpec((1, LANES), lambda c, p: (0, part * nct + c))
    consts = [cst["cwf_hi"], cst["cwf_lo"], cst["cwi_hi"], cst["cwi_lo"]]
    return pl.pallas_call(
        _hy_ctx_kernel,
        grid=(nct, NB // 2),
        in_specs=[blk(0, 0), blk(1, 0), blk(0, 1), blk(1, 1), blk(0, 2), blk(1, 2),
                  pl.BlockSpec((2, None, 2 * CTX_N, LANES), lambda c, p: (0, c, 0, 0)),
                  cw(0), cw(1), cw(2), cb(0), cb(1), cb(2),
                  pl.BlockSpec((2, 1, LANES), lambda c, p: (0, 0, c))]
        + [_const_spec(x.shape) for x in consts],
        out_specs=pl.BlockSpec((2, CTX, LANES), lambda c, p: (p, 0, c)),
        out_shape=jax.ShapeDtypeStruct((NB, CTX, HY_W), F32),
        compiler_params=_cparams(("arbitrary", "arbitrary")),
        name="hy_ctx",
    )(hy, hy, hy, hy, hy, hy, hspec, conv_w, conv_w, conv_w, conv_b, conv_b, conv_b, bias, *consts)


def _stack(c):
    return np.block([[c.real, -c.imag], [c.imag, c.real]])


def _hilo(x):
    x = jnp.asarray(np.asarray(x, np.float32))
    return _split(x)


def _fft_constants():
    cst = {}
    k1 = np.arange(N1)[:, None]
    ca = np.exp(-2j * np.pi * k1 * np.arange(N1 // 2)[None, :] / N1)
    cst["wa_hi"], cst["wa_lo"] = _hilo(_stack(ca))
    ct = np.exp(-2j * np.pi * k1 * np.arange(N1)[None, :] / N1)
    cst["wt_hi"], cst["wt_lo"] = _hilo(np.concatenate([ct.real, ct.imag], axis=0))
    e = np.exp(-2j * np.pi * np.arange(N2)[:, None] * np.arange(N2)[None, :] / N2)
    cst["p"] = jnp.asarray(_stack(e), F32)
    cst["q"] = jnp.asarray(_stack(1j * e), F32)
    tw = np.exp(-2j * np.pi * k1 * np.arange(N2)[None, :] / FFT_N)
    cst["tr2"] = jnp.asarray(np.concatenate([tw.real, tw.real], axis=1), F32)
    cst["ti2"] = jnp.asarray(np.concatenate([tw.imag, tw.imag], axis=1), F32)
    cst["g_hi"], cst["g_lo"] = _hilo(_stack(np.conj(e).T))
    bi = np.exp(2j * np.pi * np.arange(N1 // 2)[:, None] * np.arange(N1)[None, :] / N1) / FFT_N
    cst["pi"] = jnp.asarray(_stack(bi), F32)
    cst["qi"] = jnp.asarray(_stack(1j * bi), F32)
    twi = np.exp(2j * np.pi * np.arange(N2)[:, None] * np.arange(N1)[None, :] / FFT_N)
    cst["tri2"] = jnp.asarray(np.concatenate([twi.real, twi.real], axis=1), F32)
    cst["tii2"] = jnp.asarray(np.concatenate([twi.imag, twi.imag], axis=1), F32)
    kk = np.arange(CTX_N)[:, None]
    cf = np.exp(-2j * np.pi * kk * np.arange(CTX)[None, :] / CTX_N)
    cst["cwf_hi"], cst["cwf_lo"] = _hilo(_stack(cf))
    cfull = np.exp(-2j * np.pi * kk * np.arange(CTX_N)[None, :] / CTX_N)
    cst["cwt_hi"], cst["cwt_lo"] = _hilo(np.concatenate([cfull.real, cfull.imag], axis=0))
    ci = np.exp(2j * np.pi * np.arange(CTX)[:, None] * np.arange(CTX_N)[None, :] / CTX_N) / CTX_N
    cst["cwi_hi"], cst["cwi_lo"] = _hilo(_stack(ci))
    return cst


def _filter_positions(L, permute):
    bands = (HY_EMB - 1) // 2
    t = jnp.linspace(0.0, 1.0, L, dtype=F32)[:, None]
    w = (2.0 * math.pi / L) * jnp.arange(L, dtype=F32)[:, None]
    f = jnp.linspace(1e-4, bands - 1, bands, dtype=F32)[None, :]
    z = jnp.concatenate([t, jnp.cos(f * w), -jnp.sin(f * w)], axis=-1)
    n = np.arange(2 * L)
    pos = np.where(n < L, n, 2 * L - n)
    pos = np.where(n == L, 0, pos)
    fsel = (n < L).astype(np.float32)
    bsel = (n > L).astype(np.float32)
    if permute:
        order = (np.arange(N1)[None, :] * N2 + np.arange(N2)[:, None]).reshape(-1)
        pos, fsel, bsel = pos[order], fsel[order], bsel[order]
    feats = jnp.pad(z[pos], ((0, 0), (0, HY_HID - HY_EMB)))
    rc = jnp.concatenate([t[pos], jnp.asarray(fsel)[:, None], jnp.asarray(bsel)[:, None],
                          jnp.zeros((2 * L, 5), F32)], axis=1)
    return feats, rc


def _decay_rates():
    max_decay = math.log(HY_TARGET) / HY_FAST
    min_decay = math.log(HY_TARGET) / HY_SLOW
    return jnp.abs(jnp.linspace(min_decay, max_decay, HY_W, dtype=F32))[None, :]


def _gla_constants():
    r = np.arange(GLA_BLK)
    same = (r[:, None] // CHUNK) == (r[None, :] // CHUNK)
    fwd = same & (r[:, None] >= r[None, :])
    bwd = same & (r[:, None] <= r[None, :])
    tri = jnp.asarray(np.stack([fwd, bwd]).astype(np.float32)).astype(BF16)
    ones = jnp.asarray(same.astype(np.float32)).astype(BF16)
    return tri, ones


def kernel(x, c, ctx, c_ctx, ada_w, ada_b, ffn1_norm_g, ffn1_w_gate, ffn1_w_up, ffn1_w_down, mix_norm_g, w_in,
           gla_wa_f, gla_ba_f, gla_wa_b, gla_ba_b, gla_norm_g, w_gla_out, hy_conv_w, hy_conv_b, hy_f_w1, hy_f_b1,
           hy_f_freq1, hy_f_w2, hy_f_b2, hy_f_freq2, hy_f_w3, hy_bias, w_hy_out, w_out, ffn2_norm_g, ffn2_w_gate,
           ffn2_w_up, ffn2_w_down, final_norm_g):
    cst = _fft_constants()
    tri, ones = _gla_constants()
    delta = _decay_rates()
    feats_lat, rc_lat = _filter_positions(SEQ, True)
    feats_ctx, rc_ctx = _filter_positions(CTX, False)

    cvec = jnp.concatenate([c, c_ctx[None, :], jnp.zeros((3, D), F32)], axis=0)
    mods = _mods(cvec, ada_w, ada_b).reshape(DEPTH, 8, 1, N_MOD * D)
    xa = jnp.concatenate([x.reshape(R_LAT, D), ctx.reshape(R_CTX, D)], axis=0)

    bf = lambda w: w.astype(BF16)
    zpad = jnp.zeros((LOWRANK, KEY_W), F32)
    for i in range(DEPTH):
        last = i == DEPTH - 1
        xa = _ffn(xa, mods, i, 0, ffn1_norm_g[i], bf(ffn1_w_gate[i]), bf(ffn1_w_up[i]), bf(ffn1_w_down[i]), R_ALL)

        wi = w_in[i]
        wkvq = bf(jnp.concatenate([wi[:, OFF_K:OFF_A], wi[:, OFF_Q:OFF_R]], axis=1))
        wa_hi, wa_lo = _split(wi[:, OFF_A:OFF_Q])
        kvq, a, r, hy, gt = _proj(xa, mods, i, mix_norm_g[i], wkvq, wa_hi, wa_lo, bf(wi[:, OFF_R:OFF_H]),
                                  bf(wi[:, OFF_H:OFF_G]), bf(wi[:, OFF_G:IN_W]))

        wa_cat = jnp.stack([jnp.concatenate([gla_wa_f[i], zpad], axis=0),
                            jnp.concatenate([zpad, gla_wa_b[i]], axis=0)])
        ba_cat = jnp.stack([gla_ba_f[i], gla_ba_b[i]])[:, None, :]
        o2 = _gla(kvq, a, wa_cat, ba_cat, tri, ones)

        w1p = jnp.pad(hy_f_w1[i], ((0, HY_HID - HY_EMB), (0, 0)))
        mlp = (w1p, hy_f_b1[i], hy_f_freq1[i], hy_f_w2[i], hy_f_b2[i], hy_f_freq2[i])
        bias = hy_bias[i][:, None, :]
        spec_lat = _spec_lat(_filt_mlp(feats_lat, *mlp), rc_lat, hy_f_w3[i], delta, cst)
        z_lat = _hy_lat(hy, spec_lat, hy_conv_w[i], hy_conv_b[i][None, :], bias, cst).reshape(R_LAT, HY_W)

        rows = R_LAT if last else R_ALL
        if last:
            z3 = z_lat
        else:
            spec_ctx = _spec_ctx(_filt_mlp(feats_ctx, *mlp), rc_ctx, hy_f_w3[i], delta, cst)
            z_ctx = _hy_ctx(hy, spec_ctx, hy_conv_w[i], hy_conv_b[i][None, :], bias, cst).reshape(R_CTX, HY_W)
            z3 = jnp.concatenate([z_lat, z_ctx], axis=0)
        xa = _mix(xa, mods, i, o2, r, z3, gt, gla_norm_g[i], bf(w_gla_out[i]), bf(w_hy_out[i]), bf(w_out[i]), rows)
        xa = _ffn(xa, mods, i, 2, ffn2_norm_g[i], bf(ffn2_w_gate[i]), bf(ffn2_w_up[i]), bf(ffn2_w_down[i]), rows,
                  final_g=final_norm_g if last else None)
    return xa.reshape(NB, SEQ, D)
```

```python
import functools
import math

import numpy as np
import jax
import jax.numpy as jnp
from jax import lax
from jax.experimental import pallas as pl
from jax.experimental.pallas import tpu as pltpu

F32 = jnp.float32
BF16 = jnp.bfloat16

D = 1024
NB = 4
SEQ = 4096
DEPTH = 2
CTX = 256
GRID_W = 64
D_FF = 2816
HALF_STEP = 0.5
N_MOD = 9
EPS = 1e-6

HEADS = 4
DK = 128
DV = 256
LOWRANK = 16
TAU = 16.0
CHUNK = 64
KEY_W = HEADS * DK
VAL_W = HEADS * DV

HY_W = 1024
HY_EMB = 33
HY_HID = 64
HY_FAST = 0.3
HY_SLOW = 1.5
HY_TARGET = 1e-2

OFF_K = 0
OFF_V = OFF_K + KEY_W
OFF_A = OFF_V + VAL_W
OFF_Q = OFF_A + 2 * LOWRANK
OFF_R = OFF_Q + KEY_W
OFF_H = OFF_R + VAL_W
OFF_G = OFF_H + 3 * HY_W
IN_W = OFF_G + 2 * D

R_LAT = NB * SEQ
R_CTX = NB * CTX
R_ALL = R_LAT + R_CTX

LANES = 128
VMEM_LIMIT = 56 * 1024 * 1024

FFT_N = 2 * SEQ
N1 = 128
N2 = 64
PITCH_A = 264
PITCH_D = 136
PITCH_U = 72
PITCH_Z = 136
UNROLL = 8
CTX_N = 2 * CTX


def _cparams(sem):
    return pltpu.CompilerParams(dimension_semantics=sem, vmem_limit_bytes=VMEM_LIMIT)


def _const_spec(shape):
    nd = len(shape)
    return pl.BlockSpec(shape, lambda *_: (0,) * nd, pipeline_mode=pl.Buffered(1))


def _split(x):
    hi = x.astype(BF16)
    lo = (x - hi.astype(F32)).astype(BF16)
    return hi, lo


def _dot(a, b):
    return jnp.dot(a, b, preferred_element_type=F32)


def _mm3(a_hi, a_lo, b):
    b_hi, b_lo = _split(b)
    return _dot(a_hi, b_hi) + _dot(a_lo, b_hi) + _dot(a_hi, b_lo)


def _mm3f(a, b):
    a_hi, a_lo = _split(a)
    return _mm3(a_hi, a_lo, b)


def _sigmoid(x):
    return 1.0 / (1.0 + jnp.exp(-x))


def _rms(x, g):
    ms = jnp.mean(x * x, axis=-1, keepdims=True)
    return x * lax.rsqrt(ms + EPS) * g


def _mods_kernel(c_ref, w_ref, b_ref, o_ref):
    c = c_ref[...]
    s = (c * _sigmoid(c)).astype(BF16)
    o_ref[...] = _dot(s, w_ref[...].astype(BF16)) + b_ref[...]


def _mods(cvec, ada_w, ada_b):
    tn = 1024
    return pl.pallas_call(
        _mods_kernel,
        grid=(DEPTH, N_MOD * D // tn),
        in_specs=[
            pl.BlockSpec((8, D), lambda l, j: (0, 0)),
            pl.BlockSpec((None, D, tn), lambda l, j: (l, 0, j)),
            pl.BlockSpec((None, 1, tn), lambda l, j: (l, 0, j)),
        ],
        out_specs=pl.BlockSpec((None, 8, tn), lambda l, j: (l, 0, j)),
        out_shape=jax.ShapeDtypeStruct((DEPTH, 8, N_MOD * D), F32),
        compiler_params=_cparams(("arbitrary", "arbitrary")),
        name="mods",
    )(cvec, ada_w, ada_b.reshape(DEPTH, 1, N_MOD * D))


def _mod_row(t, tm):
    lat_tiles = R_LAT // tm
    per_batch = SEQ // tm
    return jnp.where(t < lat_tiles, t // per_batch, NB)


def _ffn_kernel(x_ref, m_ref, g_ref, wg_ref, wu_ref, wd_ref, *rest, final):
    if final:
        fg_ref, o_ref = rest
    else:
        (o_ref,) = rest
    x = x_ref[...]
    m = m_ref[...]
    shift, scale, gate = m[:, :D], m[:, D:2 * D], m[:, 2 * D:]
    h = (_rms(x, g_ref[...]) * (1.0 + scale) + shift).astype(BF16)
    a = _dot(h, wg_ref[...])
    u = _dot(h, wu_ref[...])
    act = (a * _sigmoid(a) * u).astype(BF16)
    y = _dot(act, wd_ref[...])
    out = x + (HALF_STEP * gate) * y
    if final:
        out = _rms(out, fg_ref[...])
    o_ref[...] = out


def _ffn(x, mods, layer, slot, norm_g, wg, wu, wd, rows, final_g=None):
    tm = 512
    final = final_g is not None
    in_specs = [
        pl.BlockSpec((tm, D), lambda t: (t, 0)),
        pl.BlockSpec((None, None, 1, 3 * D), lambda t: (layer, _mod_row(t, tm), 0, slot)),
        _const_spec((1, D)),
        _const_spec((D, D_FF)),
        _const_spec((D, D_FF)),
        _const_spec((D_FF, D)),
    ]
    args = [x, mods, norm_g.reshape(1, D), wg, wu, wd]
    if final:
        in_specs.append(_const_spec((1, D)))
        args.append(final_g.reshape(1, D))
    return pl.pallas_call(
        functools.partial(_ffn_kernel, final=final),
        grid=(rows // tm,),
        in_specs=in_specs,
        out_specs=pl.BlockSpec((tm, D), lambda t: (t, 0)),
        out_shape=jax.ShapeDtypeStruct((rows, D), F32),
        compiler_params=_cparams(("arbitrary",)),
        name="ffn",
    )(*args)


def _proj_kernel(x_ref, m_ref, g_ref, wkvq_ref, wahi_ref, walo_ref, wr_ref, wh_ref, wg_ref,
                 kvq_ref, a_ref, r_ref, hy_ref, gt_ref):
    x = x_ref[...]
    m = m_ref[...]
    shift, scale = m[:, :D], m[:, D:2 * D]
    h = _rms(x, g_ref[...]) * (1.0 + scale) + shift
    hb = h.astype(BF16)
    kvq_ref[...] = _dot(hb, wkvq_ref[...]).astype(kvq_ref.dtype)
    hl = (h - hb.astype(F32)).astype(BF16)
    a_ref[...] = _dot(hb, wahi_ref[...]) + _dot(hl, wahi_ref[...]) + _dot(hb, walo_ref[...])
    r_ref[...] = _dot(hb, wr_ref[...]).astype(r_ref.dtype)
    hy_ref[...] = _dot(hb, wh_ref[...]).astype(hy_ref.dtype)
    gt_ref[...] = _dot(hb, wg_ref[...]).astype(gt_ref.dtype)


def _proj(x, mods, layer, norm_g, wkvq, wa_hi, wa_lo, wr, wh, wg):
    tm = 256
    nk = 2 * KEY_W + VAL_W
    na = 2 * LOWRANK
    row = lambda w: pl.BlockSpec((tm, w), lambda t: (t, 0))
    return pl.pallas_call(
        _proj_kernel,
        grid=(R_ALL // tm,),
        in_specs=[
            row(D),
            pl.BlockSpec((None, None, 1, 3 * D), lambda t: (layer, _mod_row(t, tm), 0, 1)),
            _const_spec((1, D)),
            _const_spec((D, nk)),
            _const_spec((D, na)),
            _const_spec((D, na)),
            _const_spec((D, VAL_W)),
            _const_spec((D, 3 * HY_W)),
            _const_spec((D, 2 * D)),
        ],
        out_specs=[row(nk), row(na), row(VAL_W), row(3 * HY_W), row(2 * D)],
        out_shape=[
            jax.ShapeDtypeStruct((R_ALL, nk), BF16),
            jax.ShapeDtypeStruct((R_ALL, na), F32),
            jax.ShapeDtypeStruct((R_ALL, VAL_W), BF16),
            jax.ShapeDtypeStruct((R_ALL, 3 * HY_W), BF16),
            jax.ShapeDtypeStruct((R_ALL, 2 * D), BF16),
        ],
        compiler_params=_cparams(("arbitrary",)),
        name="proj",
    )(x, mods, norm_g.reshape(1, D), wkvq, wa_hi, wa_lo, wr, wh, wg)


GLA_BLK = 256
GLA_STEPS = 1 + SEQ // GLA_BLK


def _gla_row_block(b, d, s):
    lat = jnp.where(d == 0, s - 1, SEQ // GLA_BLK - s)
    return jnp.where(s == 0, R_LAT // GLA_BLK + b, b * (SEQ // GLA_BLK) + lat)


def _gla_body(kvq_ref, a_ref, wa_ref, ba_ref, tri_ref, ones_ref, o_ref, st_ref, qd_s, kn_s, ke_s, dec_s):
    d = pl.program_id(1)
    s = pl.program_id(2)

    @pl.when(s == 0)
    def _():
        st_ref[...] = jnp.zeros_like(st_ref)

    a = a_ref[...]
    pre = _mm3f(a, wa_ref[...]) + ba_ref[...]
    la = (jnp.minimum(pre, 0.0) - jnp.log(1.0 + jnp.exp(-jnp.abs(pre)))) * (1.0 / TAU)
    la_hi, la_lo = _split(la)
    tri = tri_ref[...]
    ones = ones_ref[...]
    bcum = _dot(tri, la_hi) + _dot(tri, la_lo)
    btot = _dot(ones, la_hi) + _dot(ones, la_lo)

    k = kvq_ref[:, :KEY_W].astype(F32)
    q = kvq_ref[:, KEY_W + VAL_W:].astype(F32)
    qd_s[...] = (q * (DK ** -0.5) * jnp.exp(bcum)).astype(BF16)
    kn_s[...] = (k * jnp.exp(-bcum)).astype(BF16)
    ke_s[...] = (k * jnp.exp(btot - bcum)).astype(BF16)
    dec_s[...] = jnp.exp(btot)

    ri = lax.broadcasted_iota(jnp.int32, (CHUNK, CHUNK), 0)
    ci = lax.broadcasted_iota(jnp.int32, (CHUNK, CHUNK), 1)
    seen = (ri - ci) * (1 - 2 * d) >= 0

    nchunk = GLA_BLK // CHUNK
    for j in range(nchunk):
        r0 = pl.multiple_of(jnp.where(d == 0, j, nchunk - 1 - j) * CHUNK, CHUNK)
        rows = pl.ds(r0, CHUNK)
        for h in range(HEADS):
            ksl = slice(h * DK, (h + 1) * DK)
            qd = qd_s[rows, ksl]
            kn = kn_s[rows, ksl]
            ke = ke_s[rows, ksl]
            v = kvq_ref[rows, KEY_W + h * DV:KEY_W + (h + 1) * DV]
            dec = dec_s[pl.ds(r0, 1), ksl]
            st = st_ref[h]
            att = lax.dot_general(qd, kn, (((1,), (1,)), ((), ())), preferred_element_type=F32)
            att = jnp.where(seen, att, 0.0).astype(BF16)
            o = _dot(att, v) + lax.dot_general(qd, st.astype(BF16), (((1,), (1,)), ((), ())),
                                               preferred_element_type=F32)
            o_ref[rows, h * DV:(h + 1) * DV] = o
            upd = lax.dot_general(v, ke, (((0,), (0,)), ((), ())), preferred_element_type=F32)
            st_ref[h] = st * dec + upd


def _gla(kvq, a, wa_cat, ba_cat, tri, ones):
    nk = 2 * KEY_W + VAL_W
    row_map = lambda b, d, s: (_gla_row_block(b, d, s), 0)
    return pl.pallas_call(
        _gla_body,
        grid=(NB, 2, GLA_STEPS),
        in_specs=[
            pl.BlockSpec((GLA_BLK, nk), row_map),
            pl.BlockSpec((GLA_BLK, 2 * LOWRANK), row_map),
            pl.BlockSpec((None, 2 * LOWRANK, KEY_W), lambda b, d, s: (d, 0, 0)),
            pl.BlockSpec((None, 1, KEY_W), lambda b, d, s: (d, 0, 0)),
            pl.BlockSpec((None, GLA_BLK, GLA_BLK), lambda b, d, s: (d, 0, 0)),
            pl.BlockSpec((GLA_BLK, GLA_BLK), lambda b, d, s: (0, 0)),
        ],
        out_specs=pl.BlockSpec((None, GLA_BLK, VAL_W), lambda b, d, s: (d, _gla_row_block(b, d, s), 0)),
        out_shape=jax.ShapeDtypeStruct((2, R_ALL, VAL_W), F32),
        scratch_shapes=[
            pltpu.VMEM((HEADS, DV, DK), F32),
            pltpu.VMEM((GLA_BLK, KEY_W), BF16),
            pltpu.VMEM((GLA_BLK, KEY_W), BF16),
            pltpu.VMEM((GLA_BLK, KEY_W), BF16),
            pltpu.VMEM((GLA_BLK, KEY_W), F32),
        ],
        compiler_params=_cparams(("arbitrary", "arbitrary", "arbitrary")),
        name="gla",
    )(kvq, a, wa_cat, ba_cat, tri, ones)


def _mix_kernel(x_ref, m_ref, of_ref, ob_ref, r_ref, z_ref, gt_ref, ng_ref, wgo_ref, who_ref, wo_ref, o_ref):
    x = x_ref[...]
    gate = m_ref[...][:, 2 * D:]
    o = of_ref[...] + ob_ref[...]
    ng = ng_ref[...]
    parts = []
    for h in range(HEADS):
        sl = slice(h * DV, (h + 1) * DV)
        oh = o[:, sl]
        ms = jnp.mean(oh * oh, axis=-1, keepdims=True)
        parts.append(oh * lax.rsqrt(ms + EPS) * ng[:, sl])
    on = jnp.concatenate(parts, axis=-1)
    r = r_ref[...].astype(F32)
    y_gla = _dot((on * (r * _sigmoid(r))).astype(BF16), wgo_ref[...])
    y_hy = _dot(z_ref[...].astype(BF16), who_ref[...])
    gt = gt_ref[...].astype(F32)
    mix = _sigmoid(gt[:, :D]) * y_gla + _sigmoid(gt[:, D:]) * y_hy
    o_ref[...] = x + gate * _dot(mix.astype(BF16), wo_ref[...])


def _mix(x, mods, layer, o2, r, z3, gt, norm_g, wgo, who, wo, rows):
    tm = 512
    row = lambda w: pl.BlockSpec((tm, w), lambda t: (t, 0))
    return pl.pallas_call(
        _mix_kernel,
        grid=(rows // tm,),
        in_specs=[
            row(D),
            pl.BlockSpec((None, None, 1, 3 * D), lambda t: (layer, _mod_row(t, tm), 0, 1)),
            pl.BlockSpec((None, tm, VAL_W), lambda t: (0, t, 0)),
            pl.BlockSpec((None, tm, VAL_W), lambda t: (1, t, 0)),
            row(VAL_W),
            row(HY_W),
            row(2 * D),
            _const_spec((1, VAL_W)),
            _const_spec((VAL_W, D)),
            _const_spec((HY_W, D)),
            _const_spec((D, D)),
        ],
        out_specs=row(D),
        out_shape=jax.ShapeDtypeStruct((rows, D), F32),
        compiler_params=_cparams(("arbitrary",)),
        name="mix",
    )(x, mods, o2, o2, r, z3, gt, norm_g.reshape(1, VAL_W), wgo, who, wo)


def _hp_dot(a, b):
    return jnp.dot(a, b, preferred_element_type=F32, precision=lax.Precision.HIGHEST)


def _filt_mlp_kernel(z_ref, w1_ref, b1_ref, f1_ref, w2_ref, b2_ref, f2_ref, o_ref):
    h = jnp.sin(f1_ref[...] * (_hp_dot(z_ref[...], w1_ref[...]) + b1_ref[...]))
    o_ref[...] = jnp.sin(f2_ref[...] * (_hp_dot(h, w2_ref[...]) + b2_ref[...]))


def _filt_mlp(feats, w1, b1, f1, w2, b2, f2):
    rows = feats.shape[0]
    tm = 512
    vec = lambda: _const_spec((1, HY_HID))
    return pl.pallas_call(
        _filt_mlp_kernel,
        grid=(rows // tm,),
        in_specs=[pl.BlockSpec((tm, HY_HID), lambda t: (t, 0)), _const_spec((HY_HID, HY_HID)), vec(), vec(),
                  _const_spec((HY_HID, HY_HID)), vec(), vec()],
        out_specs=pl.BlockSpec((tm, HY_HID), lambda t: (t, 0)),
        out_shape=jax.ShapeDtypeStruct((rows, HY_HID), F32),
        compiler_params=_cparams(("arbitrary",)),
        name="filt_mlp",
    )(feats, w1, b1.reshape(1, HY_HID), f1.reshape(1, HY_HID), w2, b2.reshape(1, HY_HID), f2.reshape(1, HY_HID))


def _taps_chunk(h2, rc, w3f, w3b, delta):
    t, fsel, bsel = rc[:, 0:1], rc[:, 1:2], rc[:, 2:3]
    win = jnp.exp(-t * delta)
    return win * (fsel * _hp_dot(h2, w3f) + bsel * _hp_dot(h2, w3b))


def _stage_b_fwd(buf_a, j, e_ref):
    cols = []
    for k1 in (2 * j, 2 * j + 1):
        re = buf_a[pl.ds(k1, N2, stride=PITCH_A), :]
        im = buf_a[pl.ds(N1 + k1, N2, stride=PITCH_A), :]
        cols.append(jnp.concatenate([re, im], axis=0).astype(BF16))
    return _dot(e_ref[...], jnp.concatenate(cols, axis=1))


def _spec_lat_kernel(h2_ref, rc_ref, w3f_ref, w3b_ref, dl_ref, wt_ref, e_ref, o_ref, taps, buf_a):
    tb = 512
    delta = dl_ref[...]

    def fill(i, acc):
        rows = pl.ds(pl.multiple_of(i * tb, tb), tb)
        tp = _taps_chunk(h2_ref[rows, :], rc_ref[rows, :], w3f_ref[...], w3b_ref[...], delta)
        taps[rows, :] = tp
        return acc + jnp.sum(jnp.abs(tp), axis=0, keepdims=True)

    l1 = lax.fori_loop(0, FFT_N // tb, fill, jnp.zeros((1, LANES), F32))
    inv = 1.0 / (l1 + EPS)

    def stage_a(n2, c):
        rhs = taps[pl.ds(pl.multiple_of(n2 * N1, N1), N1), :]
        buf_a[pl.ds(pl.multiple_of(n2 * PITCH_A, 8), 2 * N1), :] = _dot(wt_ref[n2], rhs.astype(BF16))
        return c

    lax.fori_loop(0, N2, stage_a, 0, unroll=UNROLL)

    def stage_b(j, c):
        y = _stage_b_fwd(buf_a, j, e_ref)
        rows = 2 * N2
        o_ref[pl.ds(pl.multiple_of(2 * j * rows, 2 * rows), rows), :] = y[:, :LANES] * inv
        o_ref[pl.ds(pl.multiple_of(2 * j * rows + rows, rows), rows), :] = y[:, LANES:] * inv
        return c

    lax.fori_loop(0, N1 // 2, stage_b, 0, unroll=UNROLL)


def _spec_lat(h2p, rcp, w3, delta, cst):
    nct = HY_W // LANES
    return pl.pallas_call(
        _spec_lat_kernel,
        grid=(2, nct),
        in_specs=[
            _const_spec((FFT_N, HY_HID)),
            _const_spec((FFT_N, 8)),
            pl.BlockSpec((HY_HID, LANES), lambda o, c: (0, o * nct + c)),
            pl.BlockSpec((HY_HID, LANES), lambda o, c: (0, 2 * nct + o * nct + c)),
            pl.BlockSpec((1, LANES), lambda o, c: (0, c)),
            _const_spec((N2, 2 * N1, N1)),
            _const_spec((2 * N2, 2 * N2)),
        ],
        out_specs=pl.BlockSpec((None, None, 2 * FFT_N, LANES), lambda o, c: (o, c, 0, 0)),
        out_shape=jax.ShapeDtypeStruct((2, nct, 2 * FFT_N, LANES), F32),
        scratch_shapes=[pltpu.VMEM((FFT_N, LANES), F32), pltpu.VMEM((N2 * PITCH_A, LANES), F32)],
        compiler_params=_cparams(("arbitrary", "arbitrary")),
        name="spec_lat",
    )(h2p, rcp, w3, w3, delta, cst["wt"], cst["e"])


def _hy_lat_kernel(ua_ref, ub_ref, h_ref, cw_ref, cb_ref, bias_ref, wa_ref, e_ref, g_ref, mi_ref,
                   o_ref, zin, buf_a, buf_d, upad):
    s = pl.program_id(2)

    def zslab(n2):
        return zin[pl.ds(pl.multiple_of(n2 * PITCH_Z, 8), 2 * N2), :]

    def repitch(n1, c):
        src = pl.ds(pl.multiple_of(n1 * GRID_W, GRID_W), GRID_W)
        dst = pl.ds(pl.multiple_of(n1 * PITCH_U, 8), GRID_W)
        upad[0, dst, :] = ua_ref[src, :].astype(F32)
        upad[1, dst, :] = ub_ref[src, :].astype(F32)
        return c
    lax.fori_loop(0, N2, repitch, 0, unroll=UNROLL)

    def conv_slices(n2):
        w0, w1, w2 = cw_ref[0:1, :], cw_ref[1:2, :], cw_ref[2:3, :]
        cb = cb_ref[...]
        nm = jnp.maximum(n2 - 1, 0)
        nx = jnp.minimum(n2 + 1, GRID_W - 1)
        outs = []
        for j in range(2):
            cur = upad[j, pl.ds(n2, N2, stride=PITCH_U), :]
            prv = jnp.where(n2 > 0, upad[j, pl.ds(nm, N2, stride=PITCH_U), :], 0.0)
            nxt = jnp.where(n2 < GRID_W - 1, upad[j, pl.ds(nx, N2, stride=PITCH_U), :], 0.0)
            outs.append(w0 * prv + w1 * cur + w2 * nxt + cb)
        return jnp.concatenate(outs, axis=0)

    @pl.when(s == 0)
    def _():
        def body(n2, c):
            zin[pl.ds(pl.multiple_of(n2 * PITCH_Z, 8), 2 * N2), :] = conv_slices(n2)
            return c
        lax.fori_loop(0, N2, body, 0, unroll=UNROLL)

    @pl.when(s > 0)
    def _():
        def stage_a(n2, c):
            buf_a[pl.ds(pl.multiple_of(n2 * PITCH_A, 8), 2 * N1), :] = _dot(wa_ref[n2], zslab(n2).astype(BF16))
            return c
        lax.fori_loop(0, N2, stage_a, 0, unroll=UNROLL)

        def stage_b(j, c):
            y = _stage_b_fwd(buf_a, j, e_ref)
            prods = []
            for i in range(2):
                yk = y[:, i * LANES:(i + 1) * LANES]
                hb = h_ref[pl.ds(pl.multiple_of((2 * j + i) * 2 * N2, 2 * N2), 2 * N2), :]
                yr, yi = yk[:N2], yk[N2:]
                hr, hi = hb[:N2], hb[N2:]
                prods.append(jnp.concatenate([yr * hr - yi * hi, yr * hi + yi * hr], axis=0).astype(BF16))
            d = _dot(g_ref[...], jnp.concatenate(prods, axis=1))
            for i in range(2):
                rows = pl.ds(pl.multiple_of((2 * j + i) * PITCH_D, 8), 2 * N2)
                buf_d[rows, :] = d[:, i * LANES:(i + 1) * LANES]
            return c
        lax.fori_loop(0, N1 // 2, stage_b, 0, unroll=UNROLL)

        def stage_ai(n2, c):
            re = buf_d[pl.ds(n2, N1, stride=PITCH_D), :]
            im = buf_d[pl.ds(N2 + n2, N1, stride=PITCH_D), :]
            x = jnp.concatenate([re, im], axis=0).astype(BF16)
            y = _dot(mi_ref[n2], x)
            rows = pl.ds(pl.multiple_of(n2 * PITCH_Z, 8), 2 * N2)
            zin[rows, :] = conv_slices(n2) * (y + bias_ref[...] * zin[rows, :])
            return c
        lax.fori_loop(0, N2, stage_ai, 0, unroll=UNROLL)

    @pl.when(s == 2)
    def _():
        def emit(n1, c):
            dst = pl.ds(pl.multiple_of(n1 * GRID_W, GRID_W), GRID_W)
            o_ref[0, dst, :] = zin[pl.ds(n1, GRID_W, stride=PITCH_Z), :].astype(o_ref.dtype)
            o_ref[1, dst, :] = zin[pl.ds(N2 + n1, GRID_W, stride=PITCH_Z), :].astype(o_ref.dtype)
            return c
        lax.fori_loop(0, N2, emit, 0, unroll=UNROLL)


def _hy_lat(hy, hspec, conv_w, conv_b, bias, cst):
    nct = HY_W // LANES
    part = lambda c, p, s: s * nct + c
    order = lambda s: jnp.maximum(s - 1, 0)
    consts = [cst[n] for n in ("wa", "e", "g", "mi")]
    return pl.pallas_call(
        _hy_lat_kernel,
        grid=(nct, NB // 2, 3),
        in_specs=[
            pl.BlockSpec((SEQ, LANES), lambda c, p, s: (2 * p, part(c, p, s))),
            pl.BlockSpec((SEQ, LANES), lambda c, p, s: (2 * p + 1, part(c, p, s))),
            pl.BlockSpec((None, None, 2 * FFT_N, LANES), lambda c, p, s: (order(s), c, 0, 0),
                         pipeline_mode=pl.Buffered(1)),
            pl.BlockSpec((3, LANES), lambda c, p, s: (0, part(c, p, s))),
            pl.BlockSpec((1, LANES), lambda c, p, s: (0, part(c, p, s))),
            pl.BlockSpec((None, 1, LANES), lambda c, p, s: (order(s), 0, c)),
        ] + [_const_spec(x.shape) for x in consts],
        out_specs=pl.BlockSpec((2, SEQ, LANES), lambda c, p, s: (p, 0, c)),
        out_shape=jax.ShapeDtypeStruct((NB, SEQ, HY_W), BF16),
        scratch_shapes=[
            pltpu.VMEM((N2 * PITCH_Z, LANES), F32),
            pltpu.VMEM((N2 * PITCH_A, LANES), F32),
            pltpu.VMEM((N1 * PITCH_D, LANES), F32),
            pltpu.VMEM((2, N2 * PITCH_U, LANES), F32),
        ],
        compiler_params=_cparams(("arbitrary", "arbitrary", "arbitrary")),
        name="hy_lat",
    )(hy, hy, hspec, conv_w, conv_b, bias, *consts)


def _spec_ctx_kernel(h2_ref, rc_ref, w3f_ref, w3b_ref, dl_ref, wt_hi, wt_lo, o_ref):
    tp = _taps_chunk(h2_ref[...], rc_ref[...], w3f_ref[...], w3b_ref[...], dl_ref[...])
    l1 = jnp.sum(jnp.abs(tp), axis=0, keepdims=True)
    o_ref[...] = _mm3(wt_hi[...], wt_lo[...], tp * (1.0 / (l1 + EPS)))


def _spec_ctx(h2, rc, w3, delta, cst):
    nct = HY_W // LANES
    return pl.pallas_call(
        _spec_ctx_kernel,
        grid=(2, nct),
        in_specs=[
            _const_spec((CTX_N, HY_HID)),
            _const_spec((CTX_N, 8)),
            pl.BlockSpec((HY_HID, LANES), lambda o, c: (0, o * nct + c)),
            pl.BlockSpec((HY_HID, LANES), lambda o, c: (0, 2 * nct + o * nct + c)),
            pl.BlockSpec((1, LANES), lambda o, c: (0, c)),
            _const_spec((2 * CTX_N, CTX_N)), _const_spec((2 * CTX_N, CTX_N)),
        ],
        out_specs=pl.BlockSpec((None, None, 2 * CTX_N, LANES), lambda o, c: (o, c, 0, 0)),
        out_shape=jax.ShapeDtypeStruct((2, nct, 2 * CTX_N, LANES), F32),
        compiler_params=_cparams(("arbitrary", "arbitrary")),
        name="spec_ctx",
    )(h2, rc, w3, w3, delta, cst["cwt_hi"], cst["cwt_lo"])


def _hy_ctx_kernel(va, vb, x1a, x1b, x2a, x2b, h_ref, cwv, cw1, cw2, cbv, cb1, cb2, bias_ref,
                   wf_hi, wf_lo, wi_hi, wi_lo, o_ref):
    row = lax.broadcasted_iota(jnp.int32, (CTX, LANES), 0)

    def sconv(u_ref, cw, cb):
        u = u_ref[...].astype(F32)
        prv = jnp.where(row == 0, 0.0, pltpu.roll(u, 1, 0))
        nxt = jnp.where(row == CTX - 1, 0.0, pltpu.roll(u, CTX - 1, 0))
        return cw[0:1, :] * prv + cw[1:2, :] * u + cw[2:3, :] * nxt + cb[...]

    def pair(ra, rb, cw, cb):
        return jnp.concatenate([sconv(ra, cw, cb), sconv(rb, cw, cb)], axis=0)

    z = pair(va, vb, cwv, cbv)
    gates = (pair(x1a, x1b, cw1, cb1), pair(x2a, x2b, cw2, cb2))
    for o in range(2):
        y = _mm3(wf_hi[...], wf_lo[...], z)
        hb = h_ref[o]
        yr, yi = y[:CTX_N], y[CTX_N:]
        hr, hi = hb[:CTX_N], hb[CTX_N:]
        prod = jnp.concatenate([yr * hr - yi * hi, yr * hi + yi * hr], axis=0)
        conv = _mm3(wi_hi[...], wi_lo[...], prod)
        z = gates[o] * (conv + bias_ref[o] * z)
    o_ref[0] = z[:CTX].astype(o_ref.dtype)
    o_ref[1] = z[CTX:].astype(o_ref.dtype)


def _hy_ctx(hy, hspec, conv_w, conv_b, bias, cst):
    nct = HY_W // LANES
    base = R_LAT // CTX

    def blk(j, part):
        return pl.BlockSpec((CTX, LANES), lambda c, p: (base + 2 * p + j, part * nct + c))

    cw = lambda part: pl.BlockSpec((3, LANES), lambda c, p: (0, part * nct + c))
    cb = lambda part: pl.BlockSpec((1, LANES), lambda c, p: (0, part * nct + c))
    consts = [cst["cwf_hi"], cst["cwf_lo"], cst["cwi_hi"], cst["cwi_lo"]]
    return pl.pallas_call(
        _hy_ctx_kernel,
        grid=(nct, NB // 2),
        in_specs=[blk(0, 0), blk(1, 0), blk(0, 1), blk(1, 1), blk(0, 2), blk(1, 2),
                  pl.BlockSpec((2, None, 2 * CTX_N, LANES), lambda c, p: (0, c, 0, 0)),
                  cw(0), cw(1), cw(2), cb(0), cb(1), cb(2),
                  pl.BlockSpec((2, 1, LANES), lambda c, p: (0, 0, c))]
        + [_const_spec(x.shape) for x in consts],
        out_specs=pl.BlockSpec((2, CTX, LANES), lambda c, p: (p, 0, c)),
        out_shape=jax.ShapeDtypeStruct((NB, CTX, HY_W), BF16),
        compiler_params=_cparams(("arbitrary", "arbitrary")),
        name="hy_ctx",
    )(hy, hy, hy, hy, hy, hy, hspec, conv_w, conv_w, conv_w, conv_b, conv_b, conv_b, bias, *consts)


def _stack(c):
    return np.block([[c.real, -c.imag], [c.imag, c.real]])


def _hilo(x):
    x = jnp.asarray(np.asarray(x, np.float32))
    return _split(x)


def _fft_constants():
    cst = {}
    k1 = np.arange(N1)[:, None]
    ca = np.exp(-2j * np.pi * k1 * np.arange(N1 // 2)[None, :] / N1)
    tw = np.exp(-2j * np.pi * k1 * np.arange(N2)[None, :] / FFT_N)
    ct = np.exp(-2j * np.pi * k1 * np.arange(N1)[None, :] / N1)
    cst["wa"] = jnp.asarray(np.stack([_stack(tw[:, n2:n2 + 1] * ca) for n2 in range(N2)]), F32).astype(BF16)
    wt = [tw[:, n2:n2 + 1] * ct for n2 in range(N2)]
    cst["wt"] = jnp.asarray(np.stack([np.concatenate([m.real, m.imag], axis=0) for m in wt]), F32).astype(BF16)
    e = np.exp(-2j * np.pi * np.arange(N2)[:, None] * np.arange(N2)[None, :] / N2)
    cst["e"] = jnp.asarray(_stack(e), F32).astype(BF16)
    cst["g"] = jnp.asarray(_stack(np.conj(e).T), F32).astype(BF16)
    bi = np.exp(2j * np.pi * np.arange(N1 // 2)[:, None] * np.arange(N1)[None, :] / N1) / FFT_N
    cst["mi"] = jnp.asarray(np.stack([_stack(bi * np.conj(tw[:, n2])[None, :]) for n2 in range(N2)]),
                            F32).astype(BF16)
    kk = np.arange(CTX_N)[:, None]
    cf = np.exp(-2j * np.pi * kk * np.arange(CTX)[None, :] / CTX_N)
    cst["cwf_hi"], cst["cwf_lo"] = _hilo(_stack(cf))
    cfull = np.exp(-2j * np.pi * kk * np.arange(CTX_N)[None, :] / CTX_N)
    cst["cwt_hi"], cst["cwt_lo"] = _hilo(np.concatenate([cfull.real, cfull.imag], axis=0))
    ci = np.exp(2j * np.pi * np.arange(CTX)[:, None] * np.arange(CTX_N)[None, :] / CTX_N) / CTX_N
    cst["cwi_hi"], cst["cwi_lo"] = _hilo(_stack(ci))
    return cst


def _filter_positions(L, permute):
    bands = (HY_EMB - 1) // 2
    t = np.linspace(0.0, 1.0, L)[:, None]
    w = (2.0 * math.pi / L) * np.arange(L)[:, None]
    f = np.linspace(1e-4, bands - 1, bands)[None, :]
    z = np.concatenate([t, np.cos(f * w), -np.sin(f * w)], axis=-1)
    n = np.arange(2 * L)
    pos = np.where(n < L, n, 2 * L - n)
    pos = np.where(n == L, 0, pos)
    fsel = (n < L).astype(np.float32)
    bsel = (n > L).astype(np.float32)
    if permute:
        order = (np.arange(N1)[None, :] * N2 + np.arange(N2)[:, None]).reshape(-1)
        pos, fsel, bsel = pos[order], fsel[order], bsel[order]
    feats = np.pad(z[pos], ((0, 0), (0, HY_HID - HY_EMB)))
    rc = np.concatenate([t[pos], fsel[:, None], bsel[:, None], np.zeros((2 * L, 5))], axis=1)
    return jnp.asarray(feats, F32), jnp.asarray(rc, F32)


def _decay_rates():
    max_decay = math.log(HY_TARGET) / HY_FAST
    min_decay = math.log(HY_TARGET) / HY_SLOW
    return jnp.asarray(np.abs(np.linspace(min_decay, max_decay, HY_W))[None, :], F32)


def _gla_constants():
    r = np.arange(GLA_BLK)
    same = (r[:, None] // CHUNK) == (r[None, :] // CHUNK)
    fwd = same & (r[:, None] >= r[None, :])
    bwd = same & (r[:, None] <= r[None, :])
    tri = jnp.asarray(np.stack([fwd, bwd]).astype(np.float32)).astype(BF16)
    ones = jnp.asarray(same.astype(np.float32)).astype(BF16)
    return tri, ones


def kernel(x, c, ctx, c_ctx, ada_w, ada_b, ffn1_norm_g, ffn1_w_gate, ffn1_w_up, ffn1_w_down, mix_norm_g, w_in,
           gla_wa_f, gla_ba_f, gla_wa_b, gla_ba_b, gla_norm_g, w_gla_out, hy_conv_w, hy_conv_b, hy_f_w1, hy_f_b1,
           hy_f_freq1, hy_f_w2, hy_f_b2, hy_f_freq2, hy_f_w3, hy_bias, w_hy_out, w_out, ffn2_norm_g, ffn2_w_gate,
           ffn2_w_up, ffn2_w_down, final_norm_g):
    cst = _fft_constants()
    tri, ones = _gla_constants()
    delta = _decay_rates()
    feats_lat, rc_lat = _filter_positions(SEQ, True)
    feats_ctx, rc_ctx = _filter_positions(CTX, False)

    cvec = jnp.concatenate([c, c_ctx[None, :], jnp.zeros((3, D), F32)], axis=0)
    mods = _mods(cvec, ada_w, ada_b).reshape(DEPTH, 8, 1, N_MOD * D)
    xa = jnp.concatenate([x.reshape(R_LAT, D), ctx.reshape(R_CTX, D)], axis=0)

    bf = lambda w: w.astype(BF16)
    zpad = jnp.zeros((LOWRANK, KEY_W), F32)
    for i in range(DEPTH):
        last = i == DEPTH - 1
        xa = _ffn(xa, mods, i, 0, ffn1_norm_g[i], bf(ffn1_w_gate[i]), bf(ffn1_w_up[i]), bf(ffn1_w_down[i]), R_ALL)

        wi = w_in[i]
        wkvq = bf(jnp.concatenate([wi[:, OFF_K:OFF_A], wi[:, OFF_Q:OFF_R]], axis=1))
        wa_hi, wa_lo = _split(wi[:, OFF_A:OFF_Q])
        kvq, a, r, hy, gt = _proj(xa, mods, i, mix_norm_g[i], wkvq, wa_hi, wa_lo, bf(wi[:, OFF_R:OFF_H]),
                                  bf(wi[:, OFF_H:OFF_G]), bf(wi[:, OFF_G:IN_W]))

        wa_cat = jnp.stack([jnp.concatenate([gla_wa_f[i], zpad], axis=0),
                            jnp.concatenate([zpad, gla_wa_b[i]], axis=0)])
        ba_cat = jnp.stack([gla_ba_f[i], gla_ba_b[i]])[:, None, :]
        o2 = _gla(kvq, a, wa_cat, ba_cat, tri, ones)

        w1p = jnp.pad(hy_f_w1[i], ((0, HY_HID - HY_EMB), (0, 0)))
        mlp = (w1p, hy_f_b1[i], hy_f_freq1[i], hy_f_w2[i], hy_f_b2[i], hy_f_freq2[i])
        bias = hy_bias[i][:, None, :]
        spec_lat = _spec_lat(_filt_mlp(feats_lat, *mlp), rc_lat, hy_f_w3[i], delta, cst)
        z_lat = _hy_lat(hy, spec_lat, hy_conv_w[i], hy_conv_b[i][None, :], bias, cst).reshape(R_LAT, HY_W)

        rows = R_LAT if last else R_ALL
        if last:
            z3 = z_lat
        else:
            spec_ctx = _spec_ctx(_filt_mlp(feats_ctx, *mlp), rc_ctx, hy_f_w3[i], delta, cst)
            z_ctx = _hy_ctx(hy, spec_ctx, hy_conv_w[i], hy_conv_b[i][None, :], bias, cst).reshape(R_CTX, HY_W)
            z3 = jnp.concatenate([z_lat, z_ctx], axis=0)
        xa = _mix(xa, mods, i, o2, r, z3, gt, gla_norm_g[i], bf(w_gla_out[i]), bf(w_hy_out[i]), bf(w_out[i]), rows)
        xa = _ffn(xa, mods, i, 2, ffn2_norm_g[i], bf(ffn2_w_gate[i]), bf(ffn2_w_up[i]), bf(ffn2_w_down[i]), rows,
                  final_g=final_norm_g if last else None)
    return xa.reshape(NB, SEQ, D)
```

```python
import functools
import math

import numpy as np
import jax
import jax.numpy as jnp
from jax import lax
from jax.experimental import pallas as pl
from jax.experimental.pallas import tpu as pltpu

F32 = jnp.float32
BF16 = jnp.bfloat16

D = 1024
NB = 4
SEQ = 4096
DEPTH = 2
CTX = 256
GRID_W = 64
D_FF = 2816
HALF_STEP = 0.5
N_MOD = 9
EPS = 1e-6

HEADS = 4
DK = 128
DV = 256
LOWRANK = 16
TAU = 16.0
CHUNK = 64
KEY_W = HEADS * DK
VAL_W = HEADS * DV

HY_W = 1024
HY_EMB = 33
HY_HID = 64
HY_FAST = 0.3
HY_SLOW = 1.5
HY_TARGET = 1e-2

OFF_K = 0
OFF_V = OFF_K + KEY_W
OFF_A = OFF_V + VAL_W
OFF_Q = OFF_A + 2 * LOWRANK
OFF_R = OFF_Q + KEY_W
OFF_H = OFF_R + VAL_W
OFF_G = OFF_H + 3 * HY_W
IN_W = OFF_G + 2 * D

R_LAT = NB * SEQ
R_CTX = NB * CTX
R_ALL = R_LAT + R_CTX

LANES = 128
VMEM_LIMIT = 56 * 1024 * 1024

FFT_N = 2 * SEQ
N1 = 128
N2 = 64
PITCH_A = 264
PITCH_D = 136
PITCH_U = 72
PITCH_Z = 136
UNROLL = 8
CTX_N = 2 * CTX


def _cparams(sem):
    return pltpu.CompilerParams(dimension_semantics=sem, vmem_limit_bytes=VMEM_LIMIT)


def _const_spec(shape):
    nd = len(shape)
    return pl.BlockSpec(shape, lambda *_: (0,) * nd, pipeline_mode=pl.Buffered(1))


def _split(x):
    hi = x.astype(BF16)
    lo = (x - hi.astype(F32)).astype(BF16)
    return hi, lo


def _dot(a, b):
    return jnp.dot(a, b, preferred_element_type=F32)


def _sigmoid(x):
    return 1.0 / (1.0 + jnp.exp(-x))


def _rms(x, g):
    ms = jnp.mean(x * x, axis=-1, keepdims=True)
    return x * lax.rsqrt(ms + EPS) * g


def _mods_kernel(c_ref, w_ref, b_ref, o_ref):
    c = c_ref[...]
    s = (c * _sigmoid(c)).astype(BF16)
    o_ref[...] = _dot(s, w_ref[...].astype(BF16)) + b_ref[...]


def _mods(cvec, ada_w, ada_b):
    tn = 1024
    return pl.pallas_call(
        _mods_kernel,
        grid=(DEPTH, N_MOD * D // tn),
        in_specs=[
            pl.BlockSpec((8, D), lambda l, j: (0, 0)),
            pl.BlockSpec((None, D, tn), lambda l, j: (l, 0, j)),
            pl.BlockSpec((None, 1, tn), lambda l, j: (l, 0, j)),
        ],
        out_specs=pl.BlockSpec((None, 8, tn), lambda l, j: (l, 0, j)),
        out_shape=jax.ShapeDtypeStruct((DEPTH, 8, N_MOD * D), F32),
        compiler_params=_cparams(("arbitrary", "arbitrary")),
        name="mods",
    )(cvec, ada_w, ada_b.reshape(DEPTH, 1, N_MOD * D))


def _mod_row(t, tm):
    lat_tiles = R_LAT // tm
    per_batch = SEQ // tm
    return jnp.where(t < lat_tiles, t // per_batch, NB)


def _row_specs(tm, width, split):
    if not split:
        return [pl.BlockSpec((tm, width), lambda t: (t, 0))]
    lat_tiles = R_LAT // tm
    return [pl.BlockSpec((tm, width), lambda t: (jnp.minimum(t, lat_tiles - 1), 0)),
            pl.BlockSpec((tm, width), lambda t: (jnp.maximum(t - lat_tiles, 0), 0))]


def _row_tile(refs, tm):
    if len(refs) == 1:
        return refs[0][...]
    return jnp.where(pl.program_id(0) < R_LAT // tm, refs[0][...], refs[1][...])


def _ffn_kernel(*refs, final, nx, tm):
    xs, (m_ref, g_ref, wg_ref, wu_ref, wd_ref), rest = refs[:nx], refs[nx:nx + 5], refs[nx + 5:]
    if final:
        fg_ref, o_ref = rest
    else:
        (o_ref,) = rest
    x = _row_tile(xs, tm)
    m = m_ref[...]
    shift, scale, gate = m[:, :D], m[:, D:2 * D], m[:, 2 * D:]
    h = (_rms(x, g_ref[...]) * (1.0 + scale) + shift).astype(BF16)
    a = _dot(h, wg_ref[...])
    u = _dot(h, wu_ref[...])
    act = (a * _sigmoid(a) * u).astype(BF16)
    y = _dot(act, wd_ref[...])
    out = x + (HALF_STEP * gate) * y
    if final:
        out = _rms(out, fg_ref[...])
    o_ref[...] = out


def _ffn(xs, mods, layer, slot, norm_g, wg, wu, wd, rows, final_g=None):
    tm = 512
    final = final_g is not None
    in_specs = _row_specs(tm, D, len(xs) == 2) + [
        pl.BlockSpec((None, None, 1, 3 * D), lambda t: (layer, _mod_row(t, tm), 0, slot)),
        _const_spec((1, D)),
        _const_spec((D, D_FF)),
        _const_spec((D, D_FF)),
        _const_spec((D_FF, D)),
    ]
    args = list(xs) + [mods, norm_g.reshape(1, D), wg, wu, wd]
    if final:
        in_specs.append(_const_spec((1, D)))
        args.append(final_g.reshape(1, D))
    return pl.pallas_call(
        functools.partial(_ffn_kernel, final=final, nx=len(xs), tm=tm),
        grid=(rows // tm,),
        in_specs=in_specs,
        out_specs=pl.BlockSpec((tm, D), lambda t: (t, 0)),
        out_shape=jax.ShapeDtypeStruct((rows, D), F32),
        compiler_params=_cparams(("arbitrary",)),
        name="ffn",
    )(*args)


def _proj_kernel(x_ref, m_ref, g_ref, wkvq_ref, wahi_ref, walo_ref, wr_ref, wh_ref, wg_ref,
                 kvq_ref, a_ref, r_ref, hy_ref, gt_ref):
    x = x_ref[...]
    m = m_ref[...]
    shift, scale = m[:, :D], m[:, D:2 * D]
    h = _rms(x, g_ref[...]) * (1.0 + scale) + shift
    hb = h.astype(BF16)
    kvq_ref[...] = _dot(hb, wkvq_ref[...]).astype(kvq_ref.dtype)
    hl = (h - hb.astype(F32)).astype(BF16)
    a_ref[...] = _dot(hb, wahi_ref[...]) + _dot(hl, wahi_ref[...]) + _dot(hb, walo_ref[...])
    r_ref[...] = _dot(hb, wr_ref[...]).astype(r_ref.dtype)
    hy_ref[...] = _dot(hb, wh_ref[...]).astype(hy_ref.dtype)
    gt_ref[...] = _dot(hb, wg_ref[...]).astype(gt_ref.dtype)


def _proj(x, mods, layer, norm_g, wkvq, wa_hi, wa_lo, wr, wh, wg):
    tm = 256
    nk = 2 * KEY_W + VAL_W
    na = 2 * LOWRANK
    row = lambda w: pl.BlockSpec((tm, w), lambda t: (t, 0))
    return pl.pallas_call(
        _proj_kernel,
        grid=(R_ALL // tm,),
        in_specs=[
            row(D),
            pl.BlockSpec((None, None, 1, 3 * D), lambda t: (layer, _mod_row(t, tm), 0, 1)),
            _const_spec((1, D)),
            _const_spec((D, nk)),
            _const_spec((D, na)),
            _const_spec((D, na)),
            _const_spec((D, VAL_W)),
            _const_spec((D, 3 * HY_W)),
            _const_spec((D, 2 * D)),
        ],
        out_specs=[row(nk), row(na), row(VAL_W), row(3 * HY_W), row(2 * D)],
        out_shape=[
            jax.ShapeDtypeStruct((R_ALL, nk), BF16),
            jax.ShapeDtypeStruct((R_ALL, na), F32),
            jax.ShapeDtypeStruct((R_ALL, VAL_W), BF16),
            jax.ShapeDtypeStruct((R_ALL, 3 * HY_W), BF16),
            jax.ShapeDtypeStruct((R_ALL, 2 * D), BF16),
        ],
        compiler_params=_cparams(("arbitrary",)),
        name="proj",
    )(x, mods, norm_g.reshape(1, D), wkvq, wa_hi, wa_lo, wr, wh, wg)


GLA_BLK = 256
GLA_STEPS = 1 + SEQ // GLA_BLK


def _gla_row_block(b, d, s):
    lat = jnp.where(d == 0, s - 1, SEQ // GLA_BLK - s)
    return jnp.where(s == 0, R_LAT // GLA_BLK + b, b * (SEQ // GLA_BLK) + lat)


def _dot_nt(a, b):
    return lax.dot_general(a, b, (((1,), (1,)), ((), ())), preferred_element_type=F32)


def _dot_tn(a, b):
    return lax.dot_general(a, b, (((0,), (0,)), ((), ())), preferred_element_type=F32)


def _gla_body(kvq_ref, a_ref, wa_ref, ba_ref, tri_ref, o_ref, st_ref, qd_s, ke_s, bc_s):
    d = pl.program_id(1)
    s = pl.program_id(2)
    nchunk = GLA_BLK // CHUNK

    @pl.when(s == 0)
    def _():
        st_ref[...] = jnp.zeros_like(st_ref)

    a = a_ref[...]
    a_hi, a_lo = _split(a)
    pre = _dot(jnp.concatenate([a_hi, a_lo, a_hi], axis=1), wa_ref[...]) + ba_ref[...]
    la = (jnp.minimum(pre, 0.0) - jnp.log(1.0 + jnp.exp(-jnp.abs(pre)))) * (1.0 / TAU)
    la_hi, la_lo = _split(la)
    tri = tri_ref[...]
    bcum = _dot(tri, la_hi) + _dot(tri, la_lo)
    bc_s[...] = bcum
    tot_row = jnp.where(d == 0, CHUNK - 1, 0)
    btot = jnp.concatenate(
        [jnp.broadcast_to(bc_s[pl.ds(j * CHUNK + tot_row, 1), :], (CHUNK, KEY_W)) for j in range(nchunk)], axis=0)

    k = kvq_ref[:, :KEY_W].astype(F32)
    q = kvq_ref[:, KEY_W + VAL_W:].astype(F32)
    qd_s[...] = (q * (DK ** -0.5) * jnp.exp(bcum)).astype(BF16)
    kn = (k * jnp.exp(-bcum)).astype(BF16)
    ke_s[...] = (k * jnp.exp(btot - bcum)).astype(BF16)

    ri = lax.broadcasted_iota(jnp.int32, (GLA_BLK, GLA_BLK), 0)
    ci = lax.broadcasted_iota(jnp.int32, (GLA_BLK, GLA_BLK), 1)
    seen = ((ri // CHUNK) == (ci // CHUNK)) & ((ri - ci) * (1 - 2 * d) >= 0)

    for h in range(HEADS):
        ksl = slice(h * DK, (h + 1) * DK)
        att = jnp.where(seen, _dot_nt(qd_s[:, ksl], kn[:, ksl]), 0.0).astype(BF16)
        o_ref[:, h * DV:(h + 1) * DV] = _dot(att, kvq_ref[:, KEY_W + h * DV:KEY_W + (h + 1) * DV])

    for j in range(nchunk):
        r0 = pl.multiple_of(jnp.where(d == 0, j, nchunk - 1 - j) * CHUNK, CHUNK)
        rows = pl.ds(r0, CHUNK)
        dec = jnp.exp(bc_s[pl.ds(r0 + tot_row, 1), :])
        for h in range(HEADS):
            ksl = slice(h * DK, (h + 1) * DK)
            vsl = slice(h * DV, (h + 1) * DV)
            st = st_ref[h]
            o_ref[rows, vsl] += _dot_nt(qd_s[rows, ksl], st.astype(BF16))
            upd = _dot_tn(kvq_ref[rows, KEY_W + h * DV:KEY_W + (h + 1) * DV], ke_s[rows, ksl])
            st_ref[h] = st * dec[:, ksl] + upd


def _gla(kvq, a, wa_cat, ba_cat, tri):
    nk = 2 * KEY_W + VAL_W
    row_map = lambda b, d, s: (_gla_row_block(b, d, s), 0)
    return pl.pallas_call(
        _gla_body,
        grid=(NB, 2, GLA_STEPS),
        in_specs=[
            pl.BlockSpec((GLA_BLK, nk), row_map),
            pl.BlockSpec((GLA_BLK, 2 * LOWRANK), row_map),
            pl.BlockSpec((None, 6 * LOWRANK, KEY_W), lambda b, d, s: (d, 0, 0)),
            pl.BlockSpec((None, 1, KEY_W), lambda b, d, s: (d, 0, 0)),
            pl.BlockSpec((None, GLA_BLK, GLA_BLK), lambda b, d, s: (d, 0, 0)),
        ],
        out_specs=pl.BlockSpec((None, GLA_BLK, VAL_W), lambda b, d, s: (d, _gla_row_block(b, d, s), 0)),
        out_shape=jax.ShapeDtypeStruct((2, R_ALL, VAL_W), F32),
        scratch_shapes=[
            pltpu.VMEM((HEADS, DV, DK), F32),
            pltpu.VMEM((GLA_BLK, KEY_W), BF16),
            pltpu.VMEM((GLA_BLK, KEY_W), BF16),
            pltpu.VMEM((GLA_BLK, KEY_W), F32),
        ],
        compiler_params=_cparams(("arbitrary", "arbitrary", "arbitrary")),
        name="gla",
    )(kvq, a, wa_cat, ba_cat, tri)


def _mix_kernel(x_ref, m_ref, of_ref, ob_ref, r_ref, gt_ref, ng_ref, wgo_ref, who_ref, wo_ref, *rest, tm):
    zs, o_ref = rest[:-1], rest[-1]
    x = x_ref[...]
    gate = m_ref[...][:, 2 * D:]
    o = of_ref[...] + ob_ref[...]
    ng = ng_ref[...]
    parts = []
    for h in range(HEADS):
        sl = slice(h * DV, (h + 1) * DV)
        oh = o[:, sl]
        ms = jnp.mean(oh * oh, axis=-1, keepdims=True)
        parts.append(oh * lax.rsqrt(ms + EPS) * ng[:, sl])
    on = jnp.concatenate(parts, axis=-1)
    r = r_ref[...].astype(F32)
    y_gla = _dot((on * (r * _sigmoid(r))).astype(BF16), wgo_ref[...])
    y_hy = _dot(_row_tile(zs, tm), who_ref[...])
    gt = gt_ref[...].astype(F32)
    mix = _sigmoid(gt[:, :D]) * y_gla + _sigmoid(gt[:, D:]) * y_hy
    o_ref[...] = x + gate * _dot(mix.astype(BF16), wo_ref[...])


def _mix(x, mods, layer, o2, r, zs, gt, norm_g, wgo, who, wo, rows):
    tm = 512
    row = lambda w: pl.BlockSpec((tm, w), lambda t: (t, 0))
    return pl.pallas_call(
        functools.partial(_mix_kernel, tm=tm),
        grid=(rows // tm,),
        in_specs=[
            row(D),
            pl.BlockSpec((None, None, 1, 3 * D), lambda t: (layer, _mod_row(t, tm), 0, 1)),
            pl.BlockSpec((None, tm, VAL_W), lambda t: (0, t, 0)),
            pl.BlockSpec((None, tm, VAL_W), lambda t: (1, t, 0)),
            row(VAL_W),
            row(2 * D),
            _const_spec((1, VAL_W)),
            _const_spec((VAL_W, D)),
            _const_spec((HY_W, D)),
            _const_spec((D, D)),
        ] + _row_specs(tm, HY_W, len(zs) == 2),
        out_specs=row(D),
        out_shape=jax.ShapeDtypeStruct((rows, D), F32),
        compiler_params=_cparams(("arbitrary",)),
        name="mix",
    )(x, mods, o2, o2, r, gt, norm_g.reshape(1, VAL_W), wgo, who, wo, *zs)


def _stack3_lhs(a):
    hi, lo = _split(a)
    return jnp.concatenate([hi, lo, hi], axis=1)


def _stack3_rhs(w):
    hi, lo = _split(w)
    return jnp.concatenate([hi, hi, lo], axis=0)


def _filt_mlp_kernel(z_ref, w1_ref, b1_ref, f1_ref, w2_ref, b2_ref, f2_ref, o_ref):
    h = jnp.sin(f1_ref[...] * (_dot(_stack3_lhs(z_ref[...]), w1_ref[...]) + b1_ref[...]))
    h = jnp.sin(f2_ref[...] * (_dot(_stack3_lhs(h), w2_ref[...]) + b2_ref[...]))
    o_ref[...] = _stack3_lhs(h)


def _filt_mlp(feats, w1, b1, f1, w2, b2, f2):
    rows = feats.shape[0]
    tm = 512
    vec = lambda: _const_spec((1, HY_HID))
    return pl.pallas_call(
        _filt_mlp_kernel,
        grid=(rows // tm,),
        in_specs=[pl.BlockSpec((tm, HY_HID), lambda t: (t, 0)), _const_spec((3 * HY_HID, HY_HID)), vec(), vec(),
                  _const_spec((3 * HY_HID, HY_HID)), vec(), vec()],
        out_specs=pl.BlockSpec((tm, 3 * HY_HID), lambda t: (t, 0)),
        out_shape=jax.ShapeDtypeStruct((rows, 3 * HY_HID), BF16),
        compiler_params=_cparams(("arbitrary",)),
        name="filt_mlp",
    )(feats, _stack3_rhs(w1), b1.reshape(1, HY_HID), f1.reshape(1, HY_HID), _stack3_rhs(w2),
      b2.reshape(1, HY_HID), f2.reshape(1, HY_HID))


def _taps_chunk(h2s, rc, w3f, w3b, delta):
    t, fsel, bsel = rc[:, 0:1], rc[:, 1:2], rc[:, 2:3]
    win = jnp.exp(-t * delta)
    h3 = _dot(h2s, jnp.concatenate([w3f, w3b], axis=1))
    return win * (fsel * h3[:, :LANES] + bsel * h3[:, LANES:])


def _stage_b_fwd(buf_a, j, e_ref):
    cols = []
    for k1 in (2 * j, 2 * j + 1):
        re = buf_a[pl.ds(k1, N2, stride=PITCH_A), :]
        im = buf_a[pl.ds(N1 + k1, N2, stride=PITCH_A), :]
        cols.append(jnp.concatenate([re, im], axis=0).astype(BF16))
    return _dot(e_ref[...], jnp.concatenate(cols, axis=1))


def _spec_lat_kernel(h2_ref, rc_ref, w3f_ref, w3b_ref, dl_ref, wt_ref, e_ref, o_ref, taps, buf_a):
    tb = 512
    delta = dl_ref[...]

    def fill(i, acc):
        rows = pl.ds(pl.multiple_of(i * tb, tb), tb)
        tp = _taps_chunk(h2_ref[rows, :], rc_ref[rows, :], w3f_ref[...], w3b_ref[...], delta)
        taps[rows, :] = tp
        return acc + jnp.sum(jnp.abs(tp), axis=0, keepdims=True)

    l1 = lax.fori_loop(0, FFT_N // tb, fill, jnp.zeros((1, LANES), F32))
    inv = 1.0 / (l1 + EPS)

    def stage_a(n2, c):
        rhs = taps[pl.ds(pl.multiple_of(n2 * N1, N1), N1), :]
        buf_a[pl.ds(pl.multiple_of(n2 * PITCH_A, 8), 2 * N1), :] = _dot(wt_ref[n2], rhs.astype(BF16))
        return c

    lax.fori_loop(0, N2, stage_a, 0, unroll=UNROLL)

    def stage_b(j, c):
        y = _stage_b_fwd(buf_a, j, e_ref)
        rows = 2 * N2
        o_ref[pl.ds(pl.multiple_of(2 * j * rows, 2 * rows), rows), :] = (y[:, :LANES] * inv).astype(o_ref.dtype)
        o_ref[pl.ds(pl.multiple_of(2 * j * rows + rows, rows), rows), :] = (y[:, LANES:] * inv).astype(o_ref.dtype)
        return c

    lax.fori_loop(0, N1 // 2, stage_b, 0, unroll=UNROLL)


def _spec_lat(h2p, rcp, w3, delta, cst):
    nct = HY_W // LANES
    return pl.pallas_call(
        _spec_lat_kernel,
        grid=(2, nct),
        in_specs=[
            _const_spec((FFT_N, 3 * HY_HID)),
            _const_spec((FFT_N, 8)),
            pl.BlockSpec((3 * HY_HID, LANES), lambda o, c: (0, o * nct + c)),
            pl.BlockSpec((3 * HY_HID, LANES), lambda o, c: (0, 2 * nct + o * nct + c)),
            pl.BlockSpec((1, LANES), lambda o, c: (0, c)),
            _const_spec((N2, 2 * N1, N1)),
            _const_spec((2 * N2, 2 * N2)),
        ],
        out_specs=pl.BlockSpec((None, None, 2 * FFT_N, LANES), lambda o, c: (o, c, 0, 0)),
        out_shape=jax.ShapeDtypeStruct((2, nct, 2 * FFT_N, LANES), BF16),
        scratch_shapes=[pltpu.VMEM((FFT_N, LANES), F32), pltpu.VMEM((N2 * PITCH_A, LANES), F32)],
        compiler_params=_cparams(("arbitrary", "arbitrary")),
        name="spec_lat",
    )(h2p, rcp, w3, w3, delta, cst["wt"], cst["e"])


def _hy_lat_kernel(ua_ref, ub_ref, h_ref, cw_ref, cb_ref, bias_ref, wa_ref, e_ref, g_ref, mi_ref,
                   o_ref, zin, buf_a, buf_d, upad):
    s = pl.program_id(2)

    def zslab(n2):
        return zin[pl.ds(pl.multiple_of(n2 * PITCH_Z, 8), 2 * N2), :]

    def repitch(n1, c):
        src = pl.ds(pl.multiple_of(n1 * GRID_W, GRID_W), GRID_W)
        dst = pl.ds(pl.multiple_of(n1 * PITCH_U, 8), GRID_W)
        upad[0, dst, :] = ua_ref[src, :].astype(F32)
        upad[1, dst, :] = ub_ref[src, :].astype(F32)
        return c
    lax.fori_loop(0, N2, repitch, 0, unroll=UNROLL)

    def conv_slices(n2):
        w0, w1, w2 = cw_ref[0:1, :], cw_ref[1:2, :], cw_ref[2:3, :]
        cb = cb_ref[...]
        nm = jnp.maximum(n2 - 1, 0)
        nx = jnp.minimum(n2 + 1, GRID_W - 1)
        outs = []
        for j in range(2):
            cur = upad[j, pl.ds(n2, N2, stride=PITCH_U), :]
            prv = jnp.where(n2 > 0, upad[j, pl.ds(nm, N2, stride=PITCH_U), :], 0.0)
            nxt = jnp.where(n2 < GRID_W - 1, upad[j, pl.ds(nx, N2, stride=PITCH_U), :], 0.0)
            outs.append(w0 * prv + w1 * cur + w2 * nxt + cb)
        return jnp.concatenate(outs, axis=0)

    @pl.when(s == 0)
    def _():
        def body(n2, c):
            zin[pl.ds(pl.multiple_of(n2 * PITCH_Z, 8), 2 * N2), :] = conv_slices(n2)
            return c
        lax.fori_loop(0, N2, body, 0, unroll=UNROLL)

    @pl.when(s > 0)
    def _():
        def stage_a(n2, c):
            buf_a[pl.ds(pl.multiple_of(n2 * PITCH_A, 8), 2 * N1), :] = _dot(wa_ref[n2], zslab(n2).astype(BF16))
            return c
        lax.fori_loop(0, N2, stage_a, 0, unroll=UNROLL)

        def stage_b(j, c):
            y = _stage_b_fwd(buf_a, j, e_ref)
            prods = []
            for i in range(2):
                yk = y[:, i * LANES:(i + 1) * LANES]
                hb = h_ref[pl.ds(pl.multiple_of((2 * j + i) * 2 * N2, 2 * N2), 2 * N2), :].astype(F32)
                yr, yi = yk[:N2], yk[N2:]
                hr, hi = hb[:N2], hb[N2:]
                prods.append(jnp.concatenate([yr * hr - yi * hi, yr * hi + yi * hr], axis=0).astype(BF16))
            d = _dot(g_ref[...], jnp.concatenate(prods, axis=1))
            for i in range(2):
                rows = pl.ds(pl.multiple_of((2 * j + i) * PITCH_D, 8), 2 * N2)
                buf_d[rows, :] = d[:, i * LANES:(i + 1) * LANES]
            return c
        lax.fori_loop(0, N1 // 2, stage_b, 0, unroll=UNROLL)

        def stage_ai(n2, c):
            re = buf_d[pl.ds(n2, N1, stride=PITCH_D), :]
            im = buf_d[pl.ds(N2 + n2, N1, stride=PITCH_D), :]
            x = jnp.concatenate([re, im], axis=0).astype(BF16)
            y = _dot(mi_ref[n2], x)
            rows = pl.ds(pl.multiple_of(n2 * PITCH_Z, 8), 2 * N2)
            zin[rows, :] = conv_slices(n2) * (y + bias_ref[...] * zin[rows, :])
            return c
        lax.fori_loop(0, N2, stage_ai, 0, unroll=UNROLL)

    @pl.when(s == 2)
    def _():
        def emit(n1, c):
            dst = pl.ds(pl.multiple_of(n1 * GRID_W, GRID_W), GRID_W)
            o_ref[0, dst, :] = zin[pl.ds(n1, GRID_W, stride=PITCH_Z), :].astype(o_ref.dtype)
            o_ref[1, dst, :] = zin[pl.ds(N2 + n1, GRID_W, stride=PITCH_Z), :].astype(o_ref.dtype)
            return c
        lax.fori_loop(0, N2, emit, 0, unroll=UNROLL)


def _hy_lat(hy, hspec, conv_w, conv_b, bias, cst):
    nct = HY_W // LANES
    part = lambda c, p, s: s * nct + c
    order = lambda s: jnp.maximum(s - 1, 0)
    consts = [cst[n] for n in ("wa", "e", "g", "mi")]
    return pl.pallas_call(
        _hy_lat_kernel,
        grid=(nct, NB // 2, 3),
        in_specs=[
            pl.BlockSpec((SEQ, LANES), lambda c, p, s: (2 * p, part(c, p, s))),
            pl.BlockSpec((SEQ, LANES), lambda c, p, s: (2 * p + 1, part(c, p, s))),
            pl.BlockSpec((None, None, 2 * FFT_N, LANES), lambda c, p, s: (order(s), c, 0, 0)),
            pl.BlockSpec((3, LANES), lambda c, p, s: (0, part(c, p, s))),
            pl.BlockSpec((1, LANES), lambda c, p, s: (0, part(c, p, s))),
            pl.BlockSpec((None, 1, LANES), lambda c, p, s: (order(s), 0, c)),
        ] + [_const_spec(x.shape) for x in consts],
        out_specs=pl.BlockSpec((2, SEQ, LANES), lambda c, p, s: (p, 0, c)),
        out_shape=jax.ShapeDtypeStruct((NB, SEQ, HY_W), BF16),
        scratch_shapes=[
            pltpu.VMEM((N2 * PITCH_Z, LANES), F32),
            pltpu.VMEM((N2 * PITCH_A, LANES), F32),
            pltpu.VMEM((N1 * PITCH_D, LANES), F32),
            pltpu.VMEM((2, N2 * PITCH_U, LANES), F32),
        ],
        compiler_params=_cparams(("arbitrary", "arbitrary", "arbitrary")),
        name="hy_lat",
    )(hy, hy, hspec, conv_w, conv_b, bias, *consts)


def _spec_ctx_kernel(h2_ref, rc_ref, w3f_ref, w3b_ref, dl_ref, wt_ref, o_ref):
    tp = _taps_chunk(h2_ref[...], rc_ref[...], w3f_ref[...], w3b_ref[...], dl_ref[...])
    l1 = jnp.sum(jnp.abs(tp), axis=0, keepdims=True)
    o_ref[...] = _dot(wt_ref[...], (tp * (1.0 / (l1 + EPS))).astype(BF16))


def _spec_ctx(h2, rc, w3, delta, cst):
    nct = HY_W // LANES
    return pl.pallas_call(
        _spec_ctx_kernel,
        grid=(2, nct),
        in_specs=[
            _const_spec((CTX_N, 3 * HY_HID)),
            _const_spec((CTX_N, 8)),
            pl.BlockSpec((3 * HY_HID, LANES), lambda o, c: (0, o * nct + c)),
            pl.BlockSpec((3 * HY_HID, LANES), lambda o, c: (0, 2 * nct + o * nct + c)),
            pl.BlockSpec((1, LANES), lambda o, c: (0, c)),
            _const_spec((2 * CTX_N, CTX_N)),
        ],
        out_specs=pl.BlockSpec((None, None, 2 * CTX_N, LANES), lambda o, c: (o, c, 0, 0)),
        out_shape=jax.ShapeDtypeStruct((2, nct, 2 * CTX_N, LANES), F32),
        compiler_params=_cparams(("arbitrary", "arbitrary")),
        name="spec_ctx",
    )(h2, rc, w3, w3, delta, cst["cwt"])


def _hy_ctx_kernel(va, vb, x1a, x1b, x2a, x2b, h_ref, cwv, cw1, cw2, cbv, cb1, cb2, bias_ref,
                   wf_ref, wi_ref, o_ref):
    row = lax.broadcasted_iota(jnp.int32, (CTX, LANES), 0)

    def sconv(u_ref, cw, cb):
        u = u_ref[...].astype(F32)
        prv = jnp.where(row == 0, 0.0, pltpu.roll(u, 1, 0))
        nxt = jnp.where(row == CTX - 1, 0.0, pltpu.roll(u, CTX - 1, 0))
        return cw[0:1, :] * prv + cw[1:2, :] * u + cw[2:3, :] * nxt + cb[...]

    def pair(ra, rb, cw, cb):
        return jnp.concatenate([sconv(ra, cw, cb), sconv(rb, cw, cb)], axis=0)

    z = pair(va, vb, cwv, cbv)
    gates = (pair(x1a, x1b, cw1, cb1), pair(x2a, x2b, cw2, cb2))
    for o in range(2):
        y = _dot(wf_ref[...], z.astype(BF16))
        hb = h_ref[o]
        yr, yi = y[:CTX_N], y[CTX_N:]
        hr, hi = hb[:CTX_N], hb[CTX_N:]
        prod = jnp.concatenate([yr * hr - yi * hi, yr * hi + yi * hr], axis=0)
        conv = _dot(wi_ref[...], prod.astype(BF16))
        z = gates[o] * (conv + bias_ref[o] * z)
    o_ref[0] = z[:CTX].astype(o_ref.dtype)
    o_ref[1] = z[CTX:].astype(o_ref.dtype)


def _hy_ctx(hy, hspec, conv_w, conv_b, bias, cst):
    nct = HY_W // LANES
    base = R_LAT // CTX

    def blk(j, part):
        return pl.BlockSpec((CTX, LANES), lambda c, p: (base + 2 * p + j, part * nct + c))

    cw = lambda part: pl.BlockSpec((3, LANES), lambda c, p: (0, part * nct + c))
    cb = lambda part: pl.BlockSpec((1, LANES), lambda c, p: (0, part * nct + c))
    consts = [cst["cwf"], cst["cwi"]]
    return pl.pallas_call(
        _hy_ctx_kernel,
        grid=(nct, NB // 2),
        in_specs=[blk(0, 0), blk(1, 0), blk(0, 1), blk(1, 1), blk(0, 2), blk(1, 2),
                  pl.BlockSpec((2, None, 2 * CTX_N, LANES), lambda c, p: (0, c, 0, 0)),
                  cw(0), cw(1), cw(2), cb(0), cb(1), cb(2),
                  pl.BlockSpec((2, 1, LANES), lambda c, p: (0, 0, c))]
        + [_const_spec(x.shape) for x in consts],
        out_specs=pl.BlockSpec((2, CTX, LANES), lambda c, p: (p, 0, c)),
        out_shape=jax.ShapeDtypeStruct((NB, CTX, HY_W), BF16),
        compiler_params=_cparams(("arbitrary", "arbitrary")),
        name="hy_ctx",
    )(hy, hy, hy, hy, hy, hy, hspec, conv_w, conv_w, conv_w, conv_b, conv_b, conv_b, bias, *consts)


def _stack(c):
    return np.block([[c.real, -c.imag], [c.imag, c.real]])


def _fft_constants():
    cst = {}
    k1 = np.arange(N1)[:, None]
    ca = np.exp(-2j * np.pi * k1 * np.arange(N1 // 2)[None, :] / N1)
    tw = np.exp(-2j * np.pi * k1 * np.arange(N2)[None, :] / FFT_N)
    ct = np.exp(-2j * np.pi * k1 * np.arange(N1)[None, :] / N1)
    cst["wa"] = jnp.asarray(np.stack([_stack(tw[:, n2:n2 + 1] * ca) for n2 in range(N2)]), F32).astype(BF16)
    wt = [tw[:, n2:n2 + 1] * ct for n2 in range(N2)]
    cst["wt"] = jnp.asarray(np.stack([np.concatenate([m.real, m.imag], axis=0) for m in wt]), F32).astype(BF16)
    e = np.exp(-2j * np.pi * np.arange(N2)[:, None] * np.arange(N2)[None, :] / N2)
    cst["e"] = jnp.asarray(_stack(e), F32).astype(BF16)
    cst["g"] = jnp.asarray(_stack(np.conj(e).T), F32).astype(BF16)
    bi = np.exp(2j * np.pi * np.arange(N1 // 2)[:, None] * np.arange(N1)[None, :] / N1) / FFT_N
    cst["mi"] = jnp.asarray(np.stack([_stack(bi * np.conj(tw[:, n2])[None, :]) for n2 in range(N2)]),
                            F32).astype(BF16)
    kk = np.arange(CTX_N)[:, None]
    cf = np.exp(-2j * np.pi * kk * np.arange(CTX)[None, :] / CTX_N)
    cst["cwf"] = jnp.asarray(_stack(cf), F32).astype(BF16)
    cfull = np.exp(-2j * np.pi * kk * np.arange(CTX_N)[None, :] / CTX_N)
    cst["cwt"] = jnp.asarray(np.concatenate([cfull.real, cfull.imag], axis=0), F32).astype(BF16)
    ci = np.exp(2j * np.pi * np.arange(CTX)[:, None] * np.arange(CTX_N)[None, :] / CTX_N) / CTX_N
    cst["cwi"] = jnp.asarray(_stack(ci), F32).astype(BF16)
    return cst


def _filter_positions(L, permute):
    bands = (HY_EMB - 1) // 2
    t = np.linspace(0.0, 1.0, L)[:, None]
    w = (2.0 * math.pi / L) * np.arange(L)[:, None]
    f = np.linspace(1e-4, bands - 1, bands)[None, :]
    z = np.concatenate([t, np.cos(f * w), -np.sin(f * w)], axis=-1)
    n = np.arange(2 * L)
    pos = np.where(n < L, n, 2 * L - n)
    pos = np.where(n == L, 0, pos)
    fsel = (n < L).astype(np.float32)
    bsel = (n > L).astype(np.float32)
    if permute:
        order = (np.arange(N1)[None, :] * N2 + np.arange(N2)[:, None]).reshape(-1)
        pos, fsel, bsel = pos[order], fsel[order], bsel[order]
    feats = np.pad(z[pos], ((0, 0), (0, HY_HID - HY_EMB)))
    rc = np.concatenate([t[pos], fsel[:, None], bsel[:, None], np.zeros((2 * L, 5))], axis=1)
    return jnp.asarray(feats, F32), jnp.asarray(rc, F32)


def _decay_rates():
    max_decay = math.log(HY_TARGET) / HY_FAST
    min_decay = math.log(HY_TARGET) / HY_SLOW
    return jnp.asarray(np.abs(np.linspace(min_decay, max_decay, HY_W))[None, :], F32)


def _gla_constants():
    r = np.arange(GLA_BLK)
    same = (r[:, None] // CHUNK) == (r[None, :] // CHUNK)
    fwd = same & (r[:, None] >= r[None, :])
    bwd = same & (r[:, None] <= r[None, :])
    return jnp.asarray(np.stack([fwd, bwd]).astype(np.float32)).astype(BF16)


def kernel(x, c, ctx, c_ctx, ada_w, ada_b, ffn1_norm_g, ffn1_w_gate, ffn1_w_up, ffn1_w_down, mix_norm_g, w_in,
           gla_wa_f, gla_ba_f, gla_wa_b, gla_ba_b, gla_norm_g, w_gla_out, hy_conv_w, hy_conv_b, hy_f_w1, hy_f_b1,
           hy_f_freq1, hy_f_w2, hy_f_b2, hy_f_freq2, hy_f_w3, hy_bias, w_hy_out, w_out, ffn2_norm_g, ffn2_w_gate,
           ffn2_w_up, ffn2_w_down, final_norm_g):
    cst = _fft_constants()
    tri = _gla_constants()
    delta = _decay_rates()
    feats_lat, rc_lat = _filter_positions(SEQ, True)
    feats_ctx, rc_ctx = _filter_positions(CTX, False)

    cvec = jnp.concatenate([c, c_ctx[None, :], jnp.zeros((3, D), F32)], axis=0)
    mods = _mods(cvec, ada_w, ada_b).reshape(DEPTH, 8, 1, N_MOD * D)
    xs = (x.reshape(R_LAT, D), ctx.reshape(R_CTX, D))

    bf = lambda w: w.astype(BF16)
    zpad = jnp.zeros((LOWRANK, KEY_W), F32)
    for i in range(DEPTH):
        last = i == DEPTH - 1
        xa = _ffn(xs, mods, i, 0, ffn1_norm_g[i], bf(ffn1_w_gate[i]), bf(ffn1_w_up[i]), bf(ffn1_w_down[i]), R_ALL)

        wi = w_in[i]
        wkvq = bf(jnp.concatenate([wi[:, OFF_K:OFF_A], wi[:, OFF_Q:OFF_R]], axis=1))
        wa_hi, wa_lo = _split(wi[:, OFF_A:OFF_Q])
        kvq, a, r, hy, gt = _proj(xa, mods, i, mix_norm_g[i], wkvq, wa_hi, wa_lo, bf(wi[:, OFF_R:OFF_H]),
                                  bf(wi[:, OFF_H:OFF_G]), bf(wi[:, OFF_G:IN_W]))

        wa_cat = jnp.stack([jnp.concatenate([gla_wa_f[i], zpad], axis=0),
                            jnp.concatenate([zpad, gla_wa_b[i]], axis=0)])
        w_hi, w_lo = _split(wa_cat)
        ba_cat = jnp.stack([gla_ba_f[i], gla_ba_b[i]])[:, None, :]
        o2 = _gla(kvq, a, jnp.concatenate([w_hi, w_hi, w_lo], axis=1), ba_cat, tri)

        w1p = jnp.pad(hy_f_w1[i], ((0, HY_HID - HY_EMB), (0, 0)))
        mlp = (w1p, hy_f_b1[i], hy_f_freq1[i], hy_f_w2[i], hy_f_b2[i], hy_f_freq2[i])
        bias = hy_bias[i][:, None, :]
        w3s = _stack3_rhs(hy_f_w3[i])
        spec_lat = _spec_lat(_filt_mlp(feats_lat, *mlp), rc_lat, w3s, delta, cst)
        zs = (_hy_lat(hy, spec_lat, hy_conv_w[i], hy_conv_b[i][None, :], bias, cst).reshape(R_LAT, HY_W),)

        rows = R_LAT if last else R_ALL
        if not last:
            spec_ctx = _spec_ctx(_filt_mlp(feats_ctx, *mlp), rc_ctx, w3s, delta, cst)
            zs += (_hy_ctx(hy, spec_ctx, hy_conv_w[i], hy_conv_b[i][None, :], bias, cst).reshape(R_CTX, HY_W),)
        xa = _mix(xa, mods, i, o2, r, zs, gt, gla_norm_g[i], bf(w_gla_out[i]), bf(w_hy_out[i]), bf(w_out[i]), rows)
        xs = (_ffn((xa,), mods, i, 2, ffn2_norm_g[i], bf(ffn2_w_gate[i]), bf(ffn2_w_up[i]), bf(ffn2_w_down[i]), rows,
                   final_g=final_norm_g if last else None),)
    return xs[0].reshape(NB, SEQ, D)
```

```python
import functools
import math

import numpy as np
import jax
import jax.numpy as jnp
from jax import lax
from jax.experimental import pallas as pl
from jax.experimental.pallas import tpu as pltpu

F32 = jnp.float32
BF16 = jnp.bfloat16

D = 1024
NB = 4
SEQ = 4096
DEPTH = 2
CTX = 256
GRID_W = 64
D_FF = 2816
HALF_STEP = 0.5
N_MOD = 9
EPS = 1e-6

HEADS = 4
DK = 128
DV = 256
LOWRANK = 16
TAU = 16.0
CHUNK = 64
KEY_W = HEADS * DK
VAL_W = HEADS * DV

HY_W = 1024
HY_EMB = 33
HY_HID = 64
HY_FAST = 0.3
HY_SLOW = 1.5
HY_TARGET = 1e-2

OFF_K = 0
OFF_V = OFF_K + KEY_W
OFF_A = OFF_V + VAL_W
OFF_Q = OFF_A + 2 * LOWRANK
OFF_R = OFF_Q + KEY_W
OFF_H = OFF_R + VAL_W
OFF_G = OFF_H + 3 * HY_W
IN_W = OFF_G + 2 * D

R_LAT = NB * SEQ
R_CTX = NB * CTX
R_ALL = R_LAT + R_CTX

LANES = 128
VMEM_LIMIT = 56 * 1024 * 1024

FFT_N = 2 * SEQ
N1 = 128
N2 = 64
PITCH_A = 264
PITCH_D = 136
PITCH_U = 72
PITCH_Z = 136
UNROLL = 8
CTX_N = 2 * CTX


def _cparams(sem):
    return pltpu.CompilerParams(dimension_semantics=sem, vmem_limit_bytes=VMEM_LIMIT)


def _const_spec(shape):
    nd = len(shape)
    return pl.BlockSpec(shape, lambda *_: (0,) * nd, pipeline_mode=pl.Buffered(1))


def _layer_spec(shape, layer):
    nd = len(shape)
    return pl.BlockSpec((None,) + tuple(shape), lambda *_: (layer,) + (0,) * nd, pipeline_mode=pl.Buffered(1))


def _split(x):
    hi = x.astype(BF16)
    lo = (x - hi.astype(F32)).astype(BF16)
    return hi, lo


def _dot(a, b):
    return jnp.dot(a, b, preferred_element_type=F32)


def _sigmoid(x):
    return 1.0 / (1.0 + jnp.exp(-x))


def _rms(x, g):
    ms = jnp.mean(x * x, axis=-1, keepdims=True)
    return x * lax.rsqrt(ms + EPS) * g


def _mods_kernel(c_ref, w_ref, b_ref, o_ref):
    c = c_ref[...]
    s = (c * _sigmoid(c)).astype(BF16)
    o_ref[...] = _dot(s, w_ref[...].astype(BF16)) + b_ref[...]


def _mods(cvec, ada_w, ada_b):
    tn = 1024
    return pl.pallas_call(
        _mods_kernel,
        grid=(DEPTH, N_MOD * D // tn),
        in_specs=[
            pl.BlockSpec((8, D), lambda l, j: (0, 0)),
            pl.BlockSpec((None, D, tn), lambda l, j: (l, 0, j)),
            pl.BlockSpec((None, 1, tn), lambda l, j: (l, 0, j)),
        ],
        out_specs=pl.BlockSpec((None, 8, tn), lambda l, j: (l, 0, j)),
        out_shape=jax.ShapeDtypeStruct((DEPTH, 8, N_MOD * D), F32),
        compiler_params=_cparams(("arbitrary", "arbitrary")),
        name="mods",
    )(cvec, ada_w, ada_b.reshape(DEPTH, 1, N_MOD * D))


def _mod_row(t, tm):
    lat_tiles = R_LAT // tm
    per_batch = SEQ // tm
    return jnp.where(t < lat_tiles, t // per_batch, NB)


def _row_specs(tm, width, split):
    if not split:
        return [pl.BlockSpec((tm, width), lambda t: (t, 0))]
    lat_tiles = R_LAT // tm
    return [pl.BlockSpec((tm, width), lambda t: (jnp.minimum(t, lat_tiles - 1), 0)),
            pl.BlockSpec((tm, width), lambda t: (jnp.maximum(t - lat_tiles, 0), 0))]


def _row_tile(refs, tm):
    if len(refs) == 1:
        return refs[0][...]
    return jnp.where(pl.program_id(0) < R_LAT // tm, refs[0][...], refs[1][...])


def _ffn_kernel(*refs, final, nx, tm):
    xs, (m_ref, g_ref, wg_ref, wu_ref, wd_ref), rest = refs[:nx], refs[nx:nx + 5], refs[nx + 5:]
    if final:
        fg_ref, o_ref = rest
    else:
        (o_ref,) = rest
    x = _row_tile(xs, tm)
    m = m_ref[...]
    shift, scale, gate = m[:, :D], m[:, D:2 * D], m[:, 2 * D:]
    h = (_rms(x, g_ref[...]) * (1.0 + scale) + shift).astype(BF16)
    a = _dot(h, wg_ref[...])
    u = _dot(h, wu_ref[...])
    act = (a * _sigmoid(a) * u).astype(BF16)
    y = _dot(act, wd_ref[...])
    out = x + (HALF_STEP * gate) * y
    if final:
        out = _rms(out, fg_ref[...])
    o_ref[...] = out


def _ffn(xs, mods, layer, slot, norm_g, wg, wu, wd, rows, final_g=None):
    tm = 512
    final = final_g is not None
    in_specs = _row_specs(tm, D, len(xs) == 2) + [
        pl.BlockSpec((None, None, 1, 3 * D), lambda t: (layer, _mod_row(t, tm), 0, slot)),
        _const_spec((1, D)),
        _layer_spec((D, D_FF), layer),
        _layer_spec((D, D_FF), layer),
        _layer_spec((D_FF, D), layer),
    ]
    args = list(xs) + [mods, norm_g.reshape(1, D), wg, wu, wd]
    if final:
        in_specs.append(_const_spec((1, D)))
        args.append(final_g.reshape(1, D))
    return pl.pallas_call(
        functools.partial(_ffn_kernel, final=final, nx=len(xs), tm=tm),
        grid=(rows // tm,),
        in_specs=in_specs,
        out_specs=pl.BlockSpec((tm, D), lambda t: (t, 0)),
        out_shape=jax.ShapeDtypeStruct((rows, D), F32),
        compiler_params=_cparams(("arbitrary",)),
        name="ffn",
    )(*args)


def _proj_kernel(x_ref, m_ref, g_ref, wkvq_ref, wahi_ref, walo_ref, wr_ref, wh_ref, wg_ref, cw_ref, cb_ref,
                 kvq_ref, a_ref, r_ref, hy_ref, gt_ref, *, tm):
    x = x_ref[...]
    m = m_ref[...]
    shift, scale = m[:, :D], m[:, D:2 * D]
    h = _rms(x, g_ref[...]) * (1.0 + scale) + shift
    hb = h.astype(BF16)
    period = jnp.where(pl.program_id(0) < R_LAT // tm, GRID_W, CTX)
    pos = lax.broadcasted_iota(jnp.int32, (tm, 1), 0) & (period - 1)
    first, last = pos == 0, pos == period - 1
    for part in range(3):
        cols = slice(part * HY_W, (part + 1) * HY_W)
        u = _dot(hb, wh_ref[:, cols])
        prv = jnp.where(first, 0.0, pltpu.roll(u, 1, 0))
        nxt = jnp.where(last, 0.0, pltpu.roll(u, tm - 1, 0))
        conv = cw_ref[0:1, cols] * prv + cw_ref[1:2, cols] * u + cw_ref[2:3, cols] * nxt + cb_ref[:, cols]
        hy_ref[:, cols] = conv.astype(hy_ref.dtype)
    kvq_ref[...] = _dot(hb, wkvq_ref[...]).astype(kvq_ref.dtype)
    r_ref[...] = _dot(hb, wr_ref[...]).astype(r_ref.dtype)
    gt_ref[...] = _dot(hb, wg_ref[...]).astype(gt_ref.dtype)
    hl = (h - hb.astype(F32)).astype(BF16)
    a_ref[...] = _dot(hb, wahi_ref[...]) + _dot(hl, wahi_ref[...]) + _dot(hb, walo_ref[...])


def _proj(x, mods, layer, norm_g, wp, wa_hi, wa_lo, conv_w, conv_b):
    tm = 512
    nk = 2 * KEY_W + VAL_W
    na = 2 * LOWRANK
    row = lambda w: pl.BlockSpec((tm, w), lambda t: (t, 0))

    def wcols(width, blk):
        return pl.BlockSpec((None, D, width), lambda t: (layer, 0, blk), pipeline_mode=pl.Buffered(1))

    return pl.pallas_call(
        functools.partial(_proj_kernel, tm=tm),
        grid=(R_ALL // tm,),
        in_specs=[
            row(D),
            pl.BlockSpec((None, None, 1, 3 * D), lambda t: (layer, _mod_row(t, tm), 0, 1)),
            _const_spec((1, D)),
            wcols(nk, 0),
            wcols(na, 0),
            wcols(na, 0),
            wcols(VAL_W, nk // VAL_W),
            wcols(3 * HY_W, 1),
            wcols(2 * D, 3),
            pl.BlockSpec((None, 3, 3 * HY_W), lambda t: (layer, 0, 0)),
            pl.BlockSpec((None, 1, 3 * HY_W), lambda t: (layer, 0, 0)),
        ],
        out_specs=[row(nk), row(na), row(VAL_W), row(3 * HY_W), row(2 * D)],
        out_shape=[
            jax.ShapeDtypeStruct((R_ALL, nk), BF16),
            jax.ShapeDtypeStruct((R_ALL, na), F32),
            jax.ShapeDtypeStruct((R_ALL, VAL_W), BF16),
            jax.ShapeDtypeStruct((R_ALL, 3 * HY_W), BF16),
            jax.ShapeDtypeStruct((R_ALL, 2 * D), BF16),
        ],
        compiler_params=_cparams(("arbitrary",)),
        name="proj",
    )(x, mods, norm_g.reshape(1, D), wp, wa_hi, wa_lo, wp, wp, wp, conv_w, conv_b)


GLA_BLK = 256
GLA_STEPS = 1 + SEQ // GLA_BLK


def _gla_row_block(b, d, s):
    lat = jnp.where(d == 0, s - 1, SEQ // GLA_BLK - s)
    return jnp.where(s == 0, R_LAT // GLA_BLK + b, b * (SEQ // GLA_BLK) + lat)


def _dot_nt(a, b):
    return lax.dot_general(a, b, (((1,), (1,)), ((), ())), preferred_element_type=F32)


def _dot_tn(a, b):
    return lax.dot_general(a, b, (((0,), (0,)), ((), ())), preferred_element_type=F32)


def _gla_body(kvq_ref, a_ref, kn_ref, qn_ref, an_ref, wa_ref, ba_ref, tri_ref, o_ref,
              st_ref, qd0, qd1, kn0, kn1, ke0, ke1, bc0, bc1):
    qd_s, kn_s, ke_s, bc_s = (qd0, qd1), (kn0, kn1), (ke0, ke1), (bc0, bc1)
    d = pl.program_id(1)
    s = pl.program_id(2)
    nchunk = GLA_BLK // CHUNK
    tot_row = jnp.where(d == 0, CHUNK - 1, 0)

    def log_gates(a_blk):
        a_hi, a_lo = _split(a_blk)
        pre = _dot(jnp.concatenate([a_hi, a_lo, a_hi], axis=1), wa_ref[...]) + ba_ref[...]
        la = (jnp.minimum(pre, 0.0) - jnp.log(1.0 + jnp.exp(-jnp.abs(pre)))) * (1.0 / TAU)
        return _split(la)

    def decay_weight(slot, la_hi, la_lo, k_blk, q_blk):
        tri = tri_ref[...]
        bcum = _dot(tri, la_hi) + _dot(tri, la_lo)
        bc_s[slot][...] = bcum
        btot = jnp.concatenate(
            [jnp.broadcast_to(bc_s[slot][pl.ds(j * CHUNK + tot_row, 1), :], (CHUNK, KEY_W)) for j in range(nchunk)],
            axis=0)
        k = k_blk.astype(F32)
        q = q_blk.astype(F32)
        qd_s[slot][...] = (q * (DK ** -0.5) * jnp.exp(bcum)).astype(BF16)
        kn_s[slot][...] = (k * jnp.exp(-bcum)).astype(BF16)
        ke_s[slot][...] = (k * jnp.exp(btot - bcum)).astype(BF16)

    @pl.when(s == 0)
    def _():
        st_ref[...] = jnp.zeros_like(st_ref)
        decay_weight(0, *log_gates(a_ref[...]), kvq_ref[:, :KEY_W], kvq_ref[:, KEY_W + VAL_W:])

    def scan_block(cur):
        ri = lax.broadcasted_iota(jnp.int32, (GLA_BLK, GLA_BLK), 0)
        ci = lax.broadcasted_iota(jnp.int32, (GLA_BLK, GLA_BLK), 1)
        seen = ((ri // CHUNK) == (ci // CHUNK)) & ((ri - ci) * (1 - 2 * d) >= 0)

        def intra(h):
            ksl = slice(h * DK, (h + 1) * DK)
            att = jnp.where(seen, _dot_nt(qd_s[cur][:, ksl], kn_s[cur][:, ksl]), 0.0).astype(BF16)
            o_ref[:, h * DV:(h + 1) * DV] = _dot(att, kvq_ref[:, KEY_W + h * DV:KEY_W + (h + 1) * DV])

        def inter(j):
            r0 = pl.multiple_of(jnp.where(d == 0, j, nchunk - 1 - j) * CHUNK, CHUNK)
            rows = pl.ds(r0, CHUNK)
            dec = jnp.exp(bc_s[cur][pl.ds(r0 + tot_row, 1), :])
            for h in range(HEADS):
                ksl = slice(h * DK, (h + 1) * DK)
                vsl = slice(h * DV, (h + 1) * DV)
                st = st_ref[h]
                o_ref[rows, vsl] += _dot_nt(qd_s[cur][rows, ksl], st.astype(BF16))
                upd = _dot_tn(kvq_ref[rows, KEY_W + h * DV:KEY_W + (h + 1) * DV], ke_s[cur][rows, ksl])
                st_ref[h] = st * dec[:, ksl] + upd

        la_hi, la_lo = log_gates(an_ref[...])
        for h in range(HEADS):
            intra(h)
        decay_weight(1 - cur, la_hi, la_lo, kn_ref[...], qn_ref[...])
        for j in range(nchunk):
            inter(j)

    for parity in range(2):
        pl.when(s % 2 == parity)(functools.partial(scan_block, parity))


def _gla(kvq, a, wa_cat, ba_cat, tri):
    nk = 2 * KEY_W + VAL_W
    row_map = lambda b, d, s: (_gla_row_block(b, d, s), 0)
    nxt = lambda b, d, s: _gla_row_block(b, d, jnp.minimum(s + 1, GLA_STEPS - 1))
    return pl.pallas_call(
        _gla_body,
        grid=(NB, 2, GLA_STEPS),
        in_specs=[
            pl.BlockSpec((GLA_BLK, nk), row_map),
            pl.BlockSpec((GLA_BLK, 2 * LOWRANK), row_map),
            pl.BlockSpec((GLA_BLK, KEY_W), lambda b, d, s: (nxt(b, d, s), 0)),
            pl.BlockSpec((GLA_BLK, KEY_W), lambda b, d, s: (nxt(b, d, s), (KEY_W + VAL_W) // KEY_W)),
            pl.BlockSpec((GLA_BLK, 2 * LOWRANK), lambda b, d, s: (nxt(b, d, s), 0)),
            pl.BlockSpec((None, 6 * LOWRANK, KEY_W), lambda b, d, s: (d, 0, 0)),
            pl.BlockSpec((None, 1, KEY_W), lambda b, d, s: (d, 0, 0)),
            pl.BlockSpec((None, GLA_BLK, GLA_BLK), lambda b, d, s: (d, 0, 0)),
        ],
        out_specs=pl.BlockSpec((None, GLA_BLK, VAL_W), lambda b, d, s: (d, _gla_row_block(b, d, s), 0)),
        out_shape=jax.ShapeDtypeStruct((2, R_ALL, VAL_W), F32),
        scratch_shapes=[
            pltpu.VMEM((HEADS, DV, DK), F32),
        ] + [pltpu.VMEM((GLA_BLK, KEY_W), BF16)] * 6 + [pltpu.VMEM((GLA_BLK, KEY_W), F32)] * 2,
        compiler_params=_cparams(("arbitrary", "arbitrary", "arbitrary")),
        name="gla",
    )(kvq, a, kvq, kvq, a, wa_cat, ba_cat, tri)


def _mix_kernel(x_ref, m_ref, of_ref, ob_ref, r_ref, gt_ref, ng_ref, wgo_ref, who_ref, wo_ref, *rest, tm):
    zs, o_ref = rest[:-1], rest[-1]
    x = x_ref[...]
    gate = m_ref[...][:, 2 * D:]
    o = of_ref[...] + ob_ref[...]
    ng = ng_ref[...]
    parts = []
    for h in range(HEADS):
        sl = slice(h * DV, (h + 1) * DV)
        oh = o[:, sl]
        ms = jnp.mean(oh * oh, axis=-1, keepdims=True)
        parts.append(oh * lax.rsqrt(ms + EPS) * ng[:, sl])
    on = jnp.concatenate(parts, axis=-1)
    r = r_ref[...].astype(F32)
    y_gla = _dot((on * (r * _sigmoid(r))).astype(BF16), wgo_ref[...])
    y_hy = _dot(_row_tile(zs, tm), who_ref[...])
    gt = gt_ref[...].astype(F32)
    mix = _sigmoid(gt[:, :D]) * y_gla + _sigmoid(gt[:, D:]) * y_hy
    o_ref[...] = x + gate * _dot(mix.astype(BF16), wo_ref[...])


def _mix(x, mods, layer, o2, r, zs, gt, norm_g, wgo, who, wo, rows):
    tm = 512
    row = lambda w: pl.BlockSpec((tm, w), lambda t: (t, 0))
    return pl.pallas_call(
        functools.partial(_mix_kernel, tm=tm),
        grid=(rows // tm,),
        in_specs=[
            row(D),
            pl.BlockSpec((None, None, 1, 3 * D), lambda t: (layer, _mod_row(t, tm), 0, 1)),
            pl.BlockSpec((None, tm, VAL_W), lambda t: (0, t, 0)),
            pl.BlockSpec((None, tm, VAL_W), lambda t: (1, t, 0)),
            row(VAL_W),
            row(2 * D),
            _const_spec((1, VAL_W)),
            _layer_spec((VAL_W, D), layer),
            _layer_spec((HY_W, D), layer),
            _layer_spec((D, D), layer),
        ] + _row_specs(tm, HY_W, len(zs) == 2),
        out_specs=row(D),
        out_shape=jax.ShapeDtypeStruct((rows, D), F32),
        compiler_params=_cparams(("arbitrary",)),
        name="mix",
    )(x, mods, o2, o2, r, gt, norm_g.reshape(1, VAL_W), wgo, who, wo, *zs)


def _stack3_lhs(a):
    hi, lo = _split(a)
    return jnp.concatenate([hi, lo, hi], axis=1)


def _stack3_rhs(w):
    hi, lo = _split(w)
    return jnp.concatenate([hi, hi, lo], axis=0)


def _filt_mlp_kernel(z_ref, w1_ref, b1_ref, f1_ref, w2_ref, b2_ref, f2_ref, o_ref):
    h = jnp.sin(f1_ref[...] * (_dot(_stack3_lhs(z_ref[...]), w1_ref[...]) + b1_ref[...]))
    h = jnp.sin(f2_ref[...] * (_dot(_stack3_lhs(h), w2_ref[...]) + b2_ref[...]))
    o_ref[...] = _stack3_lhs(h)


def _filt_mlp(feats, w1, b1, f1, w2, b2, f2):
    rows = feats.shape[0]
    tm = 512
    vec = lambda: _const_spec((1, HY_HID))
    return pl.pallas_call(
        _filt_mlp_kernel,
        grid=(rows // tm,),
        in_specs=[pl.BlockSpec((tm, HY_HID), lambda t: (t, 0)), _const_spec((3 * HY_HID, HY_HID)), vec(), vec(),
                  _const_spec((3 * HY_HID, HY_HID)), vec(), vec()],
        out_specs=pl.BlockSpec((tm, 3 * HY_HID), lambda t: (t, 0)),
        out_shape=jax.ShapeDtypeStruct((rows, 3 * HY_HID), BF16),
        compiler_params=_cparams(("arbitrary",)),
        name="filt_mlp",
    )(feats, _stack3_rhs(w1), b1.reshape(1, HY_HID), f1.reshape(1, HY_HID), _stack3_rhs(w2),
      b2.reshape(1, HY_HID), f2.reshape(1, HY_HID))


def _taps_chunk(h2s, t, w3f, w3b, delta, fwd, zero):
    win = jnp.exp(-t * delta)
    h3 = _dot(h2s, jnp.concatenate([w3f, w3b], axis=1))
    return jnp.where(zero, 0.0, win * jnp.where(fwd, h3[:, :LANES], h3[:, LANES:]))


def _stage_b_fwd(buf_a, j, e_ref):
    cols = []
    for k1 in (2 * j, 2 * j + 1):
        re = buf_a[pl.ds(k1, N2, stride=PITCH_A), :]
        im = buf_a[pl.ds(N1 + k1, N2, stride=PITCH_A), :]
        cols.append(jnp.concatenate([re, im], axis=0).astype(BF16))
    return _dot(e_ref[...], jnp.concatenate(cols, axis=1))


def _spec_lat_kernel(h2_ref, rc_ref, w3f_ref, w3b_ref, dl_ref, wt_ref, e_ref, o_ref, taps, buf_a):
    tb = 512
    delta = dl_ref[...]

    ridx = lax.broadcasted_iota(jnp.int32, (tb, LANES), 0)
    fwd = (ridx & (N1 - 1)) < N1 // 2

    def fill(i, acc):
        rows = pl.ds(pl.multiple_of(i * tb, tb), tb)
        zero = ridx == jnp.where(i == 0, N1 // 2, -1)
        tp = _taps_chunk(h2_ref[rows, :], rc_ref[rows, 0:1], w3f_ref[...], w3b_ref[...], delta, fwd, zero)
        taps[rows, :] = tp
        return acc + jnp.sum(jnp.abs(tp), axis=0, keepdims=True)

    l1 = lax.fori_loop(0, FFT_N // tb, fill, jnp.zeros((1, LANES), F32))
    inv = 1.0 / (l1 + EPS)

    def stage_a(n2, c):
        rhs = taps[pl.ds(pl.multiple_of(n2 * N1, N1), N1), :]
        buf_a[pl.ds(pl.multiple_of(n2 * PITCH_A, 8), 2 * N1), :] = _dot(wt_ref[n2], rhs.astype(BF16))
        return c

    lax.fori_loop(0, N2, stage_a, 0, unroll=UNROLL)

    def stage_b(j, c):
        y = _stage_b_fwd(buf_a, j, e_ref)
        rows = 2 * N2
        o_ref[pl.ds(pl.multiple_of(2 * j * rows, 2 * rows), rows), :] = (y[:, :LANES] * inv).astype(o_ref.dtype)
        o_ref[pl.ds(pl.multiple_of(2 * j * rows + rows, rows), rows), :] = (y[:, LANES:] * inv).astype(o_ref.dtype)
        return c

    lax.fori_loop(0, N1 // 2, stage_b, 0, unroll=UNROLL)


def _spec_lat(h2p, rcp, w3, delta, cst):
    nct = HY_W // LANES
    return pl.pallas_call(
        _spec_lat_kernel,
        grid=(2, nct),
        in_specs=[
            _const_spec((FFT_N, 3 * HY_HID)),
            _const_spec((FFT_N, 8)),
            pl.BlockSpec((3 * HY_HID, LANES), lambda o, c: (0, o * nct + c)),
            pl.BlockSpec((3 * HY_HID, LANES), lambda o, c: (0, 2 * nct + o * nct + c)),
            pl.BlockSpec((1, LANES), lambda o, c: (0, c)),
            _const_spec((N2, 2 * N1, N1)),
            _const_spec((2 * N2, 2 * N2)),
        ],
        out_specs=pl.BlockSpec((None, None, 2 * FFT_N, LANES), lambda o, c: (o, c, 0, 0)),
        out_shape=jax.ShapeDtypeStruct((2, nct, 2 * FFT_N, LANES), BF16),
        scratch_shapes=[pltpu.VMEM((FFT_N, LANES), F32), pltpu.VMEM((N2 * PITCH_A, LANES), F32)],
        compiler_params=_cparams(("arbitrary", "arbitrary")),
        name="spec_lat",
    )(h2p, rcp, w3, w3, delta, cst["wt"], cst["e"])


def _hy_lat_kernel(ua_ref, ub_ref, h_ref, bias_ref, wa_ref, e_ref, g_ref, mi_ref,
                   o_ref, zin, buf_a, buf_d, upad):
    s = pl.program_id(2)

    def zslab(n2):
        return zin[pl.ds(pl.multiple_of(n2 * PITCH_Z, 8), 2 * N2), :]

    def repitch(n1, c):
        src = pl.ds(pl.multiple_of(n1 * GRID_W, GRID_W), GRID_W)
        dst = pl.ds(pl.multiple_of(n1 * PITCH_U, 8), GRID_W)
        upad[0, dst, :] = ua_ref[src, :].astype(F32)
        upad[1, dst, :] = ub_ref[src, :].astype(F32)
        return c
    lax.fori_loop(0, N2, repitch, 0, unroll=UNROLL)

    def col_slices(n2):
        return jnp.concatenate([upad[j, pl.ds(n2, N2, stride=PITCH_U), :] for j in range(2)], axis=0)

    @pl.when(s == 0)
    def _():
        def body(n2, c):
            zin[pl.ds(pl.multiple_of(n2 * PITCH_Z, 8), 2 * N2), :] = col_slices(n2)
            return c
        lax.fori_loop(0, N2, body, 0, unroll=UNROLL)

    @pl.when(s > 0)
    def _():
        def stage_a(n2, c):
            buf_a[pl.ds(pl.multiple_of(n2 * PITCH_A, 8), 2 * N1), :] = _dot(wa_ref[n2], zslab(n2).astype(BF16))
            return c
        lax.fori_loop(0, N2, stage_a, 0, unroll=UNROLL)

        def stage_b(j, c):
            y = _stage_b_fwd(buf_a, j, e_ref)
            prods = []
            for i in range(2):
                yk = y[:, i * LANES:(i + 1) * LANES]
                hb = h_ref[pl.ds(pl.multiple_of((2 * j + i) * 2 * N2, 2 * N2), 2 * N2), :].astype(F32)
                yr, yi = yk[:N2], yk[N2:]
                hr, hi = hb[:N2], hb[N2:]
                prods.append(jnp.concatenate([yr * hr - yi * hi, yr * hi + yi * hr], axis=0).astype(BF16))
            d = _dot(g_ref[...], jnp.concatenate(prods, axis=1))
            for i in range(2):
                rows = pl.ds(pl.multiple_of((2 * j + i) * PITCH_D, 8), 2 * N2)
                buf_d[rows, :] = d[:, i * LANES:(i + 1) * LANES]
            return c
        lax.fori_loop(0, N1 // 2, stage_b, 0, unroll=2 * UNROLL)

        def stage_ai(n2, c):
            re = buf_d[pl.ds(n2, N1, stride=PITCH_D), :]
            im = buf_d[pl.ds(N2 + n2, N1, stride=PITCH_D), :]
            x = jnp.concatenate([re, im], axis=0).astype(BF16)
            y = _dot(mi_ref[n2], x)
            rows = pl.ds(pl.multiple_of(n2 * PITCH_Z, 8), 2 * N2)
            zin[rows, :] = col_slices(n2) * (y + bias_ref[...] * zin[rows, :])
            return c
        lax.fori_loop(0, N2, stage_ai, 0, unroll=UNROLL)

    @pl.when(s == 2)
    def _():
        def emit(n1, c):
            dst = pl.ds(pl.multiple_of(n1 * GRID_W, GRID_W), GRID_W)
            o_ref[0, dst, :] = zin[pl.ds(n1, GRID_W, stride=PITCH_Z), :].astype(o_ref.dtype)
            o_ref[1, dst, :] = zin[pl.ds(N2 + n1, GRID_W, stride=PITCH_Z), :].astype(o_ref.dtype)
            return c
        lax.fori_loop(0, N2, emit, 0, unroll=UNROLL)


def _hy_lat(hy, hspec, bias, cst):
    nct = HY_W // LANES
    part = lambda c, p, s: s * nct + c
    order = lambda s: jnp.maximum(s - 1, 0)
    consts = [cst[n] for n in ("wa", "e", "g", "mi")]
    return pl.pallas_call(
        _hy_lat_kernel,
        grid=(nct, NB // 2, 3),
        in_specs=[
            pl.BlockSpec((SEQ, LANES), lambda c, p, s: (2 * p, part(c, p, s))),
            pl.BlockSpec((SEQ, LANES), lambda c, p, s: (2 * p + 1, part(c, p, s))),
            pl.BlockSpec((None, None, 2 * FFT_N, LANES), lambda c, p, s: (order(s), c, 0, 0)),
            pl.BlockSpec((None, 1, LANES), lambda c, p, s: (order(s), 0, c)),
        ] + [_const_spec(x.shape) for x in consts],
        out_specs=pl.BlockSpec((2, SEQ, LANES), lambda c, p, s: (p, 0, c)),
        out_shape=jax.ShapeDtypeStruct((NB, SEQ, HY_W), BF16),
        scratch_shapes=[
            pltpu.VMEM((N2 * PITCH_Z, LANES), F32),
            pltpu.VMEM((N2 * PITCH_A, LANES), F32),
            pltpu.VMEM((N1 * PITCH_D, LANES), F32),
            pltpu.VMEM((2, N2 * PITCH_U, LANES), F32),
        ],
        compiler_params=_cparams(("arbitrary", "arbitrary", "arbitrary")),
        name="hy_lat",
    )(hy, hy, hspec, bias, *consts)


def _spec_ctx_kernel(h2_ref, rc_ref, w3f_ref, w3b_ref, dl_ref, wt_ref, o_ref):
    ridx = lax.broadcasted_iota(jnp.int32, (CTX_N, LANES), 0)
    tp = _taps_chunk(h2_ref[...], rc_ref[:, 0:1], w3f_ref[...], w3b_ref[...], dl_ref[...], ridx < CTX, ridx == CTX)
    l1 = jnp.sum(jnp.abs(tp), axis=0, keepdims=True)
    o_ref[...] = _dot(wt_ref[...], (tp * (1.0 / (l1 + EPS))).astype(BF16))


def _spec_ctx(h2, rc, w3, delta, cst):
    nct = HY_W // LANES
    return pl.pallas_call(
        _spec_ctx_kernel,
        grid=(2, nct),
        in_specs=[
            _const_spec((CTX_N, 3 * HY_HID)),
            _const_spec((CTX_N, 8)),
            pl.BlockSpec((3 * HY_HID, LANES), lambda o, c: (0, o * nct + c)),
            pl.BlockSpec((3 * HY_HID, LANES), lambda o, c: (0, 2 * nct + o * nct + c)),
            pl.BlockSpec((1, LANES), lambda o, c: (0, c)),
            _const_spec((2 * CTX_N, CTX_N)),
        ],
        out_specs=pl.BlockSpec((None, None, 2 * CTX_N, LANES), lambda o, c: (o, c, 0, 0)),
        out_shape=jax.ShapeDtypeStruct((2, nct, 2 * CTX_N, LANES), F32),
        compiler_params=_cparams(("arbitrary", "arbitrary")),
        name="spec_ctx",
    )(h2, rc, w3, w3, delta, cst["cwt"])


def _hy_ctx_kernel(va, vb, x1a, x1b, x2a, x2b, h_ref, bias_ref, wf_ref, wi_ref, o_ref):
    def pair(ra, rb):
        return jnp.concatenate([ra[...], rb[...]], axis=0).astype(F32)

    z = pair(va, vb)
    gates = (pair(x1a, x1b), pair(x2a, x2b))
    for o in range(2):
        y = _dot(wf_ref[...], z.astype(BF16))
        hb = h_ref[o]
        yr, yi = y[:CTX_N], y[CTX_N:]
        hr, hi = hb[:CTX_N], hb[CTX_N:]
        prod = jnp.concatenate([yr * hr - yi * hi, yr * hi + yi * hr], axis=0)
        conv = _dot(wi_ref[...], prod.astype(BF16))
        z = gates[o] * (conv + bias_ref[o] * z)
    o_ref[0] = z[:CTX].astype(o_ref.dtype)
    o_ref[1] = z[CTX:].astype(o_ref.dtype)


def _hy_ctx(hy, hspec, bias, cst):
    nct = HY_W // LANES
    base = R_LAT // CTX

    def blk(j, part):
        return pl.BlockSpec((CTX, LANES), lambda c, p: (base + 2 * p + j, part * nct + c))

    consts = [cst["cwf"], cst["cwi"]]
    return pl.pallas_call(
        _hy_ctx_kernel,
        grid=(nct, NB // 2),
        in_specs=[blk(0, 0), blk(1, 0), blk(0, 1), blk(1, 1), blk(0, 2), blk(1, 2),
                  pl.BlockSpec((2, None, 2 * CTX_N, LANES), lambda c, p: (0, c, 0, 0)),
                  pl.BlockSpec((2, 1, LANES), lambda c, p: (0, 0, c))]
        + [_const_spec(x.shape) for x in consts],
        out_specs=pl.BlockSpec((2, CTX, LANES), lambda c, p: (p, 0, c)),
        out_shape=jax.ShapeDtypeStruct((NB, CTX, HY_W), BF16),
        compiler_params=_cparams(("arbitrary", "arbitrary")),
        name="hy_ctx",
    )(hy, hy, hy, hy, hy, hy, hspec, bias, *consts)


def _stack(c):
    return np.block([[c.real, -c.imag], [c.imag, c.real]])


def _fft_constants():
    cst = {}
    k1 = np.arange(N1)[:, None]
    ca = np.exp(-2j * np.pi * k1 * np.arange(N1 // 2)[None, :] / N1)
    tw = np.exp(-2j * np.pi * k1 * np.arange(N2)[None, :] / FFT_N)
    ct = np.exp(-2j * np.pi * k1 * np.arange(N1)[None, :] / N1)
    cst["wa"] = jnp.asarray(np.stack([_stack(tw[:, n2:n2 + 1] * ca) for n2 in range(N2)]), F32).astype(BF16)
    wt = [tw[:, n2:n2 + 1] * ct for n2 in range(N2)]
    cst["wt"] = jnp.asarray(np.stack([np.concatenate([m.real, m.imag], axis=0) for m in wt]), F32).astype(BF16)
    e = np.exp(-2j * np.pi * np.arange(N2)[:, None] * np.arange(N2)[None, :] / N2)
    cst["e"] = jnp.asarray(_stack(e), F32).astype(BF16)
    cst["g"] = jnp.asarray(_stack(np.conj(e).T), F32).astype(BF16)
    bi = np.exp(2j * np.pi * np.arange(N1 // 2)[:, None] * np.arange(N1)[None, :] / N1) / FFT_N
    cst["mi"] = jnp.asarray(np.stack([_stack(bi * np.conj(tw[:, n2])[None, :]) for n2 in range(N2)]),
                            F32).astype(BF16)
    kk = np.arange(CTX_N)[:, None]
    cf = np.exp(-2j * np.pi * kk * np.arange(CTX)[None, :] / CTX_N)
    cst["cwf"] = jnp.asarray(_stack(cf), F32).astype(BF16)
    cfull = np.exp(-2j * np.pi * kk * np.arange(CTX_N)[None, :] / CTX_N)
    cst["cwt"] = jnp.asarray(np.concatenate([cfull.real, cfull.imag], axis=0), F32).astype(BF16)
    ci = np.exp(2j * np.pi * np.arange(CTX)[:, None] * np.arange(CTX_N)[None, :] / CTX_N) / CTX_N
    cst["cwi"] = jnp.asarray(_stack(ci), F32).astype(BF16)
    return cst


def _filter_positions(L, permute):
    bands = (HY_EMB - 1) // 2
    t = np.linspace(0.0, 1.0, L)[:, None]
    w = (2.0 * math.pi / L) * np.arange(L)[:, None]
    f = np.linspace(1e-4, bands - 1, bands)[None, :]
    z = np.concatenate([t, np.cos(f * w), -np.sin(f * w)], axis=-1)
    n = np.arange(2 * L)
    pos = np.where(n < L, n, 2 * L - n)
    pos = np.where(n == L, 0, pos)
    fsel = (n < L).astype(np.float32)
    bsel = (n > L).astype(np.float32)
    if permute:
        order = (np.arange(N1)[None, :] * N2 + np.arange(N2)[:, None]).reshape(-1)
        pos, fsel, bsel = pos[order], fsel[order], bsel[order]
    feats = np.pad(z[pos], ((0, 0), (0, HY_HID - HY_EMB)))
    rc = np.concatenate([t[pos], fsel[:, None], bsel[:, None], np.zeros((2 * L, 5))], axis=1)
    return jnp.asarray(feats, F32), jnp.asarray(rc, F32)


def _decay_rates():
    max_decay = math.log(HY_TARGET) / HY_FAST
    min_decay = math.log(HY_TARGET) / HY_SLOW
    return jnp.asarray(np.abs(np.linspace(min_decay, max_decay, HY_W))[None, :], F32)


def _gla_constants():
    r = np.arange(GLA_BLK)
    same = (r[:, None] // CHUNK) == (r[None, :] // CHUNK)
    fwd = same & (r[:, None] >= r[None, :])
    bwd = same & (r[:, None] <= r[None, :])
    return jnp.asarray(np.stack([fwd, bwd]).astype(np.float32)).astype(BF16)


def kernel(x, c, ctx, c_ctx, ada_w, ada_b, ffn1_norm_g, ffn1_w_gate, ffn1_w_up, ffn1_w_down, mix_norm_g, w_in,
           gla_wa_f, gla_ba_f, gla_wa_b, gla_ba_b, gla_norm_g, w_gla_out, hy_conv_w, hy_conv_b, hy_f_w1, hy_f_b1,
           hy_f_freq1, hy_f_w2, hy_f_b2, hy_f_freq2, hy_f_w3, hy_bias, w_hy_out, w_out, ffn2_norm_g, ffn2_w_gate,
           ffn2_w_up, ffn2_w_down, final_norm_g):
    cst = _fft_constants()
    tri = _gla_constants()
    delta = _decay_rates()
    feats_lat, rc_lat = _filter_positions(SEQ, True)
    feats_ctx, rc_ctx = _filter_positions(CTX, False)

    cvec = jnp.concatenate([c, c_ctx[None, :], jnp.zeros((3, D), F32)], axis=0)
    mods = _mods(cvec, ada_w, ada_b).reshape(DEPTH, 8, 1, N_MOD * D)
    xs = (x.reshape(R_LAT, D), ctx.reshape(R_CTX, D))

    bf = lambda w: w.astype(BF16)
    ffn1_w = (bf(ffn1_w_gate), bf(ffn1_w_up), bf(ffn1_w_down))
    ffn2_w = (bf(ffn2_w_gate), bf(ffn2_w_up), bf(ffn2_w_down))
    out_w = (bf(w_gla_out), bf(w_hy_out), bf(w_out))
    wp = bf(jnp.concatenate([w_in[:, :, OFF_K:OFF_A], w_in[:, :, OFF_Q:OFF_R], w_in[:, :, OFF_R:IN_W]], axis=2))
    wa_hi, wa_lo = _split(w_in[:, :, OFF_A:OFF_Q])
    conv_b = hy_conv_b[:, None, :]

    zpad = jnp.zeros((LOWRANK, KEY_W), F32)
    for i in range(DEPTH):
        last = i == DEPTH - 1
        xa = _ffn(xs, mods, i, 0, ffn1_norm_g[i], *ffn1_w, R_ALL)
        kvq, a, r, hy, gt = _proj(xa, mods, i, mix_norm_g[i], wp, wa_hi, wa_lo, hy_conv_w, conv_b)

        wa_cat = jnp.stack([jnp.concatenate([gla_wa_f[i], zpad], axis=0),
                            jnp.concatenate([zpad, gla_wa_b[i]], axis=0)])
        w_hi, w_lo = _split(wa_cat)
        ba_cat = jnp.stack([gla_ba_f[i], gla_ba_b[i]])[:, None, :]
        o2 = _gla(kvq, a, jnp.concatenate([w_hi, w_hi, w_lo], axis=1), ba_cat, tri)

        w1p = jnp.pad(hy_f_w1[i], ((0, HY_HID - HY_EMB), (0, 0)))
        mlp = (w1p, hy_f_b1[i], hy_f_freq1[i], hy_f_w2[i], hy_f_b2[i], hy_f_freq2[i])
        bias = hy_bias[i][:, None, :]
        w3s = _stack3_rhs(hy_f_w3[i])
        spec_lat = _spec_lat(_filt_mlp(feats_lat, *mlp), rc_lat, w3s, delta, cst)
        zs = (_hy_lat(hy, spec_lat, bias, cst).reshape(R_LAT, HY_W),)

        rows = R_LAT if last else R_ALL
        if not last:
            spec_ctx = _spec_ctx(_filt_mlp(feats_ctx, *mlp), rc_ctx, w3s, delta, cst)
            zs += (_hy_ctx(hy, spec_ctx, bias, cst).reshape(R_CTX, HY_W),)
        xa = _mix(xa, mods, i, o2, r, zs, gt, gla_norm_g[i], *out_w, rows)
        xs = (_ffn((xa,), mods, i, 2, ffn2_norm_g[i], *ffn2_w, rows, final_g=final_norm_g if last else None),)
    return xs[0].reshape(NB, SEQ, D)
```

```python
import functools
import math

import numpy as np
import jax
import jax.numpy as jnp
from jax import lax
from jax.experimental import pallas as pl
from jax.experimental.pallas import tpu as pltpu

F32 = jnp.float32
BF16 = jnp.bfloat16

D = 1024
NB = 4
SEQ = 4096
DEPTH = 2
CTX = 256
GRID_W = 64
D_FF = 2816
HALF_STEP = 0.5
N_MOD = 9
EPS = 1e-6

HEADS = 4
DK = 128
DV = 256
LOWRANK = 16
TAU = 16.0
CHUNK = 64
KEY_W = HEADS * DK
VAL_W = HEADS * DV

HY_W = 1024
HY_EMB = 33
HY_HID = 64
HY_FAST = 0.3
HY_SLOW = 1.5
HY_TARGET = 1e-2

OFF_K = 0
OFF_V = OFF_K + KEY_W
OFF_A = OFF_V + VAL_W
OFF_Q = OFF_A + 2 * LOWRANK
OFF_R = OFF_Q + KEY_W
OFF_H = OFF_R + VAL_W
OFF_G = OFF_H + 3 * HY_W
IN_W = OFF_G + 2 * D

R_LAT = NB * SEQ
R_CTX = NB * CTX
R_ALL = R_LAT + R_CTX

LANES = 128
VMEM_LIMIT = 56 * 1024 * 1024

FFT_N = 2 * SEQ
N1 = 128
N2 = 64
PITCH_A = 264
PITCH_D = 136
PITCH_U = 72
PITCH_Z = 136
UNROLL = 8
CTX_N = 2 * CTX


def _cparams(sem):
    return pltpu.CompilerParams(dimension_semantics=sem, vmem_limit_bytes=VMEM_LIMIT)


def _const_spec(shape):
    nd = len(shape)
    return pl.BlockSpec(shape, lambda *_: (0,) * nd, pipeline_mode=pl.Buffered(1))


def _layer_spec(shape, layer):
    nd = len(shape)
    return pl.BlockSpec((None,) + tuple(shape), lambda *_: (layer,) + (0,) * nd, pipeline_mode=pl.Buffered(1))


def _split(x):
    hi = x.astype(BF16)
    lo = (x - hi.astype(F32)).astype(BF16)
    return hi, lo


def _dot(a, b):
    return jnp.dot(a, b, preferred_element_type=F32)


def _sigmoid(x):
    return 1.0 / (1.0 + jnp.exp(-x))


def _rms(x, g):
    ms = jnp.mean(x * x, axis=-1, keepdims=True)
    return x * lax.rsqrt(ms + EPS) * g


def _mods_kernel(c_ref, w_ref, b_ref, o_ref):
    c = c_ref[...]
    s = (c * _sigmoid(c)).astype(BF16)
    o_ref[...] = _dot(s, w_ref[...].astype(BF16)) + b_ref[...]


def _mods(cvec, ada_w, ada_b):
    tn = 1024
    return pl.pallas_call(
        _mods_kernel,
        grid=(DEPTH, N_MOD * D // tn),
        in_specs=[
            pl.BlockSpec((8, D), lambda l, j: (0, 0)),
            pl.BlockSpec((None, D, tn), lambda l, j: (l, 0, j)),
            pl.BlockSpec((None, 1, tn), lambda l, j: (l, 0, j)),
        ],
        out_specs=pl.BlockSpec((None, 8, tn), lambda l, j: (l, 0, j)),
        out_shape=jax.ShapeDtypeStruct((DEPTH, 8, N_MOD * D), F32),
        compiler_params=_cparams(("arbitrary", "arbitrary")),
        name="mods",
    )(cvec, ada_w, ada_b.reshape(DEPTH, 1, N_MOD * D))


def _mod_row(t, tm):
    lat_tiles = R_LAT // tm
    per_batch = SEQ // tm
    return jnp.where(t < lat_tiles, t // per_batch, NB)


def _row_specs(tm, width, split):
    if not split:
        return [pl.BlockSpec((tm, width), lambda t: (t, 0))]
    lat_tiles = R_LAT // tm
    return [pl.BlockSpec((tm, width), lambda t: (jnp.minimum(t, lat_tiles - 1), 0)),
            pl.BlockSpec((tm, width), lambda t: (jnp.maximum(t - lat_tiles, 0), 0))]


def _row_tile(refs, tm):
    if len(refs) == 1:
        return refs[0][...]
    return jnp.where(pl.program_id(0) < R_LAT // tm, refs[0][...], refs[1][...])


def _ffn_kernel(*refs, final, nx, tm):
    xs, (m_ref, g_ref, wg_ref, wu_ref, wd_ref), rest = refs[:nx], refs[nx:nx + 5], refs[nx + 5:]
    if final:
        fg_ref, o_ref = rest
    else:
        (o_ref,) = rest
    x = _row_tile(xs, tm)
    m = m_ref[...]
    shift, scale, gate = m[:, :D], m[:, D:2 * D], m[:, 2 * D:]
    h = (_rms(x, g_ref[...]) * (1.0 + scale) + shift).astype(BF16)
    a = _dot(h, wg_ref[...])
    u = _dot(h, wu_ref[...])
    act = (a * _sigmoid(a) * u).astype(BF16)
    y = _dot(act, wd_ref[...])
    out = x + (HALF_STEP * gate) * y
    if final:
        out = _rms(out, fg_ref[...])
    o_ref[...] = out


def _ffn(xs, mods, layer, slot, norm_g, wg, wu, wd, rows, final_g=None):
    tm = 512
    final = final_g is not None
    in_specs = _row_specs(tm, D, len(xs) == 2) + [
        pl.BlockSpec((None, None, 1, 3 * D), lambda t: (layer, _mod_row(t, tm), 0, slot)),
        _const_spec((1, D)),
        _layer_spec((D, D_FF), layer),
        _layer_spec((D, D_FF), layer),
        _layer_spec((D_FF, D), layer),
    ]
    args = list(xs) + [mods, norm_g.reshape(1, D), wg, wu, wd]
    if final:
        in_specs.append(_const_spec((1, D)))
        args.append(final_g.reshape(1, D))
    return pl.pallas_call(
        functools.partial(_ffn_kernel, final=final, nx=len(xs), tm=tm),
        grid=(rows // tm,),
        in_specs=in_specs,
        out_specs=pl.BlockSpec((tm, D), lambda t: (t, 0)),
        out_shape=jax.ShapeDtypeStruct((rows, D), F32),
        compiler_params=_cparams(("arbitrary",)),
        name="ffn",
    )(*args)


def _proj_kernel(x_ref, m_ref, g_ref, wkvq_ref, wahi_ref, walo_ref, wr_ref, wh_ref, wg_ref, cw_ref, cb_ref,
                 kvq_ref, a_ref, r_ref, hy_ref, gt_ref, *, tm):
    x = x_ref[...]
    m = m_ref[...]
    shift, scale = m[:, :D], m[:, D:2 * D]
    h = _rms(x, g_ref[...]) * (1.0 + scale) + shift
    hb = h.astype(BF16)
    period = jnp.where(pl.program_id(0) < R_LAT // tm, GRID_W, CTX)
    pos = lax.broadcasted_iota(jnp.int32, (tm, 1), 0) & (period - 1)
    first, last = pos == 0, pos == period - 1

    def hyena_part(part):
        cols = slice(part * HY_W, (part + 1) * HY_W)
        u = _dot_nt(hb, wh_ref[cols, :])
        prv = jnp.where(first, 0.0, pltpu.roll(u, 1, 0))
        nxt = jnp.where(last, 0.0, pltpu.roll(u, tm - 1, 0))
        conv = cw_ref[0:1, cols] * prv + cw_ref[1:2, cols] * u + cw_ref[2:3, cols] * nxt + cb_ref[:, cols]
        hy_ref[:, cols] = conv.astype(hy_ref.dtype)

    hyena_part(0)
    kvq_ref[...] = _dot_nt(hb, wkvq_ref[...]).astype(kvq_ref.dtype)
    hyena_part(1)
    gt_ref[...] = _dot_nt(hb, wg_ref[...]).astype(gt_ref.dtype)
    hyena_part(2)
    r_ref[...] = _dot_nt(hb, wr_ref[...]).astype(r_ref.dtype)
    hl = (h - hb.astype(F32)).astype(BF16)
    a_ref[...] = _dot_nt(hb, wahi_ref[...]) + _dot_nt(hl, wahi_ref[...]) + _dot_nt(hb, walo_ref[...])


def _proj(x, mods, layer, norm_g, wp, wa_hi, wa_lo, conv_w, conv_b):
    tm = 512
    nk = 2 * KEY_W + VAL_W
    na = 2 * LOWRANK
    row = lambda w: pl.BlockSpec((tm, w), lambda t: (t, 0))

    def wcols(width, blk):
        return pl.BlockSpec((None, width, D), lambda t: (layer, blk, 0), pipeline_mode=pl.Buffered(1))

    return pl.pallas_call(
        functools.partial(_proj_kernel, tm=tm),
        grid=(R_ALL // tm,),
        in_specs=[
            row(D),
            pl.BlockSpec((None, None, 1, 3 * D), lambda t: (layer, _mod_row(t, tm), 0, 1)),
            _const_spec((1, D)),
            wcols(nk, 0),
            wcols(na, 0),
            wcols(na, 0),
            wcols(VAL_W, nk // VAL_W),
            wcols(3 * HY_W, 1),
            wcols(2 * D, 3),
            pl.BlockSpec((None, 3, 3 * HY_W), lambda t: (layer, 0, 0)),
            pl.BlockSpec((None, 1, 3 * HY_W), lambda t: (layer, 0, 0)),
        ],
        out_specs=[row(nk), row(na), row(VAL_W), row(3 * HY_W), row(2 * D)],
        out_shape=[
            jax.ShapeDtypeStruct((R_ALL, nk), BF16),
            jax.ShapeDtypeStruct((R_ALL, na), F32),
            jax.ShapeDtypeStruct((R_ALL, VAL_W), BF16),
            jax.ShapeDtypeStruct((R_ALL, 3 * HY_W), BF16),
            jax.ShapeDtypeStruct((R_ALL, 2 * D), BF16),
        ],
        compiler_params=_cparams(("arbitrary",)),
        name="proj",
    )(x, mods, norm_g.reshape(1, D), wp, wa_hi, wa_lo, wp, wp, wp, conv_w, conv_b)


GLA_BLK = 256
GLA_STEPS = 1 + SEQ // GLA_BLK


def _gla_row_block(b, d, s):
    lat = jnp.where(d == 0, s - 1, SEQ // GLA_BLK - s)
    return jnp.where(s == 0, R_LAT // GLA_BLK + b, b * (SEQ // GLA_BLK) + lat)


def _dot_nt(a, b):
    return lax.dot_general(a, b, (((1,), (1,)), ((), ())), preferred_element_type=F32)


def _dot_tn(a, b):
    return lax.dot_general(a, b, (((0,), (0,)), ((), ())), preferred_element_type=F32)


def _gla_body(kvq_ref, a_ref, kn_ref, qn_ref, an_ref, wa_ref, ba_ref, tri_ref, o_ref,
              st_ref, qd0, qd1, kn0, kn1, ke0, ke1, bc0, bc1):
    qd_s, kn_s, ke_s, bc_s = (qd0, qd1), (kn0, kn1), (ke0, ke1), (bc0, bc1)
    d = pl.program_id(1)
    s = pl.program_id(2)
    nchunk = GLA_BLK // CHUNK
    tot_row = jnp.where(d == 0, CHUNK - 1, 0)

    def log_gates(a_blk):
        a_hi, a_lo = _split(a_blk)
        pre = _dot(jnp.concatenate([a_hi, a_lo, a_hi], axis=1), wa_ref[...]) + ba_ref[...]
        la = (jnp.minimum(pre, 0.0) - jnp.log(1.0 + jnp.exp(-jnp.abs(pre)))) * (1.0 / TAU)
        return _split(la)

    def cum_decay(slot, la_hi, la_lo):
        tri = tri_ref[...]
        bcum = _dot(tri, la_hi) + _dot(tri, la_lo)
        bc_s[slot][...] = bcum
        return bcum

    def decay_weight(slot, bcum, k_blk, q_blk):
        btot = jnp.concatenate(
            [jnp.broadcast_to(bc_s[slot][pl.ds(j * CHUNK + tot_row, 1), :], (CHUNK, KEY_W)) for j in range(nchunk)],
            axis=0)
        k = k_blk.astype(F32)
        q = q_blk.astype(F32)
        qd_s[slot][...] = (q * (DK ** -0.5) * jnp.exp(bcum)).astype(BF16)
        kn_s[slot][...] = (k * jnp.exp(-bcum)).astype(BF16)
        ke_s[slot][...] = (k * jnp.exp(btot - bcum)).astype(BF16)

    @pl.when(s == 0)
    def _():
        st_ref[...] = jnp.zeros_like(st_ref)
        decay_weight(0, cum_decay(0, *log_gates(a_ref[...])), kvq_ref[:, :KEY_W], kvq_ref[:, KEY_W + VAL_W:])

    def scan_block(cur):
        ri = lax.broadcasted_iota(jnp.int32, (GLA_BLK, GLA_BLK), 0)
        ci = lax.broadcasted_iota(jnp.int32, (GLA_BLK, GLA_BLK), 1)
        seen = ((ri // CHUNK) == (ci // CHUNK)) & ((ri - ci) * (1 - 2 * d) >= 0)

        def intra(h):
            ksl = slice(h * DK, (h + 1) * DK)
            att = jnp.where(seen, _dot_nt(qd_s[cur][:, ksl], kn_s[cur][:, ksl]), 0.0).astype(BF16)
            o_ref[:, h * DV:(h + 1) * DV] = _dot(att, kvq_ref[:, KEY_W + h * DV:KEY_W + (h + 1) * DV])

        def inter(j):
            r0 = pl.multiple_of(jnp.where(d == 0, j, nchunk - 1 - j) * CHUNK, CHUNK)
            rows = pl.ds(r0, CHUNK)
            dec = jnp.exp(bc_s[cur][pl.ds(r0 + tot_row, 1), :])
            for h in range(HEADS):
                ksl = slice(h * DK, (h + 1) * DK)
                vsl = slice(h * DV, (h + 1) * DV)
                st = st_ref[h]
                o_ref[rows, vsl] += _dot_nt(qd_s[cur][rows, ksl], st.astype(BF16))
                upd = _dot_tn(kvq_ref[rows, KEY_W + h * DV:KEY_W + (h + 1) * DV], ke_s[cur][rows, ksl])
                st_ref[h] = st * dec[:, ksl] + upd

        intra(0)
        la_hi, la_lo = log_gates(an_ref[...])
        for h in range(1, HEADS):
            intra(h)
        bcum = cum_decay(1 - cur, la_hi, la_lo)
        inter(0)
        decay_weight(1 - cur, bcum, kn_ref[...], qn_ref[...])
        for j in range(1, nchunk):
            inter(j)

    for parity in range(2):
        pl.when(s % 2 == parity)(functools.partial(scan_block, parity))


def _gla(kvq, a, wa_cat, ba_cat, tri):
    nk = 2 * KEY_W + VAL_W
    row_map = lambda b, d, s: (_gla_row_block(b, d, s), 0)
    nxt = lambda b, d, s: _gla_row_block(b, d, jnp.minimum(s + 1, GLA_STEPS - 1))
    return pl.pallas_call(
        _gla_body,
        grid=(NB, 2, GLA_STEPS),
        in_specs=[
            pl.BlockSpec((GLA_BLK, nk), row_map),
            pl.BlockSpec((GLA_BLK, 2 * LOWRANK), row_map),
            pl.BlockSpec((GLA_BLK, KEY_W), lambda b, d, s: (nxt(b, d, s), 0)),
            pl.BlockSpec((GLA_BLK, KEY_W), lambda b, d, s: (nxt(b, d, s), (KEY_W + VAL_W) // KEY_W)),
            pl.BlockSpec((GLA_BLK, 2 * LOWRANK), lambda b, d, s: (nxt(b, d, s), 0)),
            pl.BlockSpec((None, 6 * LOWRANK, KEY_W), lambda b, d, s: (d, 0, 0)),
            pl.BlockSpec((None, 1, KEY_W), lambda b, d, s: (d, 0, 0)),
            pl.BlockSpec((None, GLA_BLK, GLA_BLK), lambda b, d, s: (d, 0, 0)),
        ],
        out_specs=pl.BlockSpec((None, GLA_BLK, VAL_W), lambda b, d, s: (d, _gla_row_block(b, d, s), 0)),
        out_shape=jax.ShapeDtypeStruct((2, R_ALL, VAL_W), F32),
        scratch_shapes=[
            pltpu.VMEM((HEADS, DV, DK), F32),
        ] + [pltpu.VMEM((GLA_BLK, KEY_W), BF16)] * 6 + [pltpu.VMEM((GLA_BLK, KEY_W), F32)] * 2,
        compiler_params=_cparams(("arbitrary", "arbitrary", "arbitrary")),
        name="gla",
    )(kvq, a, kvq, kvq, a, wa_cat, ba_cat, tri)


def _mix_kernel(x_ref, m_ref, of_ref, ob_ref, r_ref, gt_ref, ng_ref, wgo_ref, who_ref, wo_ref, *rest, tm):
    zs, o_ref = rest[:-1], rest[-1]
    x = x_ref[...]
    gate = m_ref[...][:, 2 * D:]
    o = of_ref[...] + ob_ref[...]
    ng = ng_ref[...]
    parts = []
    for h in range(HEADS):
        sl = slice(h * DV, (h + 1) * DV)
        oh = o[:, sl]
        ms = jnp.mean(oh * oh, axis=-1, keepdims=True)
        parts.append(oh * lax.rsqrt(ms + EPS) * ng[:, sl])
    on = jnp.concatenate(parts, axis=-1)
    r = r_ref[...].astype(F32)
    y_gla = _dot((on * (r * _sigmoid(r))).astype(BF16), wgo_ref[...])
    y_hy = _dot(_row_tile(zs, tm), who_ref[...])
    gt = gt_ref[...].astype(F32)
    mix = _sigmoid(gt[:, :D]) * y_gla + _sigmoid(gt[:, D:]) * y_hy
    o_ref[...] = x + gate * _dot(mix.astype(BF16), wo_ref[...])


def _mix(x, mods, layer, o2, r, zs, gt, norm_g, wgo, who, wo, rows):
    tm = 512
    row = lambda w: pl.BlockSpec((tm, w), lambda t: (t, 0))
    return pl.pallas_call(
        functools.partial(_mix_kernel, tm=tm),
        grid=(rows // tm,),
        in_specs=[
            row(D),
            pl.BlockSpec((None, None, 1, 3 * D), lambda t: (layer, _mod_row(t, tm), 0, 1)),
            pl.BlockSpec((None, tm, VAL_W), lambda t: (0, t, 0)),
            pl.BlockSpec((None, tm, VAL_W), lambda t: (1, t, 0)),
            row(VAL_W),
            row(2 * D),
            _const_spec((1, VAL_W)),
            _layer_spec((VAL_W, D), layer),
            _layer_spec((HY_W, D), layer),
            _layer_spec((D, D), layer),
        ] + _row_specs(tm, HY_W, len(zs) == 2),
        out_specs=row(D),
        out_shape=jax.ShapeDtypeStruct((rows, D), F32),
        compiler_params=_cparams(("arbitrary",)),
        name="mix",
    )(x, mods, o2, o2, r, gt, norm_g.reshape(1, VAL_W), wgo, who, wo, *zs)


def _stack3_lhs(a):
    hi, lo = _split(a)
    return jnp.concatenate([hi, lo, hi], axis=1)


def _stack3_rhs(w):
    hi, lo = _split(w)
    return jnp.concatenate([hi, hi, lo], axis=0)


def _filt_mlp_kernel(z_ref, w1_ref, b1_ref, f1_ref, w2_ref, b2_ref, f2_ref, o_ref):
    h = jnp.sin(f1_ref[...] * (_dot(_stack3_lhs(z_ref[...]), w1_ref[...]) + b1_ref[...]))
    h = jnp.sin(f2_ref[...] * (_dot(_stack3_lhs(h), w2_ref[...]) + b2_ref[...]))
    o_ref[...] = _stack3_lhs(h)


def _filt_mlp(feats, w1, b1, f1, w2, b2, f2):
    rows = feats.shape[0]
    tm = 512
    vec = lambda: _const_spec((1, HY_HID))
    return pl.pallas_call(
        _filt_mlp_kernel,
        grid=(rows // tm,),
        in_specs=[pl.BlockSpec((tm, HY_HID), lambda t: (t, 0)), _const_spec((3 * HY_HID, HY_HID)), vec(), vec(),
                  _const_spec((3 * HY_HID, HY_HID)), vec(), vec()],
        out_specs=pl.BlockSpec((tm, 3 * HY_HID), lambda t: (t, 0)),
        out_shape=jax.ShapeDtypeStruct((rows, 3 * HY_HID), BF16),
        compiler_params=_cparams(("arbitrary",)),
        name="filt_mlp",
    )(feats, _stack3_rhs(w1), b1.reshape(1, HY_HID), f1.reshape(1, HY_HID), _stack3_rhs(w2),
      b2.reshape(1, HY_HID), f2.reshape(1, HY_HID))


def _taps_chunk(h2s, t, w3f, w3b, delta, fwd, zero):
    win = jnp.exp(-t * delta)
    h3 = _dot(h2s, jnp.concatenate([w3f, w3b], axis=1))
    return jnp.where(zero, 0.0, win * jnp.where(fwd, h3[:, :LANES], h3[:, LANES:]))


def _stage_b_fwd(buf_a, j, e_ref):
    cols = []
    for k1 in (2 * j, 2 * j + 1):
        re = buf_a[pl.ds(k1, N2, stride=PITCH_A), :]
        im = buf_a[pl.ds(N1 + k1, N2, stride=PITCH_A), :]
        cols.append(jnp.concatenate([re, im], axis=0).astype(BF16))
    return _dot(e_ref[...], jnp.concatenate(cols, axis=1))


def _spec_lat_kernel(h2_ref, rc_ref, w3f_ref, w3b_ref, dl_ref, wt_ref, e_ref, o_ref, taps, buf_a):
    tb = 512
    delta = dl_ref[...]

    ridx = lax.broadcasted_iota(jnp.int32, (tb, LANES), 0)
    fwd = (ridx & (N1 - 1)) < N1 // 2

    def fill(i, acc):
        rows = pl.ds(pl.multiple_of(i * tb, tb), tb)
        zero = ridx == jnp.where(i == 0, N1 // 2, -1)
        tp = _taps_chunk(h2_ref[rows, :], rc_ref[rows, 0:1], w3f_ref[...], w3b_ref[...], delta, fwd, zero)
        taps[rows, :] = tp
        return acc + jnp.sum(jnp.abs(tp), axis=0, keepdims=True)

    l1 = lax.fori_loop(0, FFT_N // tb, fill, jnp.zeros((1, LANES), F32))
    inv = 1.0 / (l1 + EPS)

    def stage_a(n2, c):
        rhs = taps[pl.ds(pl.multiple_of(n2 * N1, N1), N1), :]
        buf_a[pl.ds(pl.multiple_of(n2 * PITCH_A, 8), 2 * N1), :] = _dot(wt_ref[n2], rhs.astype(BF16))
        return c

    lax.fori_loop(0, N2, stage_a, 0, unroll=UNROLL)

    def stage_b(j, c):
        y = _stage_b_fwd(buf_a, j, e_ref)
        rows = 2 * N2
        o_ref[pl.ds(pl.multiple_of(2 * j * rows, 2 * rows), rows), :] = (y[:, :LANES] * inv).astype(o_ref.dtype)
        o_ref[pl.ds(pl.multiple_of(2 * j * rows + rows, rows), rows), :] = (y[:, LANES:] * inv).astype(o_ref.dtype)
        return c

    lax.fori_loop(0, N1 // 2, stage_b, 0, unroll=UNROLL)


def _spec_lat(h2p, rcp, w3, delta, cst):
    nct = HY_W // LANES
    return pl.pallas_call(
        _spec_lat_kernel,
        grid=(2, nct),
        in_specs=[
            _const_spec((FFT_N, 3 * HY_HID)),
            _const_spec((FFT_N, 8)),
            pl.BlockSpec((3 * HY_HID, LANES), lambda o, c: (0, o * nct + c)),
            pl.BlockSpec((3 * HY_HID, LANES), lambda o, c: (0, 2 * nct + o * nct + c)),
            pl.BlockSpec((1, LANES), lambda o, c: (0, c)),
            _const_spec((N2, 2 * N1, N1)),
            _const_spec((2 * N2, 2 * N2)),
        ],
        out_specs=pl.BlockSpec((None, None, 2 * FFT_N, LANES), lambda o, c: (o, c, 0, 0)),
        out_shape=jax.ShapeDtypeStruct((2, nct, 2 * FFT_N, LANES), BF16),
        scratch_shapes=[pltpu.VMEM((FFT_N, LANES), F32), pltpu.VMEM((N2 * PITCH_A, LANES), F32)],
        compiler_params=_cparams(("arbitrary", "arbitrary")),
        name="spec_lat",
    )(h2p, rcp, w3, w3, delta, cst["wt"], cst["e"])


def _hy_lat_kernel(ua_ref, ub_ref, h_ref, bias_ref, wa_ref, e_ref, g_ref, mi_ref,
                   o_ref, zin, buf_a, buf_d, upad):
    s = pl.program_id(2)

    def zslab(n2):
        return zin[pl.ds(pl.multiple_of(n2 * PITCH_Z, 8), 2 * N2), :]

    def repitch(n1, c):
        src = pl.ds(pl.multiple_of(n1 * GRID_W, GRID_W), GRID_W)
        dst = pl.ds(pl.multiple_of(n1 * PITCH_U, 8), GRID_W)
        upad[0, dst, :] = ua_ref[src, :].astype(F32)
        upad[1, dst, :] = ub_ref[src, :].astype(F32)
        return c
    lax.fori_loop(0, N2, repitch, 0, unroll=UNROLL)

    def col_slices(n2):
        return jnp.concatenate([upad[j, pl.ds(n2, N2, stride=PITCH_U), :] for j in range(2)], axis=0)

    @pl.when(s == 0)
    def _():
        def body(n2, c):
            zin[pl.ds(pl.multiple_of(n2 * PITCH_Z, 8), 2 * N2), :] = col_slices(n2)
            return c
        lax.fori_loop(0, N2, body, 0, unroll=UNROLL)

    @pl.when(s > 0)
    def _():
        def stage_a(n2, c):
            buf_a[pl.ds(pl.multiple_of(n2 * PITCH_A, 8), 2 * N1), :] = _dot(wa_ref[n2], zslab(n2).astype(BF16))
            return c
        lax.fori_loop(0, N2, stage_a, 0, unroll=UNROLL)

        def stage_b(j, c):
            y = _stage_b_fwd(buf_a, j, e_ref)
            prods = []
            for i in range(2):
                yk = y[:, i * LANES:(i + 1) * LANES]
                hb = h_ref[pl.ds(pl.multiple_of((2 * j + i) * 2 * N2, 2 * N2), 2 * N2), :].astype(F32)
                yr, yi = yk[:N2], yk[N2:]
                hr, hi = hb[:N2], hb[N2:]
                prods.append(jnp.concatenate([yr * hr - yi * hi, yr * hi + yi * hr], axis=0).astype(BF16))
            d = _dot(g_ref[...], jnp.concatenate(prods, axis=1))
            for i in range(2):
                rows = pl.ds(pl.multiple_of((2 * j + i) * PITCH_D, 8), 2 * N2)
                buf_d[rows, :] = d[:, i * LANES:(i + 1) * LANES]
            return c
        lax.fori_loop(0, N1 // 2, stage_b, 0, unroll=2 * UNROLL)

        def stage_ai(n2, c):
            re = buf_d[pl.ds(n2, N1, stride=PITCH_D), :]
            im = buf_d[pl.ds(N2 + n2, N1, stride=PITCH_D), :]
            x = jnp.concatenate([re, im], axis=0).astype(BF16)
            y = _dot(mi_ref[n2], x)
            rows = pl.ds(pl.multiple_of(n2 * PITCH_Z, 8), 2 * N2)
            zin[rows, :] = col_slices(n2) * (y + bias_ref[...] * zin[rows, :])
            return c
        lax.fori_loop(0, N2, stage_ai, 0, unroll=UNROLL)

    @pl.when(s == 2)
    def _():
        def emit(n1, c):
            dst = pl.ds(pl.multiple_of(n1 * GRID_W, GRID_W), GRID_W)
            o_ref[0, dst, :] = zin[pl.ds(n1, GRID_W, stride=PITCH_Z), :].astype(o_ref.dtype)
            o_ref[1, dst, :] = zin[pl.ds(N2 + n1, GRID_W, stride=PITCH_Z), :].astype(o_ref.dtype)
            return c
        lax.fori_loop(0, N2, emit, 0, unroll=UNROLL)


def _hy_lat(hy, hspec, bias, cst):
    nct = HY_W // LANES
    part = lambda c, p, s: s * nct + c
    order = lambda s: jnp.maximum(s - 1, 0)
    consts = [cst[n] for n in ("wa", "e", "g", "mi")]
    return pl.pallas_call(
        _hy_lat_kernel,
        grid=(nct, NB // 2, 3),
        in_specs=[
            pl.BlockSpec((SEQ, LANES), lambda c, p, s: (2 * p, part(c, p, s))),
            pl.BlockSpec((SEQ, LANES), lambda c, p, s: (2 * p + 1, part(c, p, s))),
            pl.BlockSpec((None, None, 2 * FFT_N, LANES), lambda c, p, s: (order(s), c, 0, 0)),
            pl.BlockSpec((None, 1, LANES), lambda c, p, s: (order(s), 0, c)),
        ] + [_const_spec(x.shape) for x in consts],
        out_specs=pl.BlockSpec((2, SEQ, LANES), lambda c, p, s: (p, 0, c)),
        out_shape=jax.ShapeDtypeStruct((NB, SEQ, HY_W), BF16),
        scratch_shapes=[
            pltpu.VMEM((N2 * PITCH_Z, LANES), F32),
            pltpu.VMEM((N2 * PITCH_A, LANES), F32),
            pltpu.VMEM((N1 * PITCH_D, LANES), F32),
            pltpu.VMEM((2, N2 * PITCH_U, LANES), F32),
        ],
        compiler_params=_cparams(("arbitrary", "arbitrary", "arbitrary")),
        name="hy_lat",
    )(hy, hy, hspec, bias, *consts)


def _spec_ctx_kernel(h2_ref, rc_ref, w3f_ref, w3b_ref, dl_ref, wt_ref, o_ref):
    ridx = lax.broadcasted_iota(jnp.int32, (CTX_N, LANES), 0)
    tp = _taps_chunk(h2_ref[...], rc_ref[:, 0:1], w3f_ref[...], w3b_ref[...], dl_ref[...], ridx < CTX, ridx == CTX)
    l1 = jnp.sum(jnp.abs(tp), axis=0, keepdims=True)
    o_ref[...] = _dot(wt_ref[...], (tp * (1.0 / (l1 + EPS))).astype(BF16))


def _spec_ctx(h2, rc, w3, delta, cst):
    nct = HY_W // LANES
    return pl.pallas_call(
        _spec_ctx_kernel,
        grid=(2, nct),
        in_specs=[
            _const_spec((CTX_N, 3 * HY_HID)),
            _const_spec((CTX_N, 8)),
            pl.BlockSpec((3 * HY_HID, LANES), lambda o, c: (0, o * nct + c)),
            pl.BlockSpec((3 * HY_HID, LANES), lambda o, c: (0, 2 * nct + o * nct + c)),
            pl.BlockSpec((1, LANES), lambda o, c: (0, c)),
            _const_spec((2 * CTX_N, CTX_N)),
        ],
        out_specs=pl.BlockSpec((None, None, 2 * CTX_N, LANES), lambda o, c: (o, c, 0, 0)),
        out_shape=jax.ShapeDtypeStruct((2, nct, 2 * CTX_N, LANES), F32),
        compiler_params=_cparams(("arbitrary", "arbitrary")),
        name="spec_ctx",
    )(h2, rc, w3, w3, delta, cst["cwt"])


def _hy_ctx_kernel(va, vb, x1a, x1b, x2a, x2b, h_ref, bias_ref, wf_ref, wi_ref, o_ref):
    def pair(ra, rb):
        return jnp.concatenate([ra[...], rb[...]], axis=0).astype(F32)

    z = pair(va, vb)
    gates = (pair(x1a, x1b), pair(x2a, x2b))
    for o in range(2):
        y = _dot(wf_ref[...], z.astype(BF16))
        hb = h_ref[o]
        yr, yi = y[:CTX_N], y[CTX_N:]
        hr, hi = hb[:CTX_N], hb[CTX_N:]
        prod = jnp.concatenate([yr * hr - yi * hi, yr * hi + yi * hr], axis=0)
        conv = _dot(wi_ref[...], prod.astype(BF16))
        z = gates[o] * (conv + bias_ref[o] * z)
    o_ref[0] = z[:CTX].astype(o_ref.dtype)
    o_ref[1] = z[CTX:].astype(o_ref.dtype)


def _hy_ctx(hy, hspec, bias, cst):
    nct = HY_W // LANES
    base = R_LAT // CTX

    def blk(j, part):
        return pl.BlockSpec((CTX, LANES), lambda c, p: (base + 2 * p + j, part * nct + c))

    consts = [cst["cwf"], cst["cwi"]]
    return pl.pallas_call(
        _hy_ctx_kernel,
        grid=(nct, NB // 2),
        in_specs=[blk(0, 0), blk(1, 0), blk(0, 1), blk(1, 1), blk(0, 2), blk(1, 2),
                  pl.BlockSpec((2, None, 2 * CTX_N, LANES), lambda c, p: (0, c, 0, 0)),
                  pl.BlockSpec((2, 1, LANES), lambda c, p: (0, 0, c))]
        + [_const_spec(x.shape) for x in consts],
        out_specs=pl.BlockSpec((2, CTX, LANES), lambda c, p: (p, 0, c)),
        out_shape=jax.ShapeDtypeStruct((NB, CTX, HY_W), BF16),
        compiler_params=_cparams(("arbitrary", "arbitrary")),
        name="hy_ctx",
    )(hy, hy, hy, hy, hy, hy, hspec, bias, *consts)


def _stack(c):
    return np.block([[c.real, -c.imag], [c.imag, c.real]])


def _fft_constants():
    cst = {}
    k1 = np.arange(N1)[:, None]
    ca = np.exp(-2j * np.pi * k1 * np.arange(N1 // 2)[None, :] / N1)
    tw = np.exp(-2j * np.pi * k1 * np.arange(N2)[None, :] / FFT_N)
    ct = np.exp(-2j * np.pi * k1 * np.arange(N1)[None, :] / N1)
    cst["wa"] = jnp.asarray(np.stack([_stack(tw[:, n2:n2 + 1] * ca) for n2 in range(N2)]), F32).astype(BF16)
    wt = [tw[:, n2:n2 + 1] * ct for n2 in range(N2)]
    cst["wt"] = jnp.asarray(np.stack([np.concatenate([m.real, m.imag], axis=0) for m in wt]), F32).astype(BF16)
    e = np.exp(-2j * np.pi * np.arange(N2)[:, None] * np.arange(N2)[None, :] / N2)
    cst["e"] = jnp.asarray(_stack(e), F32).astype(BF16)
    cst["g"] = jnp.asarray(_stack(np.conj(e).T), F32).astype(BF16)
    bi = np.exp(2j * np.pi * np.arange(N1 // 2)[:, None] * np.arange(N1)[None, :] / N1) / FFT_N
    cst["mi"] = jnp.asarray(np.stack([_stack(bi * np.conj(tw[:, n2])[None, :]) for n2 in range(N2)]),
                            F32).astype(BF16)
    kk = np.arange(CTX_N)[:, None]
    cf = np.exp(-2j * np.pi * kk * np.arange(CTX)[None, :] / CTX_N)
    cst["cwf"] = jnp.asarray(_stack(cf), F32).astype(BF16)
    cfull = np.exp(-2j * np.pi * kk * np.arange(CTX_N)[None, :] / CTX_N)
    cst["cwt"] = jnp.asarray(np.concatenate([cfull.real, cfull.imag], axis=0), F32).astype(BF16)
    ci = np.exp(2j * np.pi * np.arange(CTX)[:, None] * np.arange(CTX_N)[None, :] / CTX_N) / CTX_N
    cst["cwi"] = jnp.asarray(_stack(ci), F32).astype(BF16)
    return cst


def _filter_positions(L, permute):
    bands = (HY_EMB - 1) // 2
    t = np.linspace(0.0, 1.0, L)[:, None]
    w = (2.0 * math.pi / L) * np.arange(L)[:, None]
    f = np.linspace(1e-4, bands - 1, bands)[None, :]
    z = np.concatenate([t, np.cos(f * w), -np.sin(f * w)], axis=-1)
    n = np.arange(2 * L)
    pos = np.where(n < L, n, 2 * L - n)
    pos = np.where(n == L, 0, pos)
    fsel = (n < L).astype(np.float32)
    bsel = (n > L).astype(np.float32)
    if permute:
        order = (np.arange(N1)[None, :] * N2 + np.arange(N2)[:, None]).reshape(-1)
        pos, fsel, bsel = pos[order], fsel[order], bsel[order]
    feats = np.pad(z[pos], ((0, 0), (0, HY_HID - HY_EMB)))
    rc = np.concatenate([t[pos], fsel[:, None], bsel[:, None], np.zeros((2 * L, 5))], axis=1)
    return jnp.asarray(feats, F32), jnp.asarray(rc, F32)


def _decay_rates():
    max_decay = math.log(HY_TARGET) / HY_FAST
    min_decay = math.log(HY_TARGET) / HY_SLOW
    return jnp.asarray(np.abs(np.linspace(min_decay, max_decay, HY_W))[None, :], F32)


def _gla_constants():
    r = np.arange(GLA_BLK)
    same = (r[:, None] // CHUNK) == (r[None, :] // CHUNK)
    fwd = same & (r[:, None] >= r[None, :])
    bwd = same & (r[:, None] <= r[None, :])
    return jnp.asarray(np.stack([fwd, bwd]).astype(np.float32)).astype(BF16)


def kernel(x, c, ctx, c_ctx, ada_w, ada_b, ffn1_norm_g, ffn1_w_gate, ffn1_w_up, ffn1_w_down, mix_norm_g, w_in,
           gla_wa_f, gla_ba_f, gla_wa_b, gla_ba_b, gla_norm_g, w_gla_out, hy_conv_w, hy_conv_b, hy_f_w1, hy_f_b1,
           hy_f_freq1, hy_f_w2, hy_f_b2, hy_f_freq2, hy_f_w3, hy_bias, w_hy_out, w_out, ffn2_norm_g, ffn2_w_gate,
           ffn2_w_up, ffn2_w_down, final_norm_g):
    cst = _fft_constants()
    tri = _gla_constants()
    delta = _decay_rates()
    feats_lat, rc_lat = _filter_positions(SEQ, True)
    feats_ctx, rc_ctx = _filter_positions(CTX, False)

    cvec = jnp.concatenate([c, c_ctx[None, :], jnp.zeros((3, D), F32)], axis=0)
    mods = _mods(cvec, ada_w, ada_b).reshape(DEPTH, 8, 1, N_MOD * D)
    xs = (x.reshape(R_LAT, D), ctx.reshape(R_CTX, D))

    bf = lambda w: w.astype(BF16)
    ffn1_w = (bf(ffn1_w_gate), bf(ffn1_w_up), bf(ffn1_w_down))
    ffn2_w = (bf(ffn2_w_gate), bf(ffn2_w_up), bf(ffn2_w_down))
    out_w = (bf(w_gla_out), bf(w_hy_out), bf(w_out))
    w_t = jnp.swapaxes(w_in, 1, 2)
    wp = bf(jnp.concatenate([w_t[:, OFF_K:OFF_A], w_t[:, OFF_Q:OFF_R], w_t[:, OFF_R:IN_W]], axis=1))
    wa_hi, wa_lo = _split(w_t[:, OFF_A:OFF_Q])
    conv_b = hy_conv_b[:, None, :]

    zpad = jnp.zeros((LOWRANK, KEY_W), F32)
    for i in range(DEPTH):
        last = i == DEPTH - 1
        xa = _ffn(xs, mods, i, 0, ffn1_norm_g[i], *ffn1_w, R_ALL)
        kvq, a, r, hy, gt = _proj(xa, mods, i, mix_norm_g[i], wp, wa_hi, wa_lo, hy_conv_w, conv_b)

        wa_cat = jnp.stack([jnp.concatenate([gla_wa_f[i], zpad], axis=0),
                            jnp.concatenate([zpad, gla_wa_b[i]], axis=0)])
        w_hi, w_lo = _split(wa_cat)
        ba_cat = jnp.stack([gla_ba_f[i], gla_ba_b[i]])[:, None, :]
        o2 = _gla(kvq, a, jnp.concatenate([w_hi, w_hi, w_lo], axis=1), ba_cat, tri)

        w1p = jnp.pad(hy_f_w1[i], ((0, HY_HID - HY_EMB), (0, 0)))
        mlp = (w1p, hy_f_b1[i], hy_f_freq1[i], hy_f_w2[i], hy_f_b2[i], hy_f_freq2[i])
        bias = hy_bias[i][:, None, :]
        w3s = _stack3_rhs(hy_f_w3[i])
        spec_lat = _spec_lat(_filt_mlp(feats_lat, *mlp), rc_lat, w3s, delta, cst)
        zs = (_hy_lat(hy, spec_lat, bias, cst).reshape(R_LAT, HY_W),)

        rows = R_LAT if last else R_ALL
        if not last:
            spec_ctx = _spec_ctx(_filt_mlp(feats_ctx, *mlp), rc_ctx, w3s, delta, cst)
            zs += (_hy_ctx(hy, spec_ctx, bias, cst).reshape(R_CTX, HY_W),)
        xa = _mix(xa, mods, i, o2, r, zs, gt, gla_norm_g[i], *out_w, rows)
        xs = (_ffn((xa,), mods, i, 2, ffn2_norm_g[i], *ffn2_w, rows, final_g=final_norm_g if last else None),)
    return xs[0].reshape(NB, SEQ, D)
```

```python
import functools
import math

import numpy as np
import jax
import jax.numpy as jnp
from jax import lax
from jax.experimental import pallas as pl
from jax.experimental.pallas import tpu as pltpu

F32 = jnp.float32
BF16 = jnp.bfloat16

D = 1024
NB = 4
SEQ = 4096
DEPTH = 2
CTX = 256
GRID_W = 64
D_FF = 2816
HALF_STEP = 0.5
N_MOD = 9
EPS = 1e-6

HEADS = 4
DK = 128
DV = 256
LOWRANK = 16
TAU = 16.0
CHUNK = 64
KEY_W = HEADS * DK
VAL_W = HEADS * DV

HY_W = 1024
HY_EMB = 33
HY_HID = 64
HY_FAST = 0.3
HY_SLOW = 1.5
HY_TARGET = 1e-2

OFF_K = 0
OFF_V = OFF_K + KEY_W
OFF_A = OFF_V + VAL_W
OFF_Q = OFF_A + 2 * LOWRANK
OFF_R = OFF_Q + KEY_W
OFF_H = OFF_R + VAL_W
OFF_G = OFF_H + 3 * HY_W
IN_W = OFF_G + 2 * D

R_LAT = NB * SEQ
R_CTX = NB * CTX
R_ALL = R_LAT + R_CTX

LANES = 128
VMEM_LIMIT = 56 * 1024 * 1024

FFT_N = 2 * SEQ
N1 = 128
N2 = 64
PITCH_A = 264
PITCH_D = 136
PITCH_U = 72
PITCH_Z = 136
UNROLL = 8
CTX_N = 2 * CTX


def _cparams(sem):
    return pltpu.CompilerParams(dimension_semantics=sem, vmem_limit_bytes=VMEM_LIMIT)


def _const_spec(shape):
    nd = len(shape)
    return pl.BlockSpec(shape, lambda *_: (0,) * nd, pipeline_mode=pl.Buffered(1))


def _layer_spec(shape, layer):
    nd = len(shape)
    return pl.BlockSpec((None,) + tuple(shape), lambda *_: (layer,) + (0,) * nd, pipeline_mode=pl.Buffered(1))


def _split(x):
    hi = x.astype(BF16)
    lo = (x - hi.astype(F32)).astype(BF16)
    return hi, lo


def _dot(a, b):
    return jnp.dot(a, b, preferred_element_type=F32)


def _sigmoid(x):
    return 1.0 / (1.0 + jnp.exp(-x))


def _rms(x, g):
    ms = jnp.mean(x * x, axis=-1, keepdims=True)
    return x * lax.rsqrt(ms + EPS) * g


def _mods_kernel(c_ref, w_ref, b_ref, o_ref):
    c = c_ref[...]
    s = (c * _sigmoid(c)).astype(BF16)
    o_ref[...] = _dot(s, w_ref[...].astype(BF16)) + b_ref[...]


def _mods(cvec, ada_w, ada_b):
    tn = 1024
    return pl.pallas_call(
        _mods_kernel,
        grid=(DEPTH, N_MOD * D // tn),
        in_specs=[
            pl.BlockSpec((8, D), lambda l, j: (0, 0)),
            pl.BlockSpec((None, D, tn), lambda l, j: (l, 0, j)),
            pl.BlockSpec((None, 1, tn), lambda l, j: (l, 0, j)),
        ],
        out_specs=pl.BlockSpec((None, 8, tn), lambda l, j: (l, 0, j)),
        out_shape=jax.ShapeDtypeStruct((DEPTH, 8, N_MOD * D), F32),
        compiler_params=_cparams(("arbitrary", "arbitrary")),
        name="mods",
    )(cvec, ada_w, ada_b.reshape(DEPTH, 1, N_MOD * D))


def _mod_row(t, tm):
    lat_tiles = R_LAT // tm
    per_batch = SEQ // tm
    return jnp.where(t < lat_tiles, t // per_batch, NB)


def _row_specs(tm, width, split):
    if not split:
        return [pl.BlockSpec((tm, width), lambda t: (t, 0))]
    lat_tiles = R_LAT // tm
    return [pl.BlockSpec((tm, width), lambda t: (jnp.minimum(t, lat_tiles - 1), 0)),
            pl.BlockSpec((tm, width), lambda t: (jnp.maximum(t - lat_tiles, 0), 0))]


def _row_tile(refs, tm):
    if len(refs) == 1:
        return refs[0][...]
    return jnp.where(pl.program_id(0) < R_LAT // tm, refs[0][...], refs[1][...])


def _ffn_kernel(*refs, final, nx, tm):
    xs, (m_ref, g_ref, wg_ref, wu_ref, wd_ref), rest = refs[:nx], refs[nx:nx + 5], refs[nx + 5:]
    if final:
        fg_ref, o_ref = rest
    else:
        (o_ref,) = rest
    x = _row_tile(xs, tm)
    m = m_ref[...]
    shift, scale, gate = m[:, :D], m[:, D:2 * D], m[:, 2 * D:]
    h = (_rms(x, g_ref[...]) * (1.0 + scale) + shift).astype(BF16)
    a = _dot(h, wg_ref[...])
    u = _dot(h, wu_ref[...])
    act = (a * _sigmoid(a) * u).astype(BF16)
    y = _dot(act, wd_ref[...])
    out = x + (HALF_STEP * gate) * y
    if final:
        out = _rms(out, fg_ref[...])
    o_ref[...] = out


def _ffn(xs, mods, layer, slot, norm_g, wg, wu, wd, rows, final_g=None):
    tm = 512
    final = final_g is not None
    in_specs = _row_specs(tm, D, len(xs) == 2) + [
        pl.BlockSpec((None, None, 1, 3 * D), lambda t: (layer, _mod_row(t, tm), 0, slot)),
        _const_spec((1, D)),
        _layer_spec((D, D_FF), layer),
        _layer_spec((D, D_FF), layer),
        _layer_spec((D_FF, D), layer),
    ]
    args = list(xs) + [mods, norm_g.reshape(1, D), wg, wu, wd]
    if final:
        in_specs.append(_const_spec((1, D)))
        args.append(final_g.reshape(1, D))
    return pl.pallas_call(
        functools.partial(_ffn_kernel, final=final, nx=len(xs), tm=tm),
        grid=(rows // tm,),
        in_specs=in_specs,
        out_specs=pl.BlockSpec((tm, D), lambda t: (t, 0)),
        out_shape=jax.ShapeDtypeStruct((rows, D), F32),
        compiler_params=_cparams(("arbitrary",)),
        name="ffn",
    )(*args)


def _proj_kernel(x_ref, m_ref, g_ref, wkv_ref, wq_ref, wahi_ref, walo_ref, wr_ref, wh_ref, wg_ref, cw_ref, cb_ref,
                 kvq_ref, a_ref, r_ref, hy_ref, gt_ref, *, tm):
    x = x_ref[...]
    m = m_ref[...]
    shift, scale = m[:, :D], m[:, D:2 * D]
    h = _rms(x, g_ref[...]) * (1.0 + scale) + shift
    hb = h.astype(BF16)
    period = jnp.where(pl.program_id(0) < R_LAT // tm, GRID_W, CTX)
    pos = lax.broadcasted_iota(jnp.int32, (tm, 1), 0) & (period - 1)
    first, last = pos == 0, pos == period - 1

    def hyena_part(part):
        cols = slice(part * HY_W, (part + 1) * HY_W)
        u = _dot_nt(hb, wh_ref[0, cols, :])
        prv = jnp.where(first, 0.0, pltpu.roll(u, 1, 0))
        nxt = jnp.where(last, 0.0, pltpu.roll(u, tm - 1, 0))
        conv = cw_ref[0:1, cols] * prv + cw_ref[1:2, cols] * u + cw_ref[2:3, cols] * nxt + cb_ref[:, cols]
        hy_ref[:, cols] = conv.astype(hy_ref.dtype)

    hyena_part(0)
    kvq_ref[:, :KEY_W + VAL_W] = _dot_nt(hb, wkv_ref[0]).astype(kvq_ref.dtype)
    kvq_ref[:, KEY_W + VAL_W:] = _dot_nt(hb, wq_ref[0]).astype(kvq_ref.dtype)
    hyena_part(1)
    gt_ref[...] = _dot_nt(hb, wg_ref[0]).astype(gt_ref.dtype)
    hyena_part(2)
    r_ref[...] = _dot_nt(hb, wr_ref[0]).astype(r_ref.dtype)
    hl = (h - hb.astype(F32)).astype(BF16)
    a_ref[...] = _dot_nt(hb, wahi_ref[...]) + _dot_nt(hl, wahi_ref[...]) + _dot_nt(hb, walo_ref[...])


def _proj(x, mods, layer, norm_g, wp, wa_hi, wa_lo, conv_w, conv_b):
    tm = 512
    nk = 2 * KEY_W + VAL_W
    na = 2 * LOWRANK
    row = lambda w: pl.BlockSpec((tm, w), lambda t: (t, 0))

    def wrows(start, stop):
        return pl.BlockSpec((pl.Element(1), pl.Element(stop - start), pl.Element(D)),
                            lambda t: (layer, start, 0), pipeline_mode=pl.Buffered(1))

    def wcols(width, blk):
        return pl.BlockSpec((None, width, D), lambda t: (layer, blk, 0), pipeline_mode=pl.Buffered(1))

    return pl.pallas_call(
        functools.partial(_proj_kernel, tm=tm),
        grid=(R_ALL // tm,),
        in_specs=[
            row(D),
            pl.BlockSpec((None, None, 1, 3 * D), lambda t: (layer, _mod_row(t, tm), 0, 1)),
            _const_spec((1, D)),
            wrows(OFF_K, OFF_A),
            wrows(OFF_Q, OFF_R),
            wcols(na, 0),
            wcols(na, 0),
            wrows(OFF_R, OFF_H),
            wrows(OFF_H, OFF_G),
            wrows(OFF_G, IN_W),
            pl.BlockSpec((None, 3, 3 * HY_W), lambda t: (layer, 0, 0)),
            pl.BlockSpec((None, 1, 3 * HY_W), lambda t: (layer, 0, 0)),
        ],
        out_specs=[row(nk), row(na), row(VAL_W), row(3 * HY_W), row(2 * D)],
        out_shape=[
            jax.ShapeDtypeStruct((R_ALL, nk), BF16),
            jax.ShapeDtypeStruct((R_ALL, na), F32),
            jax.ShapeDtypeStruct((R_ALL, VAL_W), BF16),
            jax.ShapeDtypeStruct((R_ALL, 3 * HY_W), BF16),
            jax.ShapeDtypeStruct((R_ALL, 2 * D), BF16),
        ],
        compiler_params=_cparams(("arbitrary",)),
        name="proj",
    )(x, mods, norm_g.reshape(1, D), wp, wp, wa_hi, wa_lo, wp, wp, wp, conv_w, conv_b)


GLA_BLK = 256
GLA_STEPS = 1 + SEQ // GLA_BLK


def _gla_row_block(b, d, s):
    lat = jnp.where(d == 0, s - 1, SEQ // GLA_BLK - s)
    return jnp.where(s == 0, R_LAT // GLA_BLK + b, b * (SEQ // GLA_BLK) + lat)


def _dot_nt(a, b):
    return lax.dot_general(a, b, (((1,), (1,)), ((), ())), preferred_element_type=F32)


def _dot_tn(a, b):
    return lax.dot_general(a, b, (((0,), (0,)), ((), ())), preferred_element_type=F32)


def _gla_body(kvq_ref, a_ref, kn_ref, qn_ref, an_ref, wa_ref, ba_ref, tri_ref, o_ref,
              st_ref, qd0, qd1, kn0, kn1, ke0, ke1, bc0, bc1):
    qd_s, kn_s, ke_s, bc_s = (qd0, qd1), (kn0, kn1), (ke0, ke1), (bc0, bc1)
    d = pl.program_id(1)
    s = pl.program_id(2)
    nchunk = GLA_BLK // CHUNK
    tot_row = jnp.where(d == 0, CHUNK - 1, 0)

    def log_gates(a_blk):
        a_hi, a_lo = _split(a_blk)
        pre = _dot(jnp.concatenate([a_hi, a_lo, a_hi], axis=1), wa_ref[...]) + ba_ref[...]
        la = (jnp.minimum(pre, 0.0) - jnp.log(1.0 + jnp.exp(-jnp.abs(pre)))) * (1.0 / TAU)
        return _split(la)

    def cum_decay(slot, la_hi, la_lo):
        tri = tri_ref[...]
        bcum = _dot(tri, la_hi) + _dot(tri, la_lo)
        bc_s[slot][...] = bcum
        return bcum

    def decay_weight(slot, bcum, k_blk, q_blk):
        btot = jnp.concatenate(
            [jnp.broadcast_to(bc_s[slot][pl.ds(j * CHUNK + tot_row, 1), :], (CHUNK, KEY_W)) for j in range(nchunk)],
            axis=0)
        k = k_blk.astype(F32)
        q = q_blk.astype(F32)
        qd_s[slot][...] = (q * (DK ** -0.5) * jnp.exp(bcum)).astype(BF16)
        kn_s[slot][...] = (k * jnp.exp(-bcum)).astype(BF16)
        ke_s[slot][...] = (k * jnp.exp(btot - bcum)).astype(BF16)

    @pl.when(s == 0)
    def _():
        st_ref[...] = jnp.zeros_like(st_ref)
        decay_weight(0, cum_decay(0, *log_gates(a_ref[...])), kvq_ref[:, :KEY_W], kvq_ref[:, KEY_W + VAL_W:])

    def scan_block(cur):
        seen = tri_ref[...].astype(F32) > 0.5

        def intra(h):
            ksl = slice(h * DK, (h + 1) * DK)
            att = jnp.where(seen, _dot_nt(qd_s[cur][:, ksl], kn_s[cur][:, ksl]), 0.0).astype(BF16)
            o_ref[:, h * DV:(h + 1) * DV] = _dot(att, kvq_ref[:, KEY_W + h * DV:KEY_W + (h + 1) * DV])

        def inter(j):
            r0 = pl.multiple_of(jnp.where(d == 0, j, nchunk - 1 - j) * CHUNK, CHUNK)
            rows = pl.ds(r0, CHUNK)
            dec = jnp.exp(bc_s[cur][pl.ds(r0 + tot_row, 1), :])
            for h in range(HEADS):
                ksl = slice(h * DK, (h + 1) * DK)
                vsl = slice(h * DV, (h + 1) * DV)
                st = st_ref[h]
                o_ref[rows, vsl] += _dot_nt(qd_s[cur][rows, ksl], st.astype(BF16))
                upd = _dot_tn(kvq_ref[rows, KEY_W + h * DV:KEY_W + (h + 1) * DV], ke_s[cur][rows, ksl])
                st_ref[h] = st * dec[:, ksl] + upd

        intra(0)
        la_hi, la_lo = log_gates(an_ref[...])
        for h in range(1, HEADS):
            intra(h)
        bcum = cum_decay(1 - cur, la_hi, la_lo)
        inter(0)
        decay_weight(1 - cur, bcum, kn_ref[...], qn_ref[...])
        for j in range(1, nchunk):
            inter(j)

    for parity in range(2):
        pl.when(s % 2 == parity)(functools.partial(scan_block, parity))


def _gla(kvq, a, wa_cat, ba_cat, tri):
    nk = 2 * KEY_W + VAL_W
    row_map = lambda b, d, s: (_gla_row_block(b, d, s), 0)
    nxt = lambda b, d, s: _gla_row_block(b, d, jnp.minimum(s + 1, GLA_STEPS - 1))
    return pl.pallas_call(
        _gla_body,
        grid=(NB, 2, GLA_STEPS),
        in_specs=[
            pl.BlockSpec((GLA_BLK, nk), row_map),
            pl.BlockSpec((GLA_BLK, 2 * LOWRANK), row_map),
            pl.BlockSpec((GLA_BLK, KEY_W), lambda b, d, s: (nxt(b, d, s), 0)),
            pl.BlockSpec((GLA_BLK, KEY_W), lambda b, d, s: (nxt(b, d, s), (KEY_W + VAL_W) // KEY_W)),
            pl.BlockSpec((GLA_BLK, 2 * LOWRANK), lambda b, d, s: (nxt(b, d, s), 0)),
            pl.BlockSpec((None, 6 * LOWRANK, KEY_W), lambda b, d, s: (d, 0, 0)),
            pl.BlockSpec((None, 1, KEY_W), lambda b, d, s: (d, 0, 0)),
            pl.BlockSpec((None, GLA_BLK, GLA_BLK), lambda b, d, s: (d, 0, 0)),
        ],
        out_specs=pl.BlockSpec((None, GLA_BLK, VAL_W), lambda b, d, s: (d, _gla_row_block(b, d, s), 0)),
        out_shape=jax.ShapeDtypeStruct((2, R_ALL, VAL_W), F32),
        scratch_shapes=[
            pltpu.VMEM((HEADS, DV, DK), F32),
        ] + [pltpu.VMEM((GLA_BLK, KEY_W), BF16)] * 6 + [pltpu.VMEM((GLA_BLK, KEY_W), F32)] * 2,
        compiler_params=_cparams(("arbitrary", "arbitrary", "arbitrary")),
        name="gla",
    )(kvq, a, kvq, kvq, a, wa_cat, ba_cat, tri)


def _mix_kernel(x_ref, m_ref, of_ref, ob_ref, r_ref, gt_ref, ng_ref, wgo_ref, who_ref, wo_ref, *rest, tm):
    zs, o_ref = rest[:-1], rest[-1]
    x = x_ref[...]
    gate = m_ref[...][:, 2 * D:]
    o = of_ref[...] + ob_ref[...]
    ng = ng_ref[...]
    parts = []
    for h in range(HEADS):
        sl = slice(h * DV, (h + 1) * DV)
        oh = o[:, sl]
        ms = jnp.mean(oh * oh, axis=-1, keepdims=True)
        parts.append(oh * lax.rsqrt(ms + EPS) * ng[:, sl])
    on = jnp.concatenate(parts, axis=-1)
    r = r_ref[...].astype(F32)
    y_gla = _dot((on * (r * _sigmoid(r))).astype(BF16), wgo_ref[...])
    y_hy = _dot(_row_tile(zs, tm), who_ref[...])
    gt = gt_ref[...].astype(F32)
    mix = _sigmoid(gt[:, :D]) * y_gla + _sigmoid(gt[:, D:]) * y_hy
    o_ref[...] = x + gate * _dot(mix.astype(BF16), wo_ref[...])


def _mix(x, mods, layer, o2, r, zs, gt, norm_g, wgo, who, wo, rows):
    tm = 512
    row = lambda w: pl.BlockSpec((tm, w), lambda t: (t, 0))
    return pl.pallas_call(
        functools.partial(_mix_kernel, tm=tm),
        grid=(rows // tm,),
        in_specs=[
            row(D),
            pl.BlockSpec((None, None, 1, 3 * D), lambda t: (layer, _mod_row(t, tm), 0, 1)),
            pl.BlockSpec((None, tm, VAL_W), lambda t: (0, t, 0)),
            pl.BlockSpec((None, tm, VAL_W), lambda t: (1, t, 0)),
            row(VAL_W),
            row(2 * D),
            _const_spec((1, VAL_W)),
            _layer_spec((VAL_W, D), layer),
            _layer_spec((HY_W, D), layer),
            _layer_spec((D, D), layer),
        ] + _row_specs(tm, HY_W, len(zs) == 2),
        out_specs=row(D),
        out_shape=jax.ShapeDtypeStruct((rows, D), F32),
        compiler_params=_cparams(("arbitrary",)),
        name="mix",
    )(x, mods, o2, o2, r, gt, norm_g.reshape(1, VAL_W), wgo, who, wo, *zs)


def _stack3_lhs(a):
    hi, lo = _split(a)
    return jnp.concatenate([hi, lo, hi], axis=1)


def _stack3_rhs(w):
    hi, lo = _split(w)
    return jnp.concatenate([hi, hi, lo], axis=0)


def _filt_mlp_kernel(z_ref, w1_ref, b1_ref, f1_ref, w2_ref, b2_ref, f2_ref, o_ref):
    h = jnp.sin(f1_ref[...] * (_dot(_stack3_lhs(z_ref[...]), w1_ref[...]) + b1_ref[...]))
    h = jnp.sin(f2_ref[...] * (_dot(_stack3_lhs(h), w2_ref[...]) + b2_ref[...]))
    o_ref[...] = _stack3_lhs(h)


def _filt_mlp(feats, w1, b1, f1, w2, b2, f2):
    rows = feats.shape[0]
    tm = 512
    vec = lambda: _const_spec((1, HY_HID))
    return pl.pallas_call(
        _filt_mlp_kernel,
        grid=(rows // tm,),
        in_specs=[pl.BlockSpec((tm, HY_HID), lambda t: (t, 0)), _const_spec((3 * HY_HID, HY_HID)), vec(), vec(),
                  _const_spec((3 * HY_HID, HY_HID)), vec(), vec()],
        out_specs=pl.BlockSpec((tm, 3 * HY_HID), lambda t: (t, 0)),
        out_shape=jax.ShapeDtypeStruct((rows, 3 * HY_HID), BF16),
        compiler_params=_cparams(("arbitrary",)),
        name="filt_mlp",
    )(feats, _stack3_rhs(w1), b1.reshape(1, HY_HID), f1.reshape(1, HY_HID), _stack3_rhs(w2),
      b2.reshape(1, HY_HID), f2.reshape(1, HY_HID))


def _taps_chunk(h2s, t, w3f, w3b, delta, fwd, zero):
    win = jnp.exp(-t * delta)
    h3 = _dot(h2s, jnp.concatenate([w3f, w3b], axis=1))
    return jnp.where(zero, 0.0, win * jnp.where(fwd, h3[:, :LANES], h3[:, LANES:]))


def _stage_b_fwd(buf_a, j, e_ref):
    cols = []
    for k1 in (2 * j, 2 * j + 1):
        re = buf_a[pl.ds(k1, N2, stride=PITCH_A), :]
        im = buf_a[pl.ds(N1 + k1, N2, stride=PITCH_A), :]
        cols.append(jnp.concatenate([re, im], axis=0).astype(BF16))
    return _dot(e_ref[...], jnp.concatenate(cols, axis=1))


def _spec_lat_kernel(h2_ref, rc_ref, w3f_ref, w3b_ref, dl_ref, wt_ref, e_ref, o_ref, taps, buf_a):
    tb = 512
    delta = dl_ref[...]

    ridx = lax.broadcasted_iota(jnp.int32, (tb, LANES), 0)
    fwd = (ridx & (N1 - 1)) < N1 // 2

    def fill(i, acc):
        rows = pl.ds(pl.multiple_of(i * tb, tb), tb)
        zero = ridx == jnp.where(i == 0, N1 // 2, -1)
        tp = _taps_chunk(h2_ref[rows, :], rc_ref[rows, 0:1], w3f_ref[...], w3b_ref[...], delta, fwd, zero)
        taps[rows, :] = tp
        return acc + jnp.sum(jnp.abs(tp), axis=0, keepdims=True)

    l1 = lax.fori_loop(0, FFT_N // tb, fill, jnp.zeros((1, LANES), F32), unroll=4)
    inv = 1.0 / (l1 + EPS)

    def stage_a(n2, c):
        rhs = taps[pl.ds(pl.multiple_of(n2 * N1, N1), N1), :]
        buf_a[pl.ds(pl.multiple_of(n2 * PITCH_A, 8), 2 * N1), :] = _dot(wt_ref[n2], rhs.astype(BF16))
        return c

    lax.fori_loop(0, N2, stage_a, 0, unroll=UNROLL)

    def stage_b(j, c):
        y = _stage_b_fwd(buf_a, j, e_ref)
        rows = 2 * N2
        o_ref[pl.ds(pl.multiple_of(2 * j * rows, 2 * rows), rows), :] = (y[:, :LANES] * inv).astype(o_ref.dtype)
        o_ref[pl.ds(pl.multiple_of(2 * j * rows + rows, rows), rows), :] = (y[:, LANES:] * inv).astype(o_ref.dtype)
        return c

    lax.fori_loop(0, N1 // 2, stage_b, 0, unroll=2 * UNROLL)


def _spec_lat(h2p, rcp, w3, delta, cst):
    nct = HY_W // LANES
    return pl.pallas_call(
        _spec_lat_kernel,
        grid=(2, nct),
        in_specs=[
            _const_spec((FFT_N, 3 * HY_HID)),
            _const_spec((FFT_N, 8)),
            pl.BlockSpec((3 * HY_HID, LANES), lambda o, c: (0, o * nct + c)),
            pl.BlockSpec((3 * HY_HID, LANES), lambda o, c: (0, 2 * nct + o * nct + c)),
            pl.BlockSpec((1, LANES), lambda o, c: (0, c)),
            _const_spec((N2, 2 * N1, N1)),
            _const_spec((2 * N2, 2 * N2)),
        ],
        out_specs=pl.BlockSpec((None, None, 2 * FFT_N, LANES), lambda o, c: (o, c, 0, 0)),
        out_shape=jax.ShapeDtypeStruct((2, nct, 2 * FFT_N, LANES), BF16),
        scratch_shapes=[pltpu.VMEM((FFT_N, LANES), F32), pltpu.VMEM((N2 * PITCH_A, LANES), F32)],
        compiler_params=_cparams(("arbitrary", "arbitrary")),
        name="spec_lat",
    )(h2p, rcp, w3, w3, delta, cst["wt"], cst["e"])


def _hy_lat_kernel(ua_ref, ub_ref, h_ref, bias_ref, wa_ref, e_ref, g_ref, mi_ref,
                   o_ref, zin, buf_a, buf_d, upad):
    s = pl.program_id(2)

    def zslab(n2):
        return zin[pl.ds(pl.multiple_of(n2 * PITCH_Z, 8), 2 * N2), :]

    def repitch(n1, c):
        src = pl.ds(pl.multiple_of(n1 * GRID_W, GRID_W), GRID_W)
        dst = pl.ds(pl.multiple_of(n1 * PITCH_U, 8), GRID_W)
        upad[0, dst, :] = ua_ref[src, :].astype(F32)
        upad[1, dst, :] = ub_ref[src, :].astype(F32)
        return c
    lax.fori_loop(0, N2, repitch, 0, unroll=UNROLL)

    def col_slices(n2):
        return jnp.concatenate([upad[j, pl.ds(n2, N2, stride=PITCH_U), :] for j in range(2)], axis=0)

    @pl.when(s == 0)
    def _():
        def body(n2, c):
            zin[pl.ds(pl.multiple_of(n2 * PITCH_Z, 8), 2 * N2), :] = col_slices(n2)
            return c
        lax.fori_loop(0, N2, body, 0, unroll=UNROLL)

    @pl.when(s > 0)
    def _():
        def stage_a(n2, c):
            buf_a[pl.ds(pl.multiple_of(n2 * PITCH_A, 8), 2 * N1), :] = _dot(wa_ref[n2], zslab(n2).astype(BF16))
            return c
        lax.fori_loop(0, N2, stage_a, 0, unroll=2 * UNROLL)

        def stage_b(j, c):
            y = _stage_b_fwd(buf_a, j, e_ref)
            prods = []
            for i in range(2):
                yk = y[:, i * LANES:(i + 1) * LANES]
                hb = h_ref[pl.ds(pl.multiple_of((2 * j + i) * 2 * N2, 2 * N2), 2 * N2), :].astype(F32)
                yr, yi = yk[:N2], yk[N2:]
                hr, hi = hb[:N2], hb[N2:]
                prods.append(jnp.concatenate([yr * hr - yi * hi, yr * hi + yi * hr], axis=0).astype(BF16))
            d = _dot(g_ref[...], jnp.concatenate(prods, axis=1))
            for i in range(2):
                rows = pl.ds(pl.multiple_of((2 * j + i) * PITCH_D, 8), 2 * N2)
                buf_d[rows, :] = d[:, i * LANES:(i + 1) * LANES]
            return c
        lax.fori_loop(0, N1 // 2, stage_b, 0, unroll=2 * UNROLL)

        def stage_ai(n2, c):
            re = buf_d[pl.ds(n2, N1, stride=PITCH_D), :]
            im = buf_d[pl.ds(N2 + n2, N1, stride=PITCH_D), :]
            x = jnp.concatenate([re, im], axis=0).astype(BF16)
            y = _dot(mi_ref[n2], x)
            rows = pl.ds(pl.multiple_of(n2 * PITCH_Z, 8), 2 * N2)
            zin[rows, :] = col_slices(n2) * (y + bias_ref[...] * zin[rows, :])
            return c
        lax.fori_loop(0, N2, stage_ai, 0, unroll=2 * UNROLL)

    @pl.when(s == 2)
    def _():
        def emit(n1, c):
            dst = pl.ds(pl.multiple_of(n1 * GRID_W, GRID_W), GRID_W)
            o_ref[0, dst, :] = zin[pl.ds(n1, GRID_W, stride=PITCH_Z), :].astype(o_ref.dtype)
            o_ref[1, dst, :] = zin[pl.ds(N2 + n1, GRID_W, stride=PITCH_Z), :].astype(o_ref.dtype)
            return c
        lax.fori_loop(0, N2, emit, 0, unroll=UNROLL)


def _hy_lat(hy, hspec, bias, cst):
    nct = HY_W // LANES
    part = lambda c, p, s: s * nct + c
    order = lambda s: jnp.maximum(s - 1, 0)
    consts = [cst[n] for n in ("wa", "e", "g", "mi")]
    return pl.pallas_call(
        _hy_lat_kernel,
        grid=(nct, NB // 2, 3),
        in_specs=[
            pl.BlockSpec((SEQ, LANES), lambda c, p, s: (2 * p, part(c, p, s))),
            pl.BlockSpec((SEQ, LANES), lambda c, p, s: (2 * p + 1, part(c, p, s))),
            pl.BlockSpec((None, None, 2 * FFT_N, LANES), lambda c, p, s: (order(s), c, 0, 0)),
            pl.BlockSpec((None, 1, LANES), lambda c, p, s: (order(s), 0, c)),
        ] + [_const_spec(x.shape) for x in consts],
        out_specs=pl.BlockSpec((2, SEQ, LANES), lambda c, p, s: (p, 0, c)),
        out_shape=jax.ShapeDtypeStruct((NB, SEQ, HY_W), BF16),
        scratch_shapes=[
            pltpu.VMEM((N2 * PITCH_Z, LANES), F32),
            pltpu.VMEM((N2 * PITCH_A, LANES), F32),
            pltpu.VMEM((N1 * PITCH_D, LANES), F32),
            pltpu.VMEM((2, N2 * PITCH_U, LANES), F32),
        ],
        compiler_params=_cparams(("arbitrary", "arbitrary", "arbitrary")),
        name="hy_lat",
    )(hy, hy, hspec, bias, *consts)


def _spec_ctx_kernel(h2_ref, rc_ref, w3f_ref, w3b_ref, dl_ref, wt_ref, o_ref):
    ridx = lax.broadcasted_iota(jnp.int32, (CTX_N, LANES), 0)
    tp = _taps_chunk(h2_ref[...], rc_ref[:, 0:1], w3f_ref[...], w3b_ref[...], dl_ref[...], ridx < CTX, ridx == CTX)
    l1 = jnp.sum(jnp.abs(tp), axis=0, keepdims=True)
    o_ref[...] = _dot(wt_ref[...], (tp * (1.0 / (l1 + EPS))).astype(BF16))


def _spec_ctx(h2, rc, w3, delta, cst):
    nct = HY_W // LANES
    return pl.pallas_call(
        _spec_ctx_kernel,
        grid=(2, nct),
        in_specs=[
            _const_spec((CTX_N, 3 * HY_HID)),
            _const_spec((CTX_N, 8)),
            pl.BlockSpec((3 * HY_HID, LANES), lambda o, c: (0, o * nct + c)),
            pl.BlockSpec((3 * HY_HID, LANES), lambda o, c: (0, 2 * nct + o * nct + c)),
            pl.BlockSpec((1, LANES), lambda o, c: (0, c)),
            _const_spec((2 * CTX_N, CTX_N)),
        ],
        out_specs=pl.BlockSpec((None, None, 2 * CTX_N, LANES), lambda o, c: (o, c, 0, 0)),
        out_shape=jax.ShapeDtypeStruct((2, nct, 2 * CTX_N, LANES), F32),
        compiler_params=_cparams(("arbitrary", "arbitrary")),
        name="spec_ctx",
    )(h2, rc, w3, w3, delta, cst["cwt"])


def _hy_ctx_kernel(va, vb, x1a, x1b, x2a, x2b, h_ref, bias_ref, wf_ref, wi_ref, o_ref):
    def pair(ra, rb):
        return jnp.concatenate([ra[...], rb[...]], axis=0).astype(F32)

    z = pair(va, vb)
    gates = (pair(x1a, x1b), pair(x2a, x2b))
    for o in range(2):
        y = _dot(wf_ref[...], z.astype(BF16))
        hb = h_ref[o]
        yr, yi = y[:CTX_N], y[CTX_N:]
        hr, hi = hb[:CTX_N], hb[CTX_N:]
        prod = jnp.concatenate([yr * hr - yi * hi, yr * hi + yi * hr], axis=0)
        conv = _dot(wi_ref[...], prod.astype(BF16))
        z = gates[o] * (conv + bias_ref[o] * z)
    o_ref[0] = z[:CTX].astype(o_ref.dtype)
    o_ref[1] = z[CTX:].astype(o_ref.dtype)


def _hy_ctx(hy, hspec, bias, cst):
    nct = HY_W // LANES
    base = R_LAT // CTX

    def blk(j, part):
        return pl.BlockSpec((CTX, LANES), lambda c, p: (base + 2 * p + j, part * nct + c))

    consts = [cst["cwf"], cst["cwi"]]
    return pl.pallas_call(
        _hy_ctx_kernel,
        grid=(nct, NB // 2),
        in_specs=[blk(0, 0), blk(1, 0), blk(0, 1), blk(1, 1), blk(0, 2), blk(1, 2),
                  pl.BlockSpec((2, None, 2 * CTX_N, LANES), lambda c, p: (0, c, 0, 0)),
                  pl.BlockSpec((2, 1, LANES), lambda c, p: (0, 0, c))]
        + [_const_spec(x.shape) for x in consts],
        out_specs=pl.BlockSpec((2, CTX, LANES), lambda c, p: (p, 0, c)),
        out_shape=jax.ShapeDtypeStruct((NB, CTX, HY_W), BF16),
        compiler_params=_cparams(("arbitrary", "arbitrary")),
        name="hy_ctx",
    )(hy, hy, hy, hy, hy, hy, hspec, bias, *consts)


def _stack(c):
    return np.block([[c.real, -c.imag], [c.imag, c.real]])


def _fft_constants():
    cst = {}
    k1 = np.arange(N1)[:, None]
    ca = np.exp(-2j * np.pi * k1 * np.arange(N1 // 2)[None, :] / N1)
    tw = np.exp(-2j * np.pi * k1 * np.arange(N2)[None, :] / FFT_N)
    ct = np.exp(-2j * np.pi * k1 * np.arange(N1)[None, :] / N1)
    cst["wa"] = jnp.asarray(np.stack([_stack(tw[:, n2:n2 + 1] * ca) for n2 in range(N2)]), F32).astype(BF16)
    wt = [tw[:, n2:n2 + 1] * ct for n2 in range(N2)]
    cst["wt"] = jnp.asarray(np.stack([np.concatenate([m.real, m.imag], axis=0) for m in wt]), F32).astype(BF16)
    e = np.exp(-2j * np.pi * np.arange(N2)[:, None] * np.arange(N2)[None, :] / N2)
    cst["e"] = jnp.asarray(_stack(e), F32).astype(BF16)
    cst["g"] = jnp.asarray(_stack(np.conj(e).T), F32).astype(BF16)
    bi = np.exp(2j * np.pi * np.arange(N1 // 2)[:, None] * np.arange(N1)[None, :] / N1) / FFT_N
    cst["mi"] = jnp.asarray(np.stack([_stack(bi * np.conj(tw[:, n2])[None, :]) for n2 in range(N2)]),
                            F32).astype(BF16)
    kk = np.arange(CTX_N)[:, None]
    cf = np.exp(-2j * np.pi * kk * np.arange(CTX)[None, :] / CTX_N)
    cst["cwf"] = jnp.asarray(_stack(cf), F32).astype(BF16)
    cfull = np.exp(-2j * np.pi * kk * np.arange(CTX_N)[None, :] / CTX_N)
    cst["cwt"] = jnp.asarray(np.concatenate([cfull.real, cfull.imag], axis=0), F32).astype(BF16)
    ci = np.exp(2j * np.pi * np.arange(CTX)[:, None] * np.arange(CTX_N)[None, :] / CTX_N) / CTX_N
    cst["cwi"] = jnp.asarray(_stack(ci), F32).astype(BF16)
    return cst


def _filter_positions(L, permute):
    bands = (HY_EMB - 1) // 2
    t = np.linspace(0.0, 1.0, L)[:, None]
    w = (2.0 * math.pi / L) * np.arange(L)[:, None]
    f = np.linspace(1e-4, bands - 1, bands)[None, :]
    z = np.concatenate([t, np.cos(f * w), -np.sin(f * w)], axis=-1)
    n = np.arange(2 * L)
    pos = np.where(n < L, n, 2 * L - n)
    pos = np.where(n == L, 0, pos)
    fsel = (n < L).astype(np.float32)
    bsel = (n > L).astype(np.float32)
    if permute:
        order = (np.arange(N1)[None, :] * N2 + np.arange(N2)[:, None]).reshape(-1)
        pos, fsel, bsel = pos[order], fsel[order], bsel[order]
    feats = np.pad(z[pos], ((0, 0), (0, HY_HID - HY_EMB)))
    rc = np.concatenate([t[pos], fsel[:, None], bsel[:, None], np.zeros((2 * L, 5))], axis=1)
    return jnp.asarray(feats, F32), jnp.asarray(rc, F32)


def _decay_rates():
    max_decay = math.log(HY_TARGET) / HY_FAST
    min_decay = math.log(HY_TARGET) / HY_SLOW
    return jnp.asarray(np.abs(np.linspace(min_decay, max_decay, HY_W))[None, :], F32)


def _gla_constants():
    r = np.arange(GLA_BLK)
    same = (r[:, None] // CHUNK) == (r[None, :] // CHUNK)
    fwd = same & (r[:, None] >= r[None, :])
    bwd = same & (r[:, None] <= r[None, :])
    return jnp.asarray(np.stack([fwd, bwd]).astype(np.float32)).astype(BF16)


def kernel(x, c, ctx, c_ctx, ada_w, ada_b, ffn1_norm_g, ffn1_w_gate, ffn1_w_up, ffn1_w_down, mix_norm_g, w_in,
           gla_wa_f, gla_ba_f, gla_wa_b, gla_ba_b, gla_norm_g, w_gla_out, hy_conv_w, hy_conv_b, hy_f_w1, hy_f_b1,
           hy_f_freq1, hy_f_w2, hy_f_b2, hy_f_freq2, hy_f_w3, hy_bias, w_hy_out, w_out, ffn2_norm_g, ffn2_w_gate,
           ffn2_w_up, ffn2_w_down, final_norm_g):
    cst = _fft_constants()
    tri = _gla_constants()
    delta = _decay_rates()
    feats_lat, rc_lat = _filter_positions(SEQ, True)
    feats_ctx, rc_ctx = _filter_positions(CTX, False)

    cvec = jnp.concatenate([c, c_ctx[None, :], jnp.zeros((3, D), F32)], axis=0)
    mods = _mods(cvec, ada_w, ada_b).reshape(DEPTH, 8, 1, N_MOD * D)
    xs = (x.reshape(R_LAT, D), ctx.reshape(R_CTX, D))

    bf = lambda w: w.astype(BF16)
    ffn1_w = (bf(ffn1_w_gate), bf(ffn1_w_up), bf(ffn1_w_down))
    ffn2_w = (bf(ffn2_w_gate), bf(ffn2_w_up), bf(ffn2_w_down))
    out_w = (bf(w_gla_out), bf(w_hy_out), bf(w_out))
    w_t = jnp.swapaxes(w_in, 1, 2)
    wp = bf(w_t)
    wa_hi, wa_lo = _split(w_t[:, OFF_A:OFF_Q])
    conv_b = hy_conv_b[:, None, :]

    zpad = jnp.zeros((LOWRANK, KEY_W), F32)
    for i in range(DEPTH):
        last = i == DEPTH - 1
        xa = _ffn(xs, mods, i, 0, ffn1_norm_g[i], *ffn1_w, R_ALL)
        kvq, a, r, hy, gt = _proj(xa, mods, i, mix_norm_g[i], wp, wa_hi, wa_lo, hy_conv_w, conv_b)

        wa_cat = jnp.stack([jnp.concatenate([gla_wa_f[i], zpad], axis=0),
                            jnp.concatenate([zpad, gla_wa_b[i]], axis=0)])
        w_hi, w_lo = _split(wa_cat)
        ba_cat = jnp.stack([gla_ba_f[i], gla_ba_b[i]])[:, None, :]
        o2 = _gla(kvq, a, jnp.concatenate([w_hi, w_hi, w_lo], axis=1), ba_cat, tri)

        w1p = jnp.pad(hy_f_w1[i], ((0, HY_HID - HY_EMB), (0, 0)))
        mlp = (w1p, hy_f_b1[i], hy_f_freq1[i], hy_f_w2[i], hy_f_b2[i], hy_f_freq2[i])
        bias = hy_bias[i][:, None, :]
        w3s = _stack3_rhs(hy_f_w3[i])
        spec_lat = _spec_lat(_filt_mlp(feats_lat, *mlp), rc_lat, w3s, delta, cst)
        zs = (_hy_lat(hy, spec_lat, bias, cst).reshape(R_LAT, HY_W),)

        rows = R_LAT if last else R_ALL
        if not last:
            spec_ctx = _spec_ctx(_filt_mlp(feats_ctx, *mlp), rc_ctx, w3s, delta, cst)
            zs += (_hy_ctx(hy, spec_ctx, bias, cst).reshape(R_CTX, HY_W),)
        xa = _mix(xa, mods, i, o2, r, zs, gt, gla_norm_g[i], *out_w, rows)
        xs = (_ffn((xa,), mods, i, 2, ffn2_norm_g[i], *ffn2_w, rows, final_g=final_norm_g if last else None),)
    return xs[0].reshape(NB, SEQ, D)
```

```python
import functools
import math

import numpy as np
import jax
import jax.numpy as jnp
from jax import lax
from jax.experimental import pallas as pl
from jax.experimental.pallas import tpu as pltpu

F32 = jnp.float32
BF16 = jnp.bfloat16

D = 1024
NB = 4
SEQ = 4096
DEPTH = 2
CTX = 256
GRID_W = 64
D_FF = 2816
HALF_STEP = 0.5
N_MOD = 9
EPS = 1e-6

HEADS = 4
DK = 128
DV = 256
LOWRANK = 16
TAU = 16.0
CHUNK = 64
KEY_W = HEADS * DK
VAL_W = HEADS * DV

HY_W = 1024
HY_EMB = 33
HY_HID = 64
HY_FAST = 0.3
HY_SLOW = 1.5
HY_TARGET = 1e-2

OFF_K = 0
OFF_V = OFF_K + KEY_W
OFF_A = OFF_V + VAL_W
OFF_Q = OFF_A + 2 * LOWRANK
OFF_R = OFF_Q + KEY_W
OFF_H = OFF_R + VAL_W
OFF_G = OFF_H + 3 * HY_W
IN_W = OFF_G + 2 * D

R_LAT = NB * SEQ
R_CTX = NB * CTX
R_ALL = R_LAT + R_CTX

LANES = 128
VMEM_BYTES_V7X = 64 * 1024 * 1024
VMEM_LIMIT = VMEM_BYTES_V7X * 7 // 8

FFT_N = 2 * SEQ
N1 = 128
N2 = 64
PITCH_A = 264
PITCH_D = 136
PITCH_U = 72
PITCH_Z = 136
UNROLL = 8
CTX_N = 2 * CTX


def _cparams(sem):
    return pltpu.CompilerParams(dimension_semantics=sem, vmem_limit_bytes=VMEM_LIMIT)


def _const_spec(shape):
    nd = len(shape)
    return pl.BlockSpec(shape, lambda *_: (0,) * nd, pipeline_mode=pl.Buffered(1))


def _layer_spec(shape, layer):
    nd = len(shape)
    return pl.BlockSpec((None,) + tuple(shape), lambda *_: (layer,) + (0,) * nd, pipeline_mode=pl.Buffered(1))


def _split(x):
    hi = x.astype(BF16)
    lo = (x - hi.astype(F32)).astype(BF16)
    return hi, lo


def _dot(a, b):
    return jnp.dot(a, b, preferred_element_type=F32)


def _sigmoid(x):
    return 1.0 / (1.0 + jnp.exp(-x))


def _rms(x, g):
    ms = jnp.mean(x * x, axis=-1, keepdims=True)
    return x * lax.rsqrt(ms + EPS) * g


def _mods_kernel(c_ref, w_ref, b_ref, o_ref):
    c = c_ref[...]
    s = (c * _sigmoid(c)).astype(BF16)
    o_ref[...] = _dot(s, w_ref[...].astype(BF16)) + b_ref[...]


def _mods(cvec, ada_w, ada_b):
    tn = 1024
    return pl.pallas_call(
        _mods_kernel,
        grid=(DEPTH, N_MOD * D // tn),
        in_specs=[
            pl.BlockSpec((8, D), lambda l, j: (0, 0)),
            pl.BlockSpec((None, D, tn), lambda l, j: (l, 0, j)),
            pl.BlockSpec((None, 1, tn), lambda l, j: (l, 0, j)),
        ],
        out_specs=pl.BlockSpec((None, 8, tn), lambda l, j: (l, 0, j)),
        out_shape=jax.ShapeDtypeStruct((DEPTH, 8, N_MOD * D), F32),
        compiler_params=_cparams(("arbitrary", "arbitrary")),
        name="mods",
    )(cvec, ada_w, ada_b.reshape(DEPTH, 1, N_MOD * D))


def _mod_row(t, tm):
    lat_tiles = R_LAT // tm
    per_batch = SEQ // tm
    return jnp.where(t < lat_tiles, t // per_batch, NB)


def _row_specs(tm, width, split):
    if not split:
        return [pl.BlockSpec((tm, width), lambda t: (t, 0))]
    lat_tiles = R_LAT // tm
    return [pl.BlockSpec((tm, width), lambda t: (jnp.minimum(t, lat_tiles - 1), 0)),
            pl.BlockSpec((tm, width), lambda t: (jnp.maximum(t - lat_tiles, 0), 0))]


def _row_tile(refs, tm):
    if len(refs) == 1:
        return refs[0][...]
    return jnp.where(pl.program_id(0) < R_LAT // tm, refs[0][...], refs[1][...])


def _ffn_kernel(*refs, final, nx, tm):
    xs, (m_ref, g_ref, wg_ref, wu_ref, wd_ref), rest = refs[:nx], refs[nx:nx + 5], refs[nx + 5:]
    if final:
        fg_ref, o_ref = rest
    else:
        (o_ref,) = rest
    x = _row_tile(xs, tm)
    m = m_ref[...]
    shift, scale, gate = m[:, :D], m[:, D:2 * D], m[:, 2 * D:]
    h = (_rms(x, g_ref[...]) * (1.0 + scale) + shift).astype(BF16)
    a = _dot(h, wg_ref[...])
    u = _dot(h, wu_ref[...])
    act = (a * _sigmoid(a) * u).astype(BF16)
    y = _dot(act, wd_ref[...])
    out = x + (HALF_STEP * gate) * y
    if final:
        out = _rms(out, fg_ref[...])
    o_ref[...] = out


def _ffn(xs, mods, layer, slot, norm_g, wg, wu, wd, rows, final_g=None):
    tm = 512
    final = final_g is not None
    in_specs = _row_specs(tm, D, len(xs) == 2) + [
        pl.BlockSpec((None, None, 1, 3 * D), lambda t: (layer, _mod_row(t, tm), 0, slot)),
        _const_spec((1, D)),
        _layer_spec((D, D_FF), layer),
        _layer_spec((D, D_FF), layer),
        _layer_spec((D_FF, D), layer),
    ]
    args = list(xs) + [mods, norm_g.reshape(1, D), wg, wu, wd]
    if final:
        in_specs.append(_const_spec((1, D)))
        args.append(final_g.reshape(1, D))
    return pl.pallas_call(
        functools.partial(_ffn_kernel, final=final, nx=len(xs), tm=tm),
        grid=(rows // tm,),
        in_specs=in_specs,
        out_specs=pl.BlockSpec((tm, D), lambda t: (t, 0)),
        out_shape=jax.ShapeDtypeStruct((rows, D), F32),
        compiler_params=_cparams(("arbitrary",)),
        name="ffn",
    )(*args)


def _proj_kernel(x_ref, m_ref, g_ref, wkv_ref, wq_ref, wahi_ref, walo_ref, wr_ref, wh_ref, wg_ref, cw_ref, cb_ref,
                 kvq_ref, a_ref, r_ref, hy_ref, gt_ref, *, tm):
    x = x_ref[...]
    m = m_ref[...]
    shift, scale = m[:, :D], m[:, D:2 * D]
    h = _rms(x, g_ref[...]) * (1.0 + scale) + shift
    hb = h.astype(BF16)
    period = jnp.where(pl.program_id(0) < R_LAT // tm, GRID_W, CTX)
    pos = lax.broadcasted_iota(jnp.int32, (tm, 1), 0) & (period - 1)
    first, last = pos == 0, pos == period - 1

    def hyena_part(part):
        cols = slice(part * HY_W, (part + 1) * HY_W)
        u = _dot_nt(hb, wh_ref[0, cols, :])
        prv = jnp.where(first, 0.0, pltpu.roll(u, 1, 0))
        nxt = jnp.where(last, 0.0, pltpu.roll(u, tm - 1, 0))
        conv = cw_ref[0:1, cols] * prv + cw_ref[1:2, cols] * u + cw_ref[2:3, cols] * nxt + cb_ref[:, cols]
        hy_ref[:, cols] = conv.astype(hy_ref.dtype)

    hyena_part(0)
    kvq_ref[:, :KEY_W + VAL_W] = _dot_nt(hb, wkv_ref[0]).astype(kvq_ref.dtype)
    kvq_ref[:, KEY_W + VAL_W:] = _dot_nt(hb, wq_ref[0]).astype(kvq_ref.dtype)
    hyena_part(1)
    gt_ref[...] = _dot_nt(hb, wg_ref[0]).astype(gt_ref.dtype)
    hyena_part(2)
    r_ref[...] = _dot_nt(hb, wr_ref[0]).astype(r_ref.dtype)
    hl = (h - hb.astype(F32)).astype(BF16)
    a_ref[...] = _dot_nt(hb, wahi_ref[...]) + _dot_nt(hl, wahi_ref[...]) + _dot_nt(hb, walo_ref[...])


def _proj(x, mods, layer, norm_g, wp, wa_hi, wa_lo, conv_w, conv_b):
    tm = 512
    nk = 2 * KEY_W + VAL_W
    na = 2 * LOWRANK
    row = lambda w: pl.BlockSpec((tm, w), lambda t: (t, 0))

    def wrows(start, stop):
        return pl.BlockSpec((pl.Element(1), pl.Element(stop - start), pl.Element(D)),
                            lambda t: (layer, start, 0), pipeline_mode=pl.Buffered(1))

    def wcols(width, blk):
        return pl.BlockSpec((None, width, D), lambda t: (layer, blk, 0), pipeline_mode=pl.Buffered(1))

    return pl.pallas_call(
        functools.partial(_proj_kernel, tm=tm),
        grid=(R_ALL // tm,),
        in_specs=[
            row(D),
            pl.BlockSpec((None, None, 1, 3 * D), lambda t: (layer, _mod_row(t, tm), 0, 1)),
            _const_spec((1, D)),
            wrows(OFF_K, OFF_A),
            wrows(OFF_Q, OFF_R),
            wcols(na, 0),
            wcols(na, 0),
            wrows(OFF_R, OFF_H),
            wrows(OFF_H, OFF_G),
            wrows(OFF_G, IN_W),
            pl.BlockSpec((None, 3, 3 * HY_W), lambda t: (layer, 0, 0)),
            pl.BlockSpec((None, 1, 3 * HY_W), lambda t: (layer, 0, 0)),
        ],
        out_specs=[row(nk), row(na), row(VAL_W), row(3 * HY_W), row(2 * D)],
        out_shape=[
            jax.ShapeDtypeStruct((R_ALL, nk), BF16),
            jax.ShapeDtypeStruct((R_ALL, na), F32),
            jax.ShapeDtypeStruct((R_ALL, VAL_W), BF16),
            jax.ShapeDtypeStruct((R_ALL, 3 * HY_W), BF16),
            jax.ShapeDtypeStruct((R_ALL, 2 * D), BF16),
        ],
        compiler_params=_cparams(("arbitrary",)),
        name="proj",
    )(x, mods, norm_g.reshape(1, D), wp, wp, wa_hi, wa_lo, wp, wp, wp, conv_w, conv_b)


GLA_BLK = 256
GLA_STEPS = 1 + SEQ // GLA_BLK


def _gla_row_block(b, d, s):
    lat = jnp.where(d == 0, s - 1, SEQ // GLA_BLK - s)
    return jnp.where(s == 0, R_LAT // GLA_BLK + b, b * (SEQ // GLA_BLK) + lat)


def _dot_nt(a, b):
    return lax.dot_general(a, b, (((1,), (1,)), ((), ())), preferred_element_type=F32)


def _dot_tn(a, b):
    return lax.dot_general(a, b, (((0,), (0,)), ((), ())), preferred_element_type=F32)


def _gla_body(v_ref, k0_ref, q0_ref, a0_ref, kn_ref, qn_ref, an_ref, wa_ref, ba_ref, tri_ref, o_ref,
              st_ref, qd0, qd1, kn0, kn1, ke0, ke1, bc0, bc1):
    qd_s, kn_s, ke_s, bc_s = (qd0, qd1), (kn0, kn1), (ke0, ke1), (bc0, bc1)
    d = pl.program_id(1)
    s = pl.program_id(2)
    nchunk = GLA_BLK // CHUNK
    tot_row = jnp.where(d == 0, CHUNK - 1, 0)

    def log_gates(a_blk):
        a_hi, a_lo = _split(a_blk)
        pre = _dot(jnp.concatenate([a_hi, a_lo, a_hi], axis=1), wa_ref[...]) + ba_ref[...]
        la = (jnp.minimum(pre, 0.0) - jnp.log(1.0 + jnp.exp(-jnp.abs(pre)))) * (1.0 / TAU)
        return _split(la)

    def cum_decay(slot, la_hi, la_lo):
        tri = tri_ref[...]
        bcum = _dot(tri, la_hi) + _dot(tri, la_lo)
        bc_s[slot][...] = bcum
        return bcum

    def decay_weight(slot, bcum, k_blk, q_blk):
        btot = jnp.concatenate(
            [jnp.broadcast_to(bc_s[slot][pl.ds(j * CHUNK + tot_row, 1), :], (CHUNK, KEY_W)) for j in range(nchunk)],
            axis=0)
        k = k_blk.astype(F32)
        q = q_blk.astype(F32)
        qd_s[slot][...] = (q * (DK ** -0.5) * jnp.exp(bcum)).astype(BF16)
        kn_s[slot][...] = (k * jnp.exp(-bcum)).astype(BF16)
        ke_s[slot][...] = (k * jnp.exp(btot - bcum)).astype(BF16)

    @pl.when(s == 0)
    def _():
        st_ref[...] = jnp.zeros_like(st_ref)
        decay_weight(0, cum_decay(0, *log_gates(a0_ref[...])), k0_ref[...], q0_ref[...])

    def scan_block(cur):
        seen = tri_ref[...].astype(F32) > 0.5

        def intra(h):
            ksl = slice(h * DK, (h + 1) * DK)
            att = jnp.where(seen, _dot_nt(qd_s[cur][:, ksl], kn_s[cur][:, ksl]), 0.0).astype(BF16)
            o_ref[:, h * DV:(h + 1) * DV] = _dot(att, v_ref[:, h * DV:(h + 1) * DV])

        def inter(j):
            r0 = pl.multiple_of(jnp.where(d == 0, j, nchunk - 1 - j) * CHUNK, CHUNK)
            rows = pl.ds(r0, CHUNK)
            dec = jnp.exp(bc_s[cur][pl.ds(r0 + tot_row, 1), :])
            for h in range(HEADS):
                ksl = slice(h * DK, (h + 1) * DK)
                vsl = slice(h * DV, (h + 1) * DV)
                st = st_ref[h]
                o_ref[rows, vsl] += _dot_nt(qd_s[cur][rows, ksl], st.astype(BF16))
                upd = _dot_tn(v_ref[rows, vsl], ke_s[cur][rows, ksl])
                st_ref[h] = st * dec[:, ksl] + upd

        intra(0)
        la_hi, la_lo = log_gates(an_ref[...])
        for h in range(1, HEADS):
            intra(h)
        bcum = cum_decay(1 - cur, la_hi, la_lo)
        inter(0)
        decay_weight(1 - cur, bcum, kn_ref[...], qn_ref[...])
        for j in range(1, nchunk):
            inter(j)

    for parity in range(2):
        pl.when(s % 2 == parity)(functools.partial(scan_block, parity))


def _gla(kvq, a, wa_cat, ba_cat, tri):
    qblk = (KEY_W + VAL_W) // KEY_W
    first = lambda b, d, s: _gla_row_block(b, d, 0)
    nxt = lambda b, d, s: _gla_row_block(b, d, jnp.minimum(s + 1, GLA_STEPS - 1))

    def kqa(blk):
        return [pl.BlockSpec((GLA_BLK, KEY_W), lambda b, d, s: (blk(b, d, s), 0)),
                pl.BlockSpec((GLA_BLK, KEY_W), lambda b, d, s: (blk(b, d, s), qblk)),
                pl.BlockSpec((GLA_BLK, 2 * LOWRANK), lambda b, d, s: (blk(b, d, s), 0))]

    return pl.pallas_call(
        _gla_body,
        grid=(NB, 2, GLA_STEPS),
        in_specs=[
            pl.BlockSpec((pl.Element(GLA_BLK), pl.Element(VAL_W)),
                         lambda b, d, s: (_gla_row_block(b, d, s) * GLA_BLK, KEY_W)),
        ] + kqa(first) + kqa(nxt) + [
            pl.BlockSpec((None, 6 * LOWRANK, KEY_W), lambda b, d, s: (d, 0, 0)),
            pl.BlockSpec((None, 1, KEY_W), lambda b, d, s: (d, 0, 0)),
            pl.BlockSpec((None, GLA_BLK, GLA_BLK), lambda b, d, s: (d, 0, 0)),
        ],
        out_specs=pl.BlockSpec((None, GLA_BLK, VAL_W), lambda b, d, s: (d, _gla_row_block(b, d, s), 0)),
        out_shape=jax.ShapeDtypeStruct((2, R_ALL, VAL_W), F32),
        scratch_shapes=[
            pltpu.VMEM((HEADS, DV, DK), F32),
        ] + [pltpu.VMEM((GLA_BLK, KEY_W), BF16)] * 6 + [pltpu.VMEM((GLA_BLK, KEY_W), F32)] * 2,
        compiler_params=_cparams(("arbitrary", "arbitrary", "arbitrary")),
        name="gla",
    )(kvq, kvq, kvq, a, kvq, kvq, a, wa_cat, ba_cat, tri)


def _mix_kernel(x_ref, m_ref, of_ref, ob_ref, r_ref, gt_ref, ng_ref, wgo_ref, who_ref, wo_ref, *rest, tm):
    zs, o_ref = rest[:-1], rest[-1]
    x = x_ref[...]
    gate = m_ref[...][:, 2 * D:]
    o = of_ref[...] + ob_ref[...]
    ng = ng_ref[...]
    parts = []
    for h in range(HEADS):
        sl = slice(h * DV, (h + 1) * DV)
        oh = o[:, sl]
        ms = jnp.mean(oh * oh, axis=-1, keepdims=True)
        parts.append(oh * lax.rsqrt(ms + EPS) * ng[:, sl])
    on = jnp.concatenate(parts, axis=-1)
    r = r_ref[...].astype(F32)
    y_gla = _dot((on * (r * _sigmoid(r))).astype(BF16), wgo_ref[...])
    y_hy = _dot(_row_tile(zs, tm), who_ref[...])
    gt = gt_ref[...].astype(F32)
    mix = _sigmoid(gt[:, :D]) * y_gla + _sigmoid(gt[:, D:]) * y_hy
    o_ref[...] = x + gate * _dot(mix.astype(BF16), wo_ref[...])


def _mix(x, mods, layer, o2, r, zs, gt, norm_g, wgo, who, wo, rows):
    tm = 512
    row = lambda w: pl.BlockSpec((tm, w), lambda t: (t, 0))
    return pl.pallas_call(
        functools.partial(_mix_kernel, tm=tm),
        grid=(rows // tm,),
        in_specs=[
            row(D),
            pl.BlockSpec((None, None, 1, 3 * D), lambda t: (layer, _mod_row(t, tm), 0, 1)),
            pl.BlockSpec((None, tm, VAL_W), lambda t: (0, t, 0)),
            pl.BlockSpec((None, tm, VAL_W), lambda t: (1, t, 0)),
            row(VAL_W),
            row(2 * D),
            _const_spec((1, VAL_W)),
            _layer_spec((VAL_W, D), layer),
            _layer_spec((HY_W, D), layer),
            _layer_spec((D, D), layer),
        ] + _row_specs(tm, HY_W, len(zs) == 2),
        out_specs=row(D),
        out_shape=jax.ShapeDtypeStruct((rows, D), F32),
        compiler_params=_cparams(("arbitrary",)),
        name="mix",
    )(x, mods, o2, o2, r, gt, norm_g.reshape(1, VAL_W), wgo, who, wo, *zs)


def _stack3_lhs(a):
    hi, lo = _split(a)
    return jnp.concatenate([hi, lo, hi], axis=1)


def _stack3_rhs(w):
    hi, lo = _split(w)
    return jnp.concatenate([hi, hi, lo], axis=0)


def _filt_mlp_kernel(z_ref, w1_ref, b1_ref, f1_ref, w2_ref, b2_ref, f2_ref, o_ref):
    h = jnp.sin(f1_ref[...] * (_dot(_stack3_lhs(z_ref[...]), w1_ref[...]) + b1_ref[...]))
    h = jnp.sin(f2_ref[...] * (_dot(_stack3_lhs(h), w2_ref[...]) + b2_ref[...]))
    o_ref[...] = _stack3_lhs(h)


def _filt_mlp(feats, w1, b1, f1, w2, b2, f2):
    rows = feats.shape[0]
    tm = 512
    vec = lambda: _const_spec((1, HY_HID))
    return pl.pallas_call(
        _filt_mlp_kernel,
        grid=(rows // tm,),
        in_specs=[pl.BlockSpec((tm, HY_HID), lambda t: (t, 0)), _const_spec((3 * HY_HID, HY_HID)), vec(), vec(),
                  _const_spec((3 * HY_HID, HY_HID)), vec(), vec()],
        out_specs=pl.BlockSpec((tm, 3 * HY_HID), lambda t: (t, 0)),
        out_shape=jax.ShapeDtypeStruct((rows, 3 * HY_HID), BF16),
        compiler_params=_cparams(("arbitrary",)),
        name="filt_mlp",
    )(feats, _stack3_rhs(w1), b1.reshape(1, HY_HID), f1.reshape(1, HY_HID), _stack3_rhs(w2),
      b2.reshape(1, HY_HID), f2.reshape(1, HY_HID))


def _taps_chunk(h2s, t, w3f, w3b, delta, fwd, zero):
    win = jnp.exp(-t * delta)
    h3 = _dot(h2s, jnp.concatenate([w3f, w3b], axis=1))
    return jnp.where(zero, 0.0, win * jnp.where(fwd, h3[:, :LANES], h3[:, LANES:]))


def _stage_b_fwd(buf_a, j, e_ref):
    cols = []
    for k1 in (2 * j, 2 * j + 1):
        re = buf_a[pl.ds(k1, N2, stride=PITCH_A), :]
        im = buf_a[pl.ds(N1 + k1, N2, stride=PITCH_A), :]
        cols.append(jnp.concatenate([re, im], axis=0).astype(BF16))
    return _dot(e_ref[...], jnp.concatenate(cols, axis=1))


def _spec_lat_kernel(h2_ref, rc_ref, w3f_ref, w3b_ref, dl_ref, wt_ref, e_ref, o_ref, taps, buf_a):
    tb = 512
    delta = dl_ref[...]

    ridx = lax.broadcasted_iota(jnp.int32, (tb, LANES), 0)
    fwd = (ridx & (N1 - 1)) < N1 // 2

    def fill(i, acc):
        rows = pl.ds(pl.multiple_of(i * tb, tb), tb)
        zero = ridx == jnp.where(i == 0, N1 // 2, -1)
        tp = _taps_chunk(h2_ref[rows, :], rc_ref[rows, 0:1], w3f_ref[...], w3b_ref[...], delta, fwd, zero)
        taps[rows, :] = tp
        return acc + jnp.sum(jnp.abs(tp), axis=0, keepdims=True)

    l1 = lax.fori_loop(0, FFT_N // tb, fill, jnp.zeros((1, LANES), F32), unroll=4)
    inv = 1.0 / (l1 + EPS)

    def stage_a(n2, c):
        rhs = taps[pl.ds(pl.multiple_of(n2 * N1, N1), N1), :]
        buf_a[pl.ds(pl.multiple_of(n2 * PITCH_A, 8), 2 * N1), :] = _dot(wt_ref[n2], rhs.astype(BF16))
        return c

    lax.fori_loop(0, N2, stage_a, 0, unroll=UNROLL)

    def stage_b(j, c):
        y = _stage_b_fwd(buf_a, j, e_ref)
        rows = 2 * N2
        o_ref[pl.ds(pl.multiple_of(2 * j * rows, 2 * rows), rows), :] = (y[:, :LANES] * inv).astype(o_ref.dtype)
        o_ref[pl.ds(pl.multiple_of(2 * j * rows + rows, rows), rows), :] = (y[:, LANES:] * inv).astype(o_ref.dtype)
        return c

    lax.fori_loop(0, N1 // 2, stage_b, 0, unroll=2 * UNROLL)


def _spec_lat(h2p, rcp, w3, delta, cst):
    nct = HY_W // LANES
    return pl.pallas_call(
        _spec_lat_kernel,
        grid=(2, nct),
        in_specs=[
            _const_spec((FFT_N, 3 * HY_HID)),
            _const_spec((FFT_N, 8)),
            pl.BlockSpec((3 * HY_HID, LANES), lambda o, c: (0, o * nct + c)),
            pl.BlockSpec((3 * HY_HID, LANES), lambda o, c: (0, 2 * nct + o * nct + c)),
            pl.BlockSpec((1, LANES), lambda o, c: (0, c)),
            _const_spec((N2, 2 * N1, N1)),
            _const_spec((2 * N2, 2 * N2)),
        ],
        out_specs=pl.BlockSpec((None, None, 2 * FFT_N, LANES), lambda o, c: (o, c, 0, 0)),
        out_shape=jax.ShapeDtypeStruct((2, nct, 2 * FFT_N, LANES), BF16),
        scratch_shapes=[pltpu.VMEM((FFT_N, LANES), F32), pltpu.VMEM((N2 * PITCH_A, LANES), F32)],
        compiler_params=_cparams(("arbitrary", "arbitrary")),
        name="spec_lat",
    )(h2p, rcp, w3, w3, delta, cst["wt"], cst["e"])


def _hy_lat_kernel(ua_ref, ub_ref, h_ref, bias_ref, wa_ref, e_ref, g_ref, mi_ref,
                   o_ref, zin, buf_a, buf_d, upad):
    s = pl.program_id(2)

    def zslab(n2):
        return zin[pl.ds(pl.multiple_of(n2 * PITCH_Z, 8), 2 * N2), :]

    def repitch(n1, c):
        src = pl.ds(pl.multiple_of(n1 * GRID_W, GRID_W), GRID_W)
        dst = pl.ds(pl.multiple_of(n1 * PITCH_U, 8), GRID_W)
        upad[0, dst, :] = ua_ref[src, :].astype(F32)
        upad[1, dst, :] = ub_ref[src, :].astype(F32)
        return c
    lax.fori_loop(0, N2, repitch, 0, unroll=UNROLL)

    def col_slices(n2):
        return jnp.concatenate([upad[j, pl.ds(n2, N2, stride=PITCH_U), :] for j in range(2)], axis=0)

    @pl.when(s == 0)
    def _():
        def body(n2, c):
            zin[pl.ds(pl.multiple_of(n2 * PITCH_Z, 8), 2 * N2), :] = col_slices(n2)
            return c
        lax.fori_loop(0, N2, body, 0, unroll=UNROLL)

    @pl.when(s > 0)
    def _():
        def stage_a(n2, c):
            buf_a[pl.ds(pl.multiple_of(n2 * PITCH_A, 8), 2 * N1), :] = _dot(wa_ref[n2], zslab(n2).astype(BF16))
            return c
        lax.fori_loop(0, N2, stage_a, 0, unroll=2 * UNROLL)

        def stage_b(j, c):
            y = _stage_b_fwd(buf_a, j, e_ref)
            prods = []
            for i in range(2):
                yk = y[:, i * LANES:(i + 1) * LANES]
                hb = h_ref[pl.ds(pl.multiple_of((2 * j + i) * 2 * N2, 2 * N2), 2 * N2), :].astype(F32)
                yr, yi = yk[:N2], yk[N2:]
                hr, hi = hb[:N2], hb[N2:]
                prods.append(jnp.concatenate([yr * hr - yi * hi, yr * hi + yi * hr], axis=0).astype(BF16))
            d = _dot(g_ref[...], jnp.concatenate(prods, axis=1))
            for i in range(2):
                rows = pl.ds(pl.multiple_of((2 * j + i) * PITCH_D, 8), 2 * N2)
                buf_d[rows, :] = d[:, i * LANES:(i + 1) * LANES]
            return c
        lax.fori_loop(0, N1 // 2, stage_b, 0, unroll=2 * UNROLL)

        def stage_ai(n2, c):
            re = buf_d[pl.ds(n2, N1, stride=PITCH_D), :]
            im = buf_d[pl.ds(N2 + n2, N1, stride=PITCH_D), :]
            x = jnp.concatenate([re, im], axis=0).astype(BF16)
            y = _dot(mi_ref[n2], x)
            rows = pl.ds(pl.multiple_of(n2 * PITCH_Z, 8), 2 * N2)
            zin[rows, :] = col_slices(n2) * (y + bias_ref[...] * zin[rows, :])
            return c
        lax.fori_loop(0, N2, stage_ai, 0, unroll=2 * UNROLL)

    @pl.when(s == 2)
    def _():
        def emit(n1, c):
            dst = pl.ds(pl.multiple_of(n1 * GRID_W, GRID_W), GRID_W)
            o_ref[0, dst, :] = zin[pl.ds(n1, GRID_W, stride=PITCH_Z), :].astype(o_ref.dtype)
            o_ref[1, dst, :] = zin[pl.ds(N2 + n1, GRID_W, stride=PITCH_Z), :].astype(o_ref.dtype)
            return c
        lax.fori_loop(0, N2, emit, 0, unroll=UNROLL)


def _hy_lat(hy, hspec, bias, cst):
    nct = HY_W // LANES
    part = lambda c, p, s: s * nct + c
    order = lambda s: jnp.maximum(s - 1, 0)
    consts = [cst[n] for n in ("wa", "e", "g", "mi")]
    return pl.pallas_call(
        _hy_lat_kernel,
        grid=(nct, NB // 2, 3),
        in_specs=[
            pl.BlockSpec((SEQ, LANES), lambda c, p, s: (2 * p, part(c, p, s))),
            pl.BlockSpec((SEQ, LANES), lambda c, p, s: (2 * p + 1, part(c, p, s))),
            pl.BlockSpec((None, None, 2 * FFT_N, LANES), lambda c, p, s: (order(s), c, 0, 0)),
            pl.BlockSpec((None, 1, LANES), lambda c, p, s: (order(s), 0, c)),
        ] + [_const_spec(x.shape) for x in consts],
        out_specs=pl.BlockSpec((2, SEQ, LANES), lambda c, p, s: (p, 0, c)),
        out_shape=jax.ShapeDtypeStruct((NB, SEQ, HY_W), BF16),
        scratch_shapes=[
            pltpu.VMEM((N2 * PITCH_Z, LANES), F32),
            pltpu.VMEM((N2 * PITCH_A, LANES), F32),
            pltpu.VMEM((N1 * PITCH_D, LANES), F32),
            pltpu.VMEM((2, N2 * PITCH_U, LANES), F32),
        ],
        compiler_params=_cparams(("arbitrary", "arbitrary", "arbitrary")),
        name="hy_lat",
    )(hy, hy, hspec, bias, *consts)


def _spec_ctx_kernel(h2_ref, rc_ref, w3f_ref, w3b_ref, dl_ref, wt_ref, o_ref):
    ridx = lax.broadcasted_iota(jnp.int32, (CTX_N, LANES), 0)
    tp = _taps_chunk(h2_ref[...], rc_ref[:, 0:1], w3f_ref[...], w3b_ref[...], dl_ref[...], ridx < CTX, ridx == CTX)
    l1 = jnp.sum(jnp.abs(tp), axis=0, keepdims=True)
    o_ref[...] = _dot(wt_ref[...], (tp * (1.0 / (l1 + EPS))).astype(BF16))


def _spec_ctx(h2, rc, w3, delta, cst):
    nct = HY_W // LANES
    return pl.pallas_call(
        _spec_ctx_kernel,
        grid=(2, nct),
        in_specs=[
            _const_spec((CTX_N, 3 * HY_HID)),
            _const_spec((CTX_N, 8)),
            pl.BlockSpec((3 * HY_HID, LANES), lambda o, c: (0, o * nct + c)),
            pl.BlockSpec((3 * HY_HID, LANES), lambda o, c: (0, 2 * nct + o * nct + c)),
            pl.BlockSpec((1, LANES), lambda o, c: (0, c)),
            _const_spec((2 * CTX_N, CTX_N)),
        ],
        out_specs=pl.BlockSpec((None, None, 2 * CTX_N, LANES), lambda o, c: (o, c, 0, 0)),
        out_shape=jax.ShapeDtypeStruct((2, nct, 2 * CTX_N, LANES), F32),
        compiler_params=_cparams(("arbitrary", "arbitrary")),
        name="spec_ctx",
    )(h2, rc, w3, w3, delta, cst["cwt"])


def _hy_ctx_kernel(va, vb, x1a, x1b, x2a, x2b, h_ref, bias_ref, wf_ref, wi_ref, o_ref):
    def pair(ra, rb):
        return jnp.concatenate([ra[...], rb[...]], axis=0).astype(F32)

    z = pair(va, vb)
    gates = (pair(x1a, x1b), pair(x2a, x2b))
    for o in range(2):
        y = _dot(wf_ref[...], z.astype(BF16))
        hb = h_ref[o]
        yr, yi = y[:CTX_N], y[CTX_N:]
        hr, hi = hb[:CTX_N], hb[CTX_N:]
        prod = jnp.concatenate([yr * hr - yi * hi, yr * hi + yi * hr], axis=0)
        conv = _dot(wi_ref[...], prod.astype(BF16))
        z = gates[o] * (conv + bias_ref[o] * z)
    o_ref[0] = z[:CTX].astype(o_ref.dtype)
    o_ref[1] = z[CTX:].astype(o_ref.dtype)


def _hy_ctx(hy, hspec, bias, cst):
    nct = HY_W // LANES
    base = R_LAT // CTX

    def blk(j, part):
        return pl.BlockSpec((CTX, LANES), lambda c, p: (base + 2 * p + j, part * nct + c))

    consts = [cst["cwf"], cst["cwi"]]
    return pl.pallas_call(
        _hy_ctx_kernel,
        grid=(nct, NB // 2),
        in_specs=[blk(0, 0), blk(1, 0), blk(0, 1), blk(1, 1), blk(0, 2), blk(1, 2),
                  pl.BlockSpec((2, None, 2 * CTX_N, LANES), lambda c, p: (0, c, 0, 0)),
                  pl.BlockSpec((2, 1, LANES), lambda c, p: (0, 0, c))]
        + [_const_spec(x.shape) for x in consts],
        out_specs=pl.BlockSpec((2, CTX, LANES), lambda c, p: (p, 0, c)),
        out_shape=jax.ShapeDtypeStruct((NB, CTX, HY_W), BF16),
        compiler_params=_cparams(("arbitrary", "arbitrary")),
        name="hy_ctx",
    )(hy, hy, hy, hy, hy, hy, hspec, bias, *consts)


def _stack(c):
    return np.block([[c.real, -c.imag], [c.imag, c.real]])


def _fft_constants():
    cst = {}
    k1 = np.arange(N1)[:, None]
    ca = np.exp(-2j * np.pi * k1 * np.arange(N1 // 2)[None, :] / N1)
    tw = np.exp(-2j * np.pi * k1 * np.arange(N2)[None, :] / FFT_N)
    ct = np.exp(-2j * np.pi * k1 * np.arange(N1)[None, :] / N1)
    cst["wa"] = jnp.asarray(np.stack([_stack(tw[:, n2:n2 + 1] * ca) for n2 in range(N2)]), F32).astype(BF16)
    wt = [tw[:, n2:n2 + 1] * ct for n2 in range(N2)]
    cst["wt"] = jnp.asarray(np.stack([np.concatenate([m.real, m.imag], axis=0) for m in wt]), F32).astype(BF16)
    e = np.exp(-2j * np.pi * np.arange(N2)[:, None] * np.arange(N2)[None, :] / N2)
    cst["e"] = jnp.asarray(_stack(e), F32).astype(BF16)
    cst["g"] = jnp.asarray(_stack(np.conj(e).T), F32).astype(BF16)
    bi = np.exp(2j * np.pi * np.arange(N1 // 2)[:, None] * np.arange(N1)[None, :] / N1) / FFT_N
    cst["mi"] = jnp.asarray(np.stack([_stack(bi * np.conj(tw[:, n2])[None, :]) for n2 in range(N2)]),
                            F32).astype(BF16)
    kk = np.arange(CTX_N)[:, None]
    cf = np.exp(-2j * np.pi * kk * np.arange(CTX)[None, :] / CTX_N)
    cst["cwf"] = jnp.asarray(_stack(cf), F32).astype(BF16)
    cfull = np.exp(-2j * np.pi * kk * np.arange(CTX_N)[None, :] / CTX_N)
    cst["cwt"] = jnp.asarray(np.concatenate([cfull.real, cfull.imag], axis=0), F32).astype(BF16)
    ci = np.exp(2j * np.pi * np.arange(CTX)[:, None] * np.arange(CTX_N)[None, :] / CTX_N) / CTX_N
    cst["cwi"] = jnp.asarray(_stack(ci), F32).astype(BF16)
    return cst


def _filter_positions(L, permute):
    bands = (HY_EMB - 1) // 2
    t = np.linspace(0.0, 1.0, L)[:, None]
    w = (2.0 * math.pi / L) * np.arange(L)[:, None]
    f = np.linspace(1e-4, bands - 1, bands)[None, :]
    z = np.concatenate([t, np.cos(f * w), -np.sin(f * w)], axis=-1)
    n = np.arange(2 * L)
    pos = np.where(n < L, n, 2 * L - n)
    pos = np.where(n == L, 0, pos)
    if permute:
        pos = pos[(np.arange(N1)[None, :] * N2 + np.arange(N2)[:, None]).reshape(-1)]
    feats = np.pad(z[pos], ((0, 0), (0, HY_HID - HY_EMB)))
    rc = np.pad(t[pos], ((0, 0), (0, 7)))
    return jnp.asarray(feats, F32), jnp.asarray(rc, F32)


def _decay_rates():
    max_decay = math.log(HY_TARGET) / HY_FAST
    min_decay = math.log(HY_TARGET) / HY_SLOW
    return jnp.asarray(np.abs(np.linspace(min_decay, max_decay, HY_W))[None, :], F32)


def _gla_constants():
    r = np.arange(GLA_BLK)
    same = (r[:, None] // CHUNK) == (r[None, :] // CHUNK)
    fwd = same & (r[:, None] >= r[None, :])
    bwd = same & (r[:, None] <= r[None, :])
    return jnp.asarray(np.stack([fwd, bwd]).astype(np.float32)).astype(BF16)


def kernel(x, c, ctx, c_ctx, ada_w, ada_b, ffn1_norm_g, ffn1_w_gate, ffn1_w_up, ffn1_w_down, mix_norm_g, w_in,
           gla_wa_f, gla_ba_f, gla_wa_b, gla_ba_b, gla_norm_g, w_gla_out, hy_conv_w, hy_conv_b, hy_f_w1, hy_f_b1,
           hy_f_freq1, hy_f_w2, hy_f_b2, hy_f_freq2, hy_f_w3, hy_bias, w_hy_out, w_out, ffn2_norm_g, ffn2_w_gate,
           ffn2_w_up, ffn2_w_down, final_norm_g):
    cst = _fft_constants()
    tri = _gla_constants()
    delta = _decay_rates()
    feats_lat, rc_lat = _filter_positions(SEQ, True)
    feats_ctx, rc_ctx = _filter_positions(CTX, False)

    cvec = jnp.concatenate([c, c_ctx[None, :], jnp.zeros((3, D), F32)], axis=0)
    mods = _mods(cvec, ada_w, ada_b).reshape(DEPTH, 8, 1, N_MOD * D)
    xs = (x.reshape(R_LAT, D), ctx.reshape(R_CTX, D))

    bf = lambda w: w.astype(BF16)
    ffn1_w = (bf(ffn1_w_gate), bf(ffn1_w_up), bf(ffn1_w_down))
    ffn2_w = (bf(ffn2_w_gate), bf(ffn2_w_up), bf(ffn2_w_down))
    out_w = (bf(w_gla_out), bf(w_hy_out), bf(w_out))
    w_t = jnp.swapaxes(w_in, 1, 2)
    wp = bf(w_t)
    wa_hi, wa_lo = _split(w_t[:, OFF_A:OFF_Q])
    conv_b = hy_conv_b[:, None, :]

    zpad = jnp.zeros((LOWRANK, KEY_W), F32)
    for i in range(DEPTH):
        last = i == DEPTH - 1
        xa = _ffn(xs, mods, i, 0, ffn1_norm_g[i], *ffn1_w, R_ALL)
        kvq, a, r, hy, gt = _proj(xa, mods, i, mix_norm_g[i], wp, wa_hi, wa_lo, hy_conv_w, conv_b)

        wa_cat = jnp.stack([jnp.concatenate([gla_wa_f[i], zpad], axis=0),
                            jnp.concatenate([zpad, gla_wa_b[i]], axis=0)])
        w_hi, w_lo = _split(wa_cat)
        ba_cat = jnp.stack([gla_ba_f[i], gla_ba_b[i]])[:, None, :]
        o2 = _gla(kvq, a, jnp.concatenate([w_hi, w_hi, w_lo], axis=1), ba_cat, tri)

        w1p = jnp.pad(hy_f_w1[i], ((0, HY_HID - HY_EMB), (0, 0)))
        mlp = (w1p, hy_f_b1[i], hy_f_freq1[i], hy_f_w2[i], hy_f_b2[i], hy_f_freq2[i])
        bias = hy_bias[i][:, None, :]
        w3s = _stack3_rhs(hy_f_w3[i])
        spec_lat = _spec_lat(_filt_mlp(feats_lat, *mlp), rc_lat, w3s, delta, cst)
        zs = (_hy_lat(hy, spec_lat, bias, cst).reshape(R_LAT, HY_W),)

        rows = R_LAT if last else R_ALL
        if not last:
            spec_ctx = _spec_ctx(_filt_mlp(feats_ctx, *mlp), rc_ctx, w3s, delta, cst)
            zs += (_hy_ctx(hy, spec_ctx, bias, cst).reshape(R_CTX, HY_W),)
        xa = _mix(xa, mods, i, o2, r, zs, gt, gla_norm_g[i], *out_w, rows)
        xs = (_ffn((xa,), mods, i, 2, ffn2_norm_g[i], *ffn2_w, rows, final_g=final_norm_g if last else None),)
    return xs[0].reshape(NB, SEQ, D)
```

```python
import functools
import math

import numpy as np
import jax
import jax.numpy as jnp
from jax import lax
from jax.experimental import pallas as pl
from jax.experimental.pallas import tpu as pltpu

F32 = jnp.float32
BF16 = jnp.bfloat16

D = 1024
NB = 4
SEQ = 4096
DEPTH = 2
CTX = 256
GRID_W = 64
D_FF = 2816
HALF_STEP = 0.5
N_MOD = 9
EPS = 1e-6

HEADS = 4
DK = 128
DV = 256
LOWRANK = 16
TAU = 16.0
CHUNK = 64
KEY_W = HEADS * DK
VAL_W = HEADS * DV

HY_W = 1024
HY_EMB = 33
HY_HID = 64
HY_FAST = 0.3
HY_SLOW = 1.5
HY_TARGET = 1e-2

OFF_K = 0
OFF_V = OFF_K + KEY_W
OFF_A = OFF_V + VAL_W
OFF_Q = OFF_A + 2 * LOWRANK
OFF_R = OFF_Q + KEY_W
OFF_H = OFF_R + VAL_W
OFF_G = OFF_H + 3 * HY_W
IN_W = OFF_G + 2 * D

R_LAT = NB * SEQ
R_CTX = NB * CTX
R_ALL = R_LAT + R_CTX

LANES = 128
VMEM_BYTES_V7X = 64 * 1024 * 1024
VMEM_LIMIT = VMEM_BYTES_V7X * 7 // 8

FFT_N = 2 * SEQ
N1 = 128
N2 = 64
PITCH_A = 264
PITCH_D = 136
PITCH_U = 72
PITCH_Z = 136
UNROLL = 8
CTX_N = 2 * CTX


def _cparams(sem):
    return pltpu.CompilerParams(dimension_semantics=sem, vmem_limit_bytes=VMEM_LIMIT)


def _const_spec(shape):
    nd = len(shape)
    return pl.BlockSpec(shape, lambda *_: (0,) * nd, pipeline_mode=pl.Buffered(1))


def _layer_spec(shape, layer):
    nd = len(shape)
    return pl.BlockSpec((None,) + tuple(shape), lambda *_: (layer,) + (0,) * nd, pipeline_mode=pl.Buffered(1))


def _split(x):
    hi = x.astype(BF16)
    lo = (x - hi.astype(F32)).astype(BF16)
    return hi, lo


def _dot(a, b):
    return jnp.dot(a, b, preferred_element_type=F32)


def _sigmoid(x):
    return 1.0 / (1.0 + jnp.exp(-x))


def _rms(x, g):
    ms = jnp.mean(x * x, axis=-1, keepdims=True)
    return x * lax.rsqrt(ms + EPS) * g


def _mods_kernel(c_ref, w_ref, b_ref, o_ref):
    c = c_ref[...]
    s = (c * _sigmoid(c)).astype(BF16)
    o_ref[...] = _dot(s, w_ref[...].astype(BF16)) + b_ref[...]


def _mods(cvec, ada_w, ada_b):
    tn = 1024
    return pl.pallas_call(
        _mods_kernel,
        grid=(DEPTH, N_MOD * D // tn),
        in_specs=[
            pl.BlockSpec((8, D), lambda l, j: (0, 0)),
            pl.BlockSpec((None, D, tn), lambda l, j: (l, 0, j)),
            pl.BlockSpec((None, 1, tn), lambda l, j: (l, 0, j)),
        ],
        out_specs=pl.BlockSpec((None, 8, tn), lambda l, j: (l, 0, j)),
        out_shape=jax.ShapeDtypeStruct((DEPTH, 8, N_MOD * D), F32),
        compiler_params=_cparams(("arbitrary", "arbitrary")),
        name="mods",
    )(cvec, ada_w, ada_b.reshape(DEPTH, 1, N_MOD * D))


def _mod_row(t, tm):
    lat_tiles = R_LAT // tm
    per_batch = SEQ // tm
    return jnp.where(t < lat_tiles, t // per_batch, NB)


def _row_specs(tm, width, split):
    if not split:
        return [pl.BlockSpec((tm, width), lambda t: (t, 0))]
    lat_tiles = R_LAT // tm
    return [pl.BlockSpec((tm, width), lambda t: (jnp.minimum(t, lat_tiles - 1), 0)),
            pl.BlockSpec((tm, width), lambda t: (jnp.maximum(t - lat_tiles, 0), 0))]


def _row_tile(refs, tm):
    if len(refs) == 1:
        return refs[0][...]
    return jnp.where(pl.program_id(0) < R_LAT // tm, refs[0][...], refs[1][...])


def _ffn_kernel(*refs, final, nx, tm):
    xs, (m_ref, g_ref, wg_ref, wu_ref, wd_ref), rest = refs[:nx], refs[nx:nx + 5], refs[nx + 5:]
    if final:
        fg_ref, o_ref = rest
    else:
        (o_ref,) = rest
    x = _row_tile(xs, tm)
    m = m_ref[...]
    shift, scale, gate = m[:, :D], m[:, D:2 * D], m[:, 2 * D:]
    h = (_rms(x, g_ref[...]) * (1.0 + scale) + shift).astype(BF16)
    a = _dot(h, wg_ref[...])
    u = _dot(h, wu_ref[...])
    act = (a * _sigmoid(a) * u).astype(BF16)
    y = _dot(act, wd_ref[...])
    out = x + (HALF_STEP * gate) * y
    if final:
        out = _rms(out, fg_ref[...])
    o_ref[...] = out


def _ffn(xs, mods, layer, slot, norm_g, wg, wu, wd, rows, final_g=None):
    tm = 512
    final = final_g is not None
    in_specs = _row_specs(tm, D, len(xs) == 2) + [
        pl.BlockSpec((None, None, 1, 3 * D), lambda t: (layer, _mod_row(t, tm), 0, slot)),
        _const_spec((1, D)),
        _layer_spec((D, D_FF), layer),
        _layer_spec((D, D_FF), layer),
        _layer_spec((D_FF, D), layer),
    ]
    args = list(xs) + [mods, norm_g.reshape(1, D), wg, wu, wd]
    if final:
        in_specs.append(_const_spec((1, D)))
        args.append(final_g.reshape(1, D))
    return pl.pallas_call(
        functools.partial(_ffn_kernel, final=final, nx=len(xs), tm=tm),
        grid=(rows // tm,),
        in_specs=in_specs,
        out_specs=pl.BlockSpec((tm, D), lambda t: (t, 0)),
        out_shape=jax.ShapeDtypeStruct((rows, D), F32),
        compiler_params=_cparams(("arbitrary",)),
        name="ffn",
    )(*args)


def _proj_kernel(x_ref, m_ref, g_ref, wkv_ref, wq_ref, wahi_ref, walo_ref, wr_ref, wh_ref, wg_ref, cw_ref, cb_ref,
                 kvq_ref, a_ref, r_ref, hy_ref, gt_ref, *, tm):
    x = x_ref[...]
    m = m_ref[...]
    shift, scale = m[:, :D], m[:, D:2 * D]
    h = _rms(x, g_ref[...]) * (1.0 + scale) + shift
    hb = h.astype(BF16)
    period = jnp.where(pl.program_id(0) < R_LAT // tm, GRID_W, CTX)
    pos = lax.broadcasted_iota(jnp.int32, (tm, 1), 0) & (period - 1)
    first, last = pos == 0, pos == period - 1

    def hyena_part(part):
        cols = slice(part * HY_W, (part + 1) * HY_W)
        u = _dot_nt(hb, wh_ref[0, cols, :])
        prv = jnp.where(first, 0.0, pltpu.roll(u, 1, 0))
        nxt = jnp.where(last, 0.0, pltpu.roll(u, tm - 1, 0))
        conv = cw_ref[0:1, cols] * prv + cw_ref[1:2, cols] * u + cw_ref[2:3, cols] * nxt + cb_ref[:, cols]
        hy_ref[:, cols] = conv.astype(hy_ref.dtype)

    hyena_part(0)
    kvq_ref[:, :KEY_W + VAL_W] = _dot_nt(hb, wkv_ref[0]).astype(kvq_ref.dtype)
    kvq_ref[:, KEY_W + VAL_W:] = _dot_nt(hb, wq_ref[0]).astype(kvq_ref.dtype)
    hyena_part(1)
    gt_ref[...] = _dot_nt(hb, wg_ref[0]).astype(gt_ref.dtype)
    hyena_part(2)
    r_ref[...] = _dot_nt(hb, wr_ref[0]).astype(r_ref.dtype)
    hl = (h - hb.astype(F32)).astype(BF16)
    a_ref[...] = _dot_nt(hb, wahi_ref[...]) + _dot_nt(hl, wahi_ref[...]) + _dot_nt(hb, walo_ref[...])


def _proj(x, mods, layer, norm_g, wp, wa_hi, wa_lo, conv_w, conv_b):
    tm = 512
    nk = 2 * KEY_W + VAL_W
    na = 2 * LOWRANK
    row = lambda w: pl.BlockSpec((tm, w), lambda t: (t, 0))

    def wrows(start, stop):
        return pl.BlockSpec((pl.Element(1), pl.Element(stop - start), pl.Element(D)),
                            lambda t: (layer, start, 0), pipeline_mode=pl.Buffered(1))

    def wcols(width, blk):
        return pl.BlockSpec((None, width, D), lambda t: (layer, blk, 0), pipeline_mode=pl.Buffered(1))

    return pl.pallas_call(
        functools.partial(_proj_kernel, tm=tm),
        grid=(R_ALL // tm,),
        in_specs=[
            row(D),
            pl.BlockSpec((None, None, 1, 3 * D), lambda t: (layer, _mod_row(t, tm), 0, 1)),
            _const_spec((1, D)),
            wrows(OFF_K, OFF_A),
            wrows(OFF_Q, OFF_R),
            wcols(na, 0),
            wcols(na, 0),
            wrows(OFF_R, OFF_H),
            wrows(OFF_H, OFF_G),
            wrows(OFF_G, IN_W),
            pl.BlockSpec((None, 3, 3 * HY_W), lambda t: (layer, 0, 0)),
            pl.BlockSpec((None, 1, 3 * HY_W), lambda t: (layer, 0, 0)),
        ],
        out_specs=[row(nk), row(na), row(VAL_W), row(3 * HY_W), row(2 * D)],
        out_shape=[
            jax.ShapeDtypeStruct((R_ALL, nk), BF16),
            jax.ShapeDtypeStruct((R_ALL, na), F32),
            jax.ShapeDtypeStruct((R_ALL, VAL_W), BF16),
            jax.ShapeDtypeStruct((R_ALL, 3 * HY_W), BF16),
            jax.ShapeDtypeStruct((R_ALL, 2 * D), BF16),
        ],
        compiler_params=_cparams(("arbitrary",)),
        name="proj",
    )(x, mods, norm_g.reshape(1, D), wp, wp, wa_hi, wa_lo, wp, wp, wp, conv_w, conv_b)


GLA_BLK = 256
GLA_STEPS = 1 + SEQ // GLA_BLK


def _gla_row_block(b, d, s):
    lat = jnp.where(d == 0, s - 1, SEQ // GLA_BLK - s)
    return jnp.where(s == 0, R_LAT // GLA_BLK + b, b * (SEQ // GLA_BLK) + lat)


def _dot_nt(a, b):
    return lax.dot_general(a, b, (((1,), (1,)), ((), ())), preferred_element_type=F32)


def _dot_tn(a, b):
    return lax.dot_general(a, b, (((0,), (0,)), ((), ())), preferred_element_type=F32)


def _gla_body(v_ref, k0_ref, q0_ref, a0_ref, kn_ref, qn_ref, an_ref, wa_ref, ba_ref, tri_ref, o_ref,
              st_ref, qd0, qd1, kn0, kn1, ke0, ke1, bc0, bc1):
    qd_s, kn_s, ke_s, bc_s = (qd0, qd1), (kn0, kn1), (ke0, ke1), (bc0, bc1)
    d = pl.program_id(1)
    s = pl.program_id(2)
    nchunk = GLA_BLK // CHUNK
    tot_row = jnp.where(d == 0, CHUNK - 1, 0)

    def log_gates(a_blk):
        a_hi, a_lo = _split(a_blk)
        pre = _dot(jnp.concatenate([a_hi, a_lo, a_hi], axis=1), wa_ref[...]) + ba_ref[...]
        la = (jnp.minimum(pre, 0.0) - jnp.log(1.0 + jnp.exp(-jnp.abs(pre)))) * (1.0 / TAU)
        return _split(la)

    def cum_decay(slot, la_hi, la_lo):
        tri = tri_ref[...]
        bcum = _dot(tri, la_hi) + _dot(tri, la_lo)
        bc_s[slot][...] = bcum
        return bcum

    def decay_weight(slot, bcum, k_blk, q_blk):
        btot = jnp.concatenate(
            [jnp.broadcast_to(bc_s[slot][pl.ds(j * CHUNK + tot_row, 1), :], (CHUNK, KEY_W)) for j in range(nchunk)],
            axis=0)
        k = k_blk.astype(F32)
        q = q_blk.astype(F32)
        qd_s[slot][...] = (q * (DK ** -0.5) * jnp.exp(bcum)).astype(BF16)
        kn_s[slot][...] = (k * jnp.exp(-bcum)).astype(BF16)
        ke_s[slot][...] = (k * jnp.exp(btot - bcum)).astype(BF16)

    @pl.when(s == 0)
    def _():
        st_ref[...] = jnp.zeros_like(st_ref)
        decay_weight(0, cum_decay(0, *log_gates(a0_ref[...])), k0_ref[...], q0_ref[...])

    def scan_block(cur):
        seen = tri_ref[...].astype(F32) > 0.5

        def intra(h):
            ksl = slice(h * DK, (h + 1) * DK)
            att = jnp.where(seen, _dot_nt(qd_s[cur][:, ksl], kn_s[cur][:, ksl]), 0.0).astype(BF16)
            o_ref[:, h * DV:(h + 1) * DV] = _dot(att, v_ref[:, h * DV:(h + 1) * DV])

        def inter(j):
            r0 = pl.multiple_of(jnp.where(d == 0, j, nchunk - 1 - j) * CHUNK, CHUNK)
            rows = pl.ds(r0, CHUNK)
            dec = jnp.exp(bc_s[cur][pl.ds(r0 + tot_row, 1), :])
            for h in range(HEADS):
                ksl = slice(h * DK, (h + 1) * DK)
                vsl = slice(h * DV, (h + 1) * DV)
                st = st_ref[h]
                o_ref[rows, vsl] += _dot_nt(qd_s[cur][rows, ksl], st.astype(BF16))
                upd = _dot_tn(v_ref[rows, vsl], ke_s[cur][rows, ksl])
                st_ref[h] = st * dec[:, ksl] + upd

        intra(0)
        la_hi, la_lo = log_gates(an_ref[...])
        for h in range(1, HEADS):
            intra(h)
        bcum = cum_decay(1 - cur, la_hi, la_lo)
        inter(0)
        decay_weight(1 - cur, bcum, kn_ref[...], qn_ref[...])
        for j in range(1, nchunk):
            inter(j)

    for parity in range(2):
        pl.when(s % 2 == parity)(functools.partial(scan_block, parity))


def _gla(kvq, a, wa_cat, ba_cat, tri):
    qblk = (KEY_W + VAL_W) // KEY_W
    first = lambda b, d, s: _gla_row_block(b, d, 0)
    nxt = lambda b, d, s: _gla_row_block(b, d, jnp.minimum(s + 1, GLA_STEPS - 1))

    def kqa(blk):
        return [pl.BlockSpec((GLA_BLK, KEY_W), lambda b, d, s: (blk(b, d, s), 0)),
                pl.BlockSpec((GLA_BLK, KEY_W), lambda b, d, s: (blk(b, d, s), qblk)),
                pl.BlockSpec((GLA_BLK, 2 * LOWRANK), lambda b, d, s: (blk(b, d, s), 0))]

    return pl.pallas_call(
        _gla_body,
        grid=(NB, 2, GLA_STEPS),
        in_specs=[
            pl.BlockSpec((pl.Element(GLA_BLK), pl.Element(VAL_W)),
                         lambda b, d, s: (_gla_row_block(b, d, s) * GLA_BLK, KEY_W)),
        ] + kqa(first) + kqa(nxt) + [
            pl.BlockSpec((None, 6 * LOWRANK, KEY_W), lambda b, d, s: (d, 0, 0)),
            pl.BlockSpec((None, 1, KEY_W), lambda b, d, s: (d, 0, 0)),
            pl.BlockSpec((None, GLA_BLK, GLA_BLK), lambda b, d, s: (d, 0, 0)),
        ],
        out_specs=pl.BlockSpec((None, GLA_BLK, VAL_W), lambda b, d, s: (d, _gla_row_block(b, d, s), 0)),
        out_shape=jax.ShapeDtypeStruct((2, R_ALL, VAL_W), F32),
        scratch_shapes=[
            pltpu.VMEM((HEADS, DV, DK), F32),
        ] + [pltpu.VMEM((GLA_BLK, KEY_W), BF16)] * 6 + [pltpu.VMEM((GLA_BLK, KEY_W), F32)] * 2,
        compiler_params=_cparams(("arbitrary", "arbitrary", "arbitrary")),
        name="gla",
    )(kvq, kvq, kvq, a, kvq, kvq, a, wa_cat, ba_cat, tri)


def _mix_kernel(x_ref, m_ref, of_ref, ob_ref, r_ref, gt_ref, ng_ref, wgo_ref, who_ref, wo_ref, *rest, tm):
    zs, o_ref = rest[:-1], rest[-1]
    x = x_ref[...]
    gate = m_ref[...][:, 2 * D:]
    o = of_ref[...] + ob_ref[...]
    ng = ng_ref[...]
    parts = []
    for h in range(HEADS):
        sl = slice(h * DV, (h + 1) * DV)
        oh = o[:, sl]
        ms = jnp.mean(oh * oh, axis=-1, keepdims=True)
        parts.append(oh * lax.rsqrt(ms + EPS) * ng[:, sl])
    on = jnp.concatenate(parts, axis=-1)
    r = r_ref[...].astype(F32)
    y_gla = _dot((on * (r * _sigmoid(r))).astype(BF16), wgo_ref[...])
    y_hy = _dot(_row_tile(zs, tm), who_ref[...])
    gt = gt_ref[...].astype(F32)
    mix = _sigmoid(gt[:, :D]) * y_gla + _sigmoid(gt[:, D:]) * y_hy
    o_ref[...] = x + gate * _dot(mix.astype(BF16), wo_ref[...])


def _mix(x, mods, layer, o2, r, zs, gt, norm_g, wgo, who, wo, rows):
    tm = 512
    row = lambda w: pl.BlockSpec((tm, w), lambda t: (t, 0))
    return pl.pallas_call(
        functools.partial(_mix_kernel, tm=tm),
        grid=(rows // tm,),
        in_specs=[
            row(D),
            pl.BlockSpec((None, None, 1, 3 * D), lambda t: (layer, _mod_row(t, tm), 0, 1)),
            pl.BlockSpec((None, tm, VAL_W), lambda t: (0, t, 0)),
            pl.BlockSpec((None, tm, VAL_W), lambda t: (1, t, 0)),
            row(VAL_W),
            row(2 * D),
            _const_spec((1, VAL_W)),
            _layer_spec((VAL_W, D), layer),
            _layer_spec((HY_W, D), layer),
            _layer_spec((D, D), layer),
        ] + _row_specs(tm, HY_W, len(zs) == 2),
        out_specs=row(D),
        out_shape=jax.ShapeDtypeStruct((rows, D), F32),
        compiler_params=_cparams(("arbitrary",)),
        name="mix",
    )(x, mods, o2, o2, r, gt, norm_g.reshape(1, VAL_W), wgo, who, wo, *zs)


def _stack3_lhs(a):
    hi, lo = _split(a)
    return jnp.concatenate([hi, lo, hi], axis=1)


def _stack3_rhs(w):
    hi, lo = _split(w)
    return jnp.concatenate([hi, hi, lo], axis=0)


def _filt_mlp_kernel(z_ref, w1_ref, b1_ref, f1_ref, w2_ref, b2_ref, f2_ref, o_ref):
    h = jnp.sin(f1_ref[...] * (_dot(_stack3_lhs(z_ref[...]), w1_ref[...]) + b1_ref[...]))
    h = jnp.sin(f2_ref[...] * (_dot(_stack3_lhs(h), w2_ref[...]) + b2_ref[...]))
    o_ref[...] = _stack3_lhs(h)


def _filt_mlp(feats, w1, b1, f1, w2, b2, f2):
    rows = feats.shape[0]
    tm = 512
    vec = lambda: _const_spec((1, HY_HID))
    return pl.pallas_call(
        _filt_mlp_kernel,
        grid=(rows // tm,),
        in_specs=[pl.BlockSpec((tm, HY_HID), lambda t: (t, 0)), _const_spec((3 * HY_HID, HY_HID)), vec(), vec(),
                  _const_spec((3 * HY_HID, HY_HID)), vec(), vec()],
        out_specs=pl.BlockSpec((tm, 3 * HY_HID), lambda t: (t, 0)),
        out_shape=jax.ShapeDtypeStruct((rows, 3 * HY_HID), BF16),
        compiler_params=_cparams(("arbitrary",)),
        name="filt_mlp",
    )(feats, _stack3_rhs(w1), b1.reshape(1, HY_HID), f1.reshape(1, HY_HID), _stack3_rhs(w2),
      b2.reshape(1, HY_HID), f2.reshape(1, HY_HID))


def _taps_chunk(h2s, t, w3f, w3b, delta, fwd, zero):
    win = jnp.exp(-t * delta)
    h3 = _dot(h2s, jnp.concatenate([w3f, w3b], axis=1))
    return jnp.where(zero, 0.0, win * jnp.where(fwd, h3[:, :LANES], h3[:, LANES:]))


def _stage_b_fwd(buf_a, j, e_ref):
    cols = []
    for k1 in (2 * j, 2 * j + 1):
        re = buf_a[pl.ds(k1, N2, stride=PITCH_A), :]
        im = buf_a[pl.ds(N1 + k1, N2, stride=PITCH_A), :]
        cols.append(jnp.concatenate([re, im], axis=0).astype(BF16))
    return _dot(e_ref[...], jnp.concatenate(cols, axis=1))


def _spec_lat_kernel(h2_ref, rc_ref, w3f_ref, w3b_ref, dl_ref, wt_ref, e_ref, o_ref, taps, buf_a):
    tb = 512
    delta = dl_ref[...]

    ridx = lax.broadcasted_iota(jnp.int32, (tb, LANES), 0)
    fwd = (ridx & (N1 - 1)) < N1 // 2

    def fill(i, acc):
        rows = pl.ds(pl.multiple_of(i * tb, tb), tb)
        zero = ridx == jnp.where(i == 0, N1 // 2, -1)
        tp = _taps_chunk(h2_ref[rows, :], rc_ref[rows, 0:1], w3f_ref[...], w3b_ref[...], delta, fwd, zero)
        taps[rows, :] = tp
        return acc + jnp.sum(jnp.abs(tp), axis=0, keepdims=True)

    l1 = lax.fori_loop(0, FFT_N // tb, fill, jnp.zeros((1, LANES), F32), unroll=4)
    inv = 1.0 / (l1 + EPS)

    def stage_a(n2, c):
        rhs = taps[pl.ds(pl.multiple_of(n2 * N1, N1), N1), :]
        buf_a[pl.ds(pl.multiple_of(n2 * PITCH_A, 8), 2 * N1), :] = _dot(wt_ref[n2], rhs.astype(BF16))
        return c

    lax.fori_loop(0, N2, stage_a, 0, unroll=UNROLL)

    def stage_b(j, c):
        y = _stage_b_fwd(buf_a, j, e_ref)
        rows = 2 * N2
        o_ref[pl.ds(pl.multiple_of(2 * j * rows, 2 * rows), rows), :] = (y[:, :LANES] * inv).astype(o_ref.dtype)
        o_ref[pl.ds(pl.multiple_of(2 * j * rows + rows, rows), rows), :] = (y[:, LANES:] * inv).astype(o_ref.dtype)
        return c

    lax.fori_loop(0, N1 // 2, stage_b, 0, unroll=2 * UNROLL)


def _spec_lat(h2p, rcp, w3, delta, cst):
    nct = HY_W // LANES
    return pl.pallas_call(
        _spec_lat_kernel,
        grid=(2, nct),
        in_specs=[
            _const_spec((FFT_N, 3 * HY_HID)),
            _const_spec((FFT_N, 8)),
            pl.BlockSpec((3 * HY_HID, LANES), lambda o, c: (0, o * nct + c)),
            pl.BlockSpec((3 * HY_HID, LANES), lambda o, c: (0, 2 * nct + o * nct + c)),
            pl.BlockSpec((1, LANES), lambda o, c: (0, c)),
            _const_spec((N2, 2 * N1, N1)),
            _const_spec((2 * N2, 2 * N2)),
        ],
        out_specs=pl.BlockSpec((None, None, 2 * FFT_N, LANES), lambda o, c: (o, c, 0, 0)),
        out_shape=jax.ShapeDtypeStruct((2, nct, 2 * FFT_N, LANES), BF16),
        scratch_shapes=[pltpu.VMEM((FFT_N, LANES), F32), pltpu.VMEM((N2 * PITCH_A, LANES), F32)],
        compiler_params=_cparams(("arbitrary", "arbitrary")),
        name="spec_lat",
    )(h2p, rcp, w3, w3, delta, cst["wt"], cst["e"])


def _hy_lat_kernel(ua_ref, ub_ref, h_ref, bias_ref, wa_ref, e_ref, g_ref, mi_ref,
                   o_ref, zin, buf_a, buf_d, upad):
    s = pl.program_id(2)

    def zslab(n2):
        return zin[pl.ds(pl.multiple_of(n2 * PITCH_Z, 8), 2 * N2), :]

    def repitch(n1, c):
        src = pl.ds(pl.multiple_of(n1 * GRID_W, GRID_W), GRID_W)
        dst = pl.ds(pl.multiple_of(n1 * PITCH_U, 8), GRID_W)
        upad[0, dst, :] = ua_ref[src, :].astype(F32)
        upad[1, dst, :] = ub_ref[src, :].astype(F32)
        return c
    lax.fori_loop(0, N2, repitch, 0, unroll=UNROLL)

    def col_slices(n2):
        return jnp.concatenate([upad[j, pl.ds(n2, N2, stride=PITCH_U), :] for j in range(2)], axis=0)

    def signal_in(n2, z):
        zin[pl.ds(pl.multiple_of(n2 * PITCH_Z, 8), 2 * N2), :] = z
        buf_a[pl.ds(pl.multiple_of(n2 * PITCH_A, 8), 2 * N1), :] = _dot(wa_ref[n2], z.astype(BF16))

    @pl.when(s == 0)
    def _():
        def body(n2, c):
            signal_in(n2, col_slices(n2))
            return c
        lax.fori_loop(0, N2, body, 0, unroll=2 * UNROLL)

    @pl.when(s > 0)
    def _():
        def stage_b(j, c):
            y = _stage_b_fwd(buf_a, j, e_ref)
            prods = []
            for i in range(2):
                yk = y[:, i * LANES:(i + 1) * LANES]
                hb = h_ref[pl.ds(pl.multiple_of((2 * j + i) * 2 * N2, 2 * N2), 2 * N2), :].astype(F32)
                yr, yi = yk[:N2], yk[N2:]
                hr, hi = hb[:N2], hb[N2:]
                prods.append(jnp.concatenate([yr * hr - yi * hi, yr * hi + yi * hr], axis=0).astype(BF16))
            d = _dot(g_ref[...], jnp.concatenate(prods, axis=1))
            for i in range(2):
                rows = pl.ds(pl.multiple_of((2 * j + i) * PITCH_D, 8), 2 * N2)
                buf_d[rows, :] = d[:, i * LANES:(i + 1) * LANES]
            return c
        lax.fori_loop(0, N1 // 2, stage_b, 0, unroll=2 * UNROLL)

    def gated_conv(n2):
        re = buf_d[pl.ds(n2, N1, stride=PITCH_D), :]
        im = buf_d[pl.ds(N2 + n2, N1, stride=PITCH_D), :]
        x = jnp.concatenate([re, im], axis=0).astype(BF16)
        y = _dot(mi_ref[n2], x)
        return col_slices(n2) * (y + bias_ref[...] * zslab(n2))

    @pl.when(s == 1)
    def _():
        def body(n2, c):
            signal_in(n2, gated_conv(n2))
            return c
        lax.fori_loop(0, N2, body, 0, unroll=2 * UNROLL)

    @pl.when(s == 2)
    def _():
        def body(n2, c):
            zin[pl.ds(pl.multiple_of(n2 * PITCH_Z, 8), 2 * N2), :] = gated_conv(n2)
            return c
        lax.fori_loop(0, N2, body, 0, unroll=2 * UNROLL)

        def emit(n1, c):
            dst = pl.ds(pl.multiple_of(n1 * GRID_W, GRID_W), GRID_W)
            o_ref[0, dst, :] = zin[pl.ds(n1, GRID_W, stride=PITCH_Z), :].astype(o_ref.dtype)
            o_ref[1, dst, :] = zin[pl.ds(N2 + n1, GRID_W, stride=PITCH_Z), :].astype(o_ref.dtype)
            return c
        lax.fori_loop(0, N2, emit, 0, unroll=UNROLL)


def _hy_lat(hy, hspec, bias, cst):
    nct = HY_W // LANES
    part = lambda c, p, s: s * nct + c
    order = lambda s: jnp.maximum(s - 1, 0)
    consts = [cst[n] for n in ("wa", "e", "g", "mi")]
    return pl.pallas_call(
        _hy_lat_kernel,
        grid=(nct, NB // 2, 3),
        in_specs=[
            pl.BlockSpec((SEQ, LANES), lambda c, p, s: (2 * p, part(c, p, s))),
            pl.BlockSpec((SEQ, LANES), lambda c, p, s: (2 * p + 1, part(c, p, s))),
            pl.BlockSpec((None, None, 2 * FFT_N, LANES), lambda c, p, s: (order(s), c, 0, 0)),
            pl.BlockSpec((None, 1, LANES), lambda c, p, s: (order(s), 0, c)),
        ] + [_const_spec(x.shape) for x in consts],
        out_specs=pl.BlockSpec((2, SEQ, LANES), lambda c, p, s: (p, 0, c)),
        out_shape=jax.ShapeDtypeStruct((NB, SEQ, HY_W), BF16),
        scratch_shapes=[
            pltpu.VMEM((N2 * PITCH_Z, LANES), F32),
            pltpu.VMEM((N2 * PITCH_A, LANES), F32),
            pltpu.VMEM((N1 * PITCH_D, LANES), F32),
            pltpu.VMEM((2, N2 * PITCH_U, LANES), F32),
        ],
        compiler_params=_cparams(("arbitrary", "arbitrary", "arbitrary")),
        name="hy_lat",
    )(hy, hy, hspec, bias, *consts)


def _spec_ctx_kernel(h2_ref, rc_ref, w3f_ref, w3b_ref, dl_ref, wt_ref, o_ref):
    ridx = lax.broadcasted_iota(jnp.int32, (CTX_N, LANES), 0)
    tp = _taps_chunk(h2_ref[...], rc_ref[:, 0:1], w3f_ref[...], w3b_ref[...], dl_ref[...], ridx < CTX, ridx == CTX)
    l1 = jnp.sum(jnp.abs(tp), axis=0, keepdims=True)
    o_ref[...] = _dot(wt_ref[...], (tp * (1.0 / (l1 + EPS))).astype(BF16))


def _spec_ctx(h2, rc, w3, delta, cst):
    nct = HY_W // LANES
    return pl.pallas_call(
        _spec_ctx_kernel,
        grid=(2, nct),
        in_specs=[
            _const_spec((CTX_N, 3 * HY_HID)),
            _const_spec((CTX_N, 8)),
            pl.BlockSpec((3 * HY_HID, LANES), lambda o, c: (0, o * nct + c)),
            pl.BlockSpec((3 * HY_HID, LANES), lambda o, c: (0, 2 * nct + o * nct + c)),
            pl.BlockSpec((1, LANES), lambda o, c: (0, c)),
            _const_spec((2 * CTX_N, CTX_N)),
        ],
        out_specs=pl.BlockSpec((None, None, 2 * CTX_N, LANES), lambda o, c: (o, c, 0, 0)),
        out_shape=jax.ShapeDtypeStruct((2, nct, 2 * CTX_N, LANES), F32),
        compiler_params=_cparams(("arbitrary", "arbitrary")),
        name="spec_ctx",
    )(h2, rc, w3, w3, delta, cst["cwt"])


def _hy_ctx_kernel(va, vb, x1a, x1b, x2a, x2b, h_ref, bias_ref, wf_ref, wi_ref, o_ref):
    def pair(ra, rb):
        return jnp.concatenate([ra[...], rb[...]], axis=0).astype(F32)

    z = pair(va, vb)
    gates = (pair(x1a, x1b), pair(x2a, x2b))
    for o in range(2):
        y = _dot(wf_ref[...], z.astype(BF16))
        hb = h_ref[o]
        yr, yi = y[:CTX_N], y[CTX_N:]
        hr, hi = hb[:CTX_N], hb[CTX_N:]
        prod = jnp.concatenate([yr * hr - yi * hi, yr * hi + yi * hr], axis=0)
        conv = _dot(wi_ref[...], prod.astype(BF16))
        z = gates[o] * (conv + bias_ref[o] * z)
    o_ref[0] = z[:CTX].astype(o_ref.dtype)
    o_ref[1] = z[CTX:].astype(o_ref.dtype)


def _hy_ctx(hy, hspec, bias, cst):
    nct = HY_W // LANES
    base = R_LAT // CTX

    def blk(j, part):
        return pl.BlockSpec((CTX, LANES), lambda c, p: (base + 2 * p + j, part * nct + c))

    consts = [cst["cwf"], cst["cwi"]]
    return pl.pallas_call(
        _hy_ctx_kernel,
        grid=(nct, NB // 2),
        in_specs=[blk(0, 0), blk(1, 0), blk(0, 1), blk(1, 1), blk(0, 2), blk(1, 2),
                  pl.BlockSpec((2, None, 2 * CTX_N, LANES), lambda c, p: (0, c, 0, 0)),
                  pl.BlockSpec((2, 1, LANES), lambda c, p: (0, 0, c))]
        + [_const_spec(x.shape) for x in consts],
        out_specs=pl.BlockSpec((2, CTX, LANES), lambda c, p: (p, 0, c)),
        out_shape=jax.ShapeDtypeStruct((NB, CTX, HY_W), BF16),
        compiler_params=_cparams(("arbitrary", "arbitrary")),
        name="hy_ctx",
    )(hy, hy, hy, hy, hy, hy, hspec, bias, *consts)


def _stack(c):
    return np.block([[c.real, -c.imag], [c.imag, c.real]])


def _fft_constants():
    cst = {}
    k1 = np.arange(N1)[:, None]
    ca = np.exp(-2j * np.pi * k1 * np.arange(N1 // 2)[None, :] / N1)
    tw = np.exp(-2j * np.pi * k1 * np.arange(N2)[None, :] / FFT_N)
    ct = np.exp(-2j * np.pi * k1 * np.arange(N1)[None, :] / N1)
    cst["wa"] = jnp.asarray(np.stack([_stack(tw[:, n2:n2 + 1] * ca) for n2 in range(N2)]), F32).astype(BF16)
    wt = [tw[:, n2:n2 + 1] * ct for n2 in range(N2)]
    cst["wt"] = jnp.asarray(np.stack([np.concatenate([m.real, m.imag], axis=0) for m in wt]), F32).astype(BF16)
    e = np.exp(-2j * np.pi * np.arange(N2)[:, None] * np.arange(N2)[None, :] / N2)
    cst["e"] = jnp.asarray(_stack(e), F32).astype(BF16)
    cst["g"] = jnp.asarray(_stack(np.conj(e).T), F32).astype(BF16)
    bi = np.exp(2j * np.pi * np.arange(N1 // 2)[:, None] * np.arange(N1)[None, :] / N1) / FFT_N
    cst["mi"] = jnp.asarray(np.stack([_stack(bi * np.conj(tw[:, n2])[None, :]) for n2 in range(N2)]),
                            F32).astype(BF16)
    kk = np.arange(CTX_N)[:, None]
    cf = np.exp(-2j * np.pi * kk * np.arange(CTX)[None, :] / CTX_N)
    cst["cwf"] = jnp.asarray(_stack(cf), F32).astype(BF16)
    cfull = np.exp(-2j * np.pi * kk * np.arange(CTX_N)[None, :] / CTX_N)
    cst["cwt"] = jnp.asarray(np.concatenate([cfull.real, cfull.imag], axis=0), F32).astype(BF16)
    ci = np.exp(2j * np.pi * np.arange(CTX)[:, None] * np.arange(CTX_N)[None, :] / CTX_N) / CTX_N
    cst["cwi"] = jnp.asarray(_stack(ci), F32).astype(BF16)
    return cst


def _filter_positions(L, permute):
    bands = (HY_EMB - 1) // 2
    t = np.linspace(0.0, 1.0, L)[:, None]
    w = (2.0 * math.pi / L) * np.arange(L)[:, None]
    f = np.linspace(1e-4, bands - 1, bands)[None, :]
    z = np.concatenate([t, np.cos(f * w), -np.sin(f * w)], axis=-1)
    n = np.arange(2 * L)
    pos = np.where(n < L, n, 2 * L - n)
    pos = np.where(n == L, 0, pos)
    if permute:
        pos = pos[(np.arange(N1)[None, :] * N2 + np.arange(N2)[:, None]).reshape(-1)]
    feats = np.pad(z[pos], ((0, 0), (0, HY_HID - HY_EMB)))
    rc = np.pad(t[pos], ((0, 0), (0, 7)))
    return jnp.asarray(feats, F32), jnp.asarray(rc, F32)


def _decay_rates():
    max_decay = math.log(HY_TARGET) / HY_FAST
    min_decay = math.log(HY_TARGET) / HY_SLOW
    return jnp.asarray(np.abs(np.linspace(min_decay, max_decay, HY_W))[None, :], F32)


def _gla_constants():
    r = np.arange(GLA_BLK)
    same = (r[:, None] // CHUNK) == (r[None, :] // CHUNK)
    fwd = same & (r[:, None] >= r[None, :])
    bwd = same & (r[:, None] <= r[None, :])
    return jnp.asarray(np.stack([fwd, bwd]).astype(np.float32)).astype(BF16)


def kernel(x, c, ctx, c_ctx, ada_w, ada_b, ffn1_norm_g, ffn1_w_gate, ffn1_w_up, ffn1_w_down, mix_norm_g, w_in,
           gla_wa_f, gla_ba_f, gla_wa_b, gla_ba_b, gla_norm_g, w_gla_out, hy_conv_w, hy_conv_b, hy_f_w1, hy_f_b1,
           hy_f_freq1, hy_f_w2, hy_f_b2, hy_f_freq2, hy_f_w3, hy_bias, w_hy_out, w_out, ffn2_norm_g, ffn2_w_gate,
           ffn2_w_up, ffn2_w_down, final_norm_g):
    cst = _fft_constants()
    tri = _gla_constants()
    delta = _decay_rates()
    feats_lat, rc_lat = _filter_positions(SEQ, True)
    feats_ctx, rc_ctx = _filter_positions(CTX, False)

    cvec = jnp.concatenate([c, c_ctx[None, :], jnp.zeros((3, D), F32)], axis=0)
    mods = _mods(cvec, ada_w, ada_b).reshape(DEPTH, 8, 1, N_MOD * D)
    xs = (x.reshape(R_LAT, D), ctx.reshape(R_CTX, D))

    bf = lambda w: w.astype(BF16)
    ffn1_w = (bf(ffn1_w_gate), bf(ffn1_w_up), bf(ffn1_w_down))
    ffn2_w = (bf(ffn2_w_gate), bf(ffn2_w_up), bf(ffn2_w_down))
    out_w = (bf(w_gla_out), bf(w_hy_out), bf(w_out))
    w_t = jnp.swapaxes(w_in, 1, 2)
    wp = bf(w_t)
    wa_hi, wa_lo = _split(w_t[:, OFF_A:OFF_Q])
    conv_b = hy_conv_b[:, None, :]

    zpad = jnp.zeros((LOWRANK, KEY_W), F32)
    for i in range(DEPTH):
        last = i == DEPTH - 1
        xa = _ffn(xs, mods, i, 0, ffn1_norm_g[i], *ffn1_w, R_ALL)
        kvq, a, r, hy, gt = _proj(xa, mods, i, mix_norm_g[i], wp, wa_hi, wa_lo, hy_conv_w, conv_b)

        wa_cat = jnp.stack([jnp.concatenate([gla_wa_f[i], zpad], axis=0),
                            jnp.concatenate([zpad, gla_wa_b[i]], axis=0)])
        w_hi, w_lo = _split(wa_cat)
        ba_cat = jnp.stack([gla_ba_f[i], gla_ba_b[i]])[:, None, :]
        o2 = _gla(kvq, a, jnp.concatenate([w_hi, w_hi, w_lo], axis=1), ba_cat, tri)

        w1p = jnp.pad(hy_f_w1[i], ((0, HY_HID - HY_EMB), (0, 0)))
        mlp = (w1p, hy_f_b1[i], hy_f_freq1[i], hy_f_w2[i], hy_f_b2[i], hy_f_freq2[i])
        bias = hy_bias[i][:, None, :]
        w3s = _stack3_rhs(hy_f_w3[i])
        spec_lat = _spec_lat(_filt_mlp(feats_lat, *mlp), rc_lat, w3s, delta, cst)
        zs = (_hy_lat(hy, spec_lat, bias, cst).reshape(R_LAT, HY_W),)

        rows = R_LAT if last else R_ALL
        if not last:
            spec_ctx = _spec_ctx(_filt_mlp(feats_ctx, *mlp), rc_ctx, w3s, delta, cst)
            zs += (_hy_ctx(hy, spec_ctx, bias, cst).reshape(R_CTX, HY_W),)
        xa = _mix(xa, mods, i, o2, r, zs, gt, gla_norm_g[i], *out_w, rows)
        xs = (_ffn((xa,), mods, i, 2, ffn2_norm_g[i], *ffn2_w, rows, final_g=final_norm_g if last else None),)
    return xs[0].reshape(NB, SEQ, D)
```

```python
import functools
import math

import numpy as np
import jax
import jax.numpy as jnp
from jax import lax
from jax.experimental import pallas as pl
from jax.experimental.pallas import tpu as pltpu

F32 = jnp.float32
BF16 = jnp.bfloat16

D = 1024
NB = 4
SEQ = 4096
DEPTH = 2
CTX = 256
GRID_W = 64
D_FF = 2816
HALF_STEP = 0.5
N_MOD = 9
EPS = 1e-6

HEADS = 4
DK = 128
DV = 256
LOWRANK = 16
TAU = 16.0
CHUNK = 64
KEY_W = HEADS * DK
VAL_W = HEADS * DV

HY_W = 1024
HY_EMB = 33
HY_HID = 64
HY_FAST = 0.3
HY_SLOW = 1.5
HY_TARGET = 1e-2

OFF_K = 0
OFF_V = OFF_K + KEY_W
OFF_A = OFF_V + VAL_W
OFF_Q = OFF_A + 2 * LOWRANK
OFF_R = OFF_Q + KEY_W
OFF_H = OFF_R + VAL_W
OFF_G = OFF_H + 3 * HY_W
IN_W = OFF_G + 2 * D

R_LAT = NB * SEQ
R_CTX = NB * CTX
R_ALL = R_LAT + R_CTX

LANES = 128
VMEM_BYTES_V7X = 64 * 1024 * 1024
VMEM_LIMIT = VMEM_BYTES_V7X * 7 // 8

FFT_N = 2 * SEQ
N1 = 128
N2 = 64
PITCH_A = 264
PITCH_D = 136
PITCH_U = 72
PITCH_Z = 136
UNROLL = 8
CTX_N = 2 * CTX


def _cparams(sem):
    return pltpu.CompilerParams(dimension_semantics=sem, vmem_limit_bytes=VMEM_LIMIT)


def _const_spec(shape):
    nd = len(shape)
    return pl.BlockSpec(shape, lambda *_: (0,) * nd, pipeline_mode=pl.Buffered(1))


def _layer_spec(shape, layer):
    nd = len(shape)
    return pl.BlockSpec((None,) + tuple(shape), lambda *_: (layer,) + (0,) * nd, pipeline_mode=pl.Buffered(1))


def _split(x):
    hi = x.astype(BF16)
    lo = (x - hi.astype(F32)).astype(BF16)
    return hi, lo


def _dot(a, b):
    return jnp.dot(a, b, preferred_element_type=F32)


def _sigmoid(x):
    return 1.0 / (1.0 + jnp.exp(-x))


def _rms(x, g):
    ms = jnp.mean(x * x, axis=-1, keepdims=True)
    return x * lax.rsqrt(ms + EPS) * g


def _mods_kernel(c_ref, w_ref, b_ref, o_ref):
    c = c_ref[...]
    s = (c * _sigmoid(c)).astype(BF16)
    o_ref[...] = _dot(s, w_ref[...].astype(BF16)) + b_ref[...]


def _mods(cvec, ada_w, ada_b):
    tn = 1024
    return pl.pallas_call(
        _mods_kernel,
        grid=(DEPTH, N_MOD * D // tn),
        in_specs=[
            pl.BlockSpec((8, D), lambda l, j: (0, 0)),
            pl.BlockSpec((None, D, tn), lambda l, j: (l, 0, j)),
            pl.BlockSpec((None, 1, tn), lambda l, j: (l, 0, j)),
        ],
        out_specs=pl.BlockSpec((None, 8, tn), lambda l, j: (l, 0, j)),
        out_shape=jax.ShapeDtypeStruct((DEPTH, 8, N_MOD * D), F32),
        compiler_params=_cparams(("arbitrary", "arbitrary")),
        name="mods",
    )(cvec, ada_w, ada_b.reshape(DEPTH, 1, N_MOD * D))


def _mod_row(t, tm):
    lat_tiles = R_LAT // tm
    per_batch = SEQ // tm
    return jnp.where(t < lat_tiles, t // per_batch, NB)


def _row_specs(tm, width, split):
    if not split:
        return [pl.BlockSpec((tm, width), lambda t: (t, 0))]
    lat_tiles = R_LAT // tm
    return [pl.BlockSpec((tm, width), lambda t: (jnp.minimum(t, lat_tiles - 1), 0)),
            pl.BlockSpec((tm, width), lambda t: (jnp.maximum(t - lat_tiles, 0), 0))]


def _row_tile(refs, tm):
    if len(refs) == 1:
        return refs[0][...]
    return jnp.where(pl.program_id(0) < R_LAT // tm, refs[0][...], refs[1][...])


def _ffn_kernel(*refs, final, nx, tm):
    xs, (m_ref, g_ref, wg_ref, wu_ref, wd_ref), rest = refs[:nx], refs[nx:nx + 5], refs[nx + 5:]
    if final:
        fg_ref, o_ref = rest
    else:
        (o_ref,) = rest
    x = _row_tile(xs, tm)
    m = m_ref[...]
    shift, scale, gate = m[:, :D], m[:, D:2 * D], m[:, 2 * D:]
    h = (_rms(x, g_ref[...]) * (1.0 + scale) + shift).astype(BF16)
    a = _dot(h, wg_ref[...])
    u = _dot(h, wu_ref[...])
    act = (a * _sigmoid(a) * u).astype(BF16)
    y = _dot(act, wd_ref[...])
    out = x + (HALF_STEP * gate) * y
    if final:
        out = _rms(out, fg_ref[...])
    o_ref[...] = out


def _ffn(xs, mods, layer, slot, norm_g, wg, wu, wd, rows, final_g=None):
    tm = 512
    final = final_g is not None
    in_specs = _row_specs(tm, D, len(xs) == 2) + [
        pl.BlockSpec((None, None, 1, 3 * D), lambda t: (layer, _mod_row(t, tm), 0, slot)),
        _const_spec((1, D)),
        _layer_spec((D, D_FF), layer),
        _layer_spec((D, D_FF), layer),
        _layer_spec((D_FF, D), layer),
    ]
    args = list(xs) + [mods, norm_g.reshape(1, D), wg, wu, wd]
    if final:
        in_specs.append(_const_spec((1, D)))
        args.append(final_g.reshape(1, D))
    return pl.pallas_call(
        functools.partial(_ffn_kernel, final=final, nx=len(xs), tm=tm),
        grid=(rows // tm,),
        in_specs=in_specs,
        out_specs=pl.BlockSpec((tm, D), lambda t: (t, 0)),
        out_shape=jax.ShapeDtypeStruct((rows, D), F32),
        compiler_params=_cparams(("arbitrary",)),
        name="ffn",
    )(*args)


def _proj_kernel(x_ref, m_ref, g_ref, wkv_ref, wq_ref, wahi_ref, walo_ref, wr_ref, wh_ref, wg_ref, cw_ref, cb_ref,
                 kvq_ref, a_ref, r_ref, hy_ref, gt_ref, *, tm):
    x = x_ref[...]
    m = m_ref[...]
    shift, scale = m[:, :D], m[:, D:2 * D]
    h = _rms(x, g_ref[...]) * (1.0 + scale) + shift
    hb = h.astype(BF16)
    period = jnp.where(pl.program_id(0) < R_LAT // tm, GRID_W, CTX)
    pos = lax.broadcasted_iota(jnp.int32, (tm, 1), 0) & (period - 1)
    first, last = pos == 0, pos == period - 1

    def hyena_part(part):
        cols = slice(part * HY_W, (part + 1) * HY_W)
        u = _dot_nt(hb, wh_ref[0, cols, :])
        prv = jnp.where(first, 0.0, pltpu.roll(u, 1, 0))
        nxt = jnp.where(last, 0.0, pltpu.roll(u, tm - 1, 0))
        conv = cw_ref[0:1, cols] * prv + cw_ref[1:2, cols] * u + cw_ref[2:3, cols] * nxt + cb_ref[:, cols]
        hy_ref[:, cols] = conv.astype(hy_ref.dtype)

    hyena_part(0)
    kvq_ref[:, :KEY_W + VAL_W] = _dot_nt(hb, wkv_ref[0]).astype(kvq_ref.dtype)
    kvq_ref[:, KEY_W + VAL_W:] = _dot_nt(hb, wq_ref[0]).astype(kvq_ref.dtype)
    hyena_part(1)
    gt_ref[...] = _dot_nt(hb, wg_ref[0]).astype(gt_ref.dtype)
    hyena_part(2)
    r_ref[...] = _dot_nt(hb, wr_ref[0]).astype(r_ref.dtype)
    hl = (h - hb.astype(F32)).astype(BF16)
    a_ref[...] = _dot_nt(hb, wahi_ref[...]) + _dot_nt(hl, wahi_ref[...]) + _dot_nt(hb, walo_ref[...])


def _proj(x, mods, layer, norm_g, wp, wa_hi, wa_lo, conv_w, conv_b):
    tm = 512
    nk = 2 * KEY_W + VAL_W
    na = 2 * LOWRANK
    row = lambda w: pl.BlockSpec((tm, w), lambda t: (t, 0))

    def wrows(start, stop):
        return pl.BlockSpec((pl.Element(1), pl.Element(stop - start), pl.Element(D)),
                            lambda t: (layer, start, 0), pipeline_mode=pl.Buffered(1))

    def wcols(width, blk):
        return pl.BlockSpec((None, width, D), lambda t: (layer, blk, 0), pipeline_mode=pl.Buffered(1))

    return pl.pallas_call(
        functools.partial(_proj_kernel, tm=tm),
        grid=(R_ALL // tm,),
        in_specs=[
            row(D),
            pl.BlockSpec((None, None, 1, 3 * D), lambda t: (layer, _mod_row(t, tm), 0, 1)),
            _const_spec((1, D)),
            wrows(OFF_K, OFF_A),
            wrows(OFF_Q, OFF_R),
            wcols(na, 0),
            wcols(na, 0),
            wrows(OFF_R, OFF_H),
            wrows(OFF_H, OFF_G),
            wrows(OFF_G, IN_W),
            pl.BlockSpec((None, 3, 3 * HY_W), lambda t: (layer, 0, 0)),
            pl.BlockSpec((None, 1, 3 * HY_W), lambda t: (layer, 0, 0)),
        ],
        out_specs=[row(nk), row(na), row(VAL_W), row(3 * HY_W), row(2 * D)],
        out_shape=[
            jax.ShapeDtypeStruct((R_ALL, nk), BF16),
            jax.ShapeDtypeStruct((R_ALL, na), F32),
            jax.ShapeDtypeStruct((R_ALL, VAL_W), BF16),
            jax.ShapeDtypeStruct((R_ALL, 3 * HY_W), BF16),
            jax.ShapeDtypeStruct((R_ALL, 2 * D), BF16),
        ],
        compiler_params=_cparams(("arbitrary",)),
        name="proj",
    )(x, mods, norm_g.reshape(1, D), wp, wp, wa_hi, wa_lo, wp, wp, wp, conv_w, conv_b)


GLA_BLK = 256
GLA_STEPS = 1 + SEQ // GLA_BLK


def _gla_row_block(b, d, s):
    lat = jnp.where(d == 0, s - 1, SEQ // GLA_BLK - s)
    return jnp.where(s == 0, R_LAT // GLA_BLK + b, b * (SEQ // GLA_BLK) + lat)


def _dot_nt(a, b):
    return lax.dot_general(a, b, (((1,), (1,)), ((), ())), preferred_element_type=F32)


def _dot_tn(a, b):
    return lax.dot_general(a, b, (((0,), (0,)), ((), ())), preferred_element_type=F32)


def _gla_body(v_ref, k0_ref, q0_ref, a0_ref, kn_ref, qn_ref, an_ref, wa_ref, ba_ref, tri_ref, o_ref,
              st_ref, qd0, qd1, kn0, kn1, ke0, ke1, bc0, bc1):
    qd_s, kn_s, ke_s, bc_s = (qd0, qd1), (kn0, kn1), (ke0, ke1), (bc0, bc1)
    d = pl.program_id(1)
    s = pl.program_id(2)
    nchunk = GLA_BLK // CHUNK
    tot_row = jnp.where(d == 0, CHUNK - 1, 0)

    def log_gates(a_blk):
        a_hi, a_lo = _split(a_blk)
        pre = _dot(jnp.concatenate([a_hi, a_lo, a_hi], axis=1), wa_ref[...]) + ba_ref[...]
        la = (jnp.minimum(pre, 0.0) - jnp.log(1.0 + jnp.exp(-jnp.abs(pre)))) * (1.0 / TAU)
        return _split(la)

    def cum_decay(slot, la_hi, la_lo):
        tri = tri_ref[...]
        bcum = _dot(tri, la_hi) + _dot(tri, la_lo)
        bc_s[slot][...] = bcum
        return bcum

    def decay_weight(slot, bcum, k_blk, q_blk):
        btot = jnp.concatenate(
            [jnp.broadcast_to(bc_s[slot][pl.ds(j * CHUNK + tot_row, 1), :], (CHUNK, KEY_W)) for j in range(nchunk)],
            axis=0)
        k = k_blk.astype(F32)
        q = q_blk.astype(F32)
        qd_s[slot][...] = (q * (DK ** -0.5) * jnp.exp(bcum)).astype(BF16)
        kn_s[slot][...] = (k * jnp.exp(-bcum)).astype(BF16)
        ke_s[slot][...] = (k * jnp.exp(btot - bcum)).astype(BF16)

    @pl.when(s == 0)
    def _():
        st_ref[...] = jnp.zeros_like(st_ref)
        decay_weight(0, cum_decay(0, *log_gates(a0_ref[...])), k0_ref[...], q0_ref[...])

    def scan_block(cur):
        seen = tri_ref[...].astype(F32) > 0.5

        def intra(h):
            ksl = slice(h * DK, (h + 1) * DK)
            att = jnp.where(seen, _dot_nt(qd_s[cur][:, ksl], kn_s[cur][:, ksl]), 0.0).astype(BF16)
            o_ref[:, h * DV:(h + 1) * DV] = _dot(att, v_ref[:, h * DV:(h + 1) * DV])

        def inter(j):
            r0 = pl.multiple_of(jnp.where(d == 0, j, nchunk - 1 - j) * CHUNK, CHUNK)
            rows = pl.ds(r0, CHUNK)
            dec = jnp.exp(bc_s[cur][pl.ds(r0 + tot_row, 1), :])
            for h in range(HEADS):
                ksl = slice(h * DK, (h + 1) * DK)
                vsl = slice(h * DV, (h + 1) * DV)
                st = st_ref[h]
                o_ref[rows, vsl] += _dot_nt(qd_s[cur][rows, ksl], st.astype(BF16))
                upd = _dot_tn(v_ref[rows, vsl], ke_s[cur][rows, ksl])
                st_ref[h] = st * dec[:, ksl] + upd

        intra(0)
        la_hi, la_lo = log_gates(an_ref[...])
        for h in range(1, HEADS):
            intra(h)
        bcum = cum_decay(1 - cur, la_hi, la_lo)
        inter(0)
        decay_weight(1 - cur, bcum, kn_ref[...], qn_ref[...])
        for j in range(1, nchunk):
            inter(j)

    for parity in range(2):
        pl.when(s % 2 == parity)(functools.partial(scan_block, parity))


def _gla(kvq, a, wa_cat, ba_cat, tri):
    qblk = (KEY_W + VAL_W) // KEY_W
    first = lambda b, d, s: _gla_row_block(b, d, 0)
    nxt = lambda b, d, s: _gla_row_block(b, d, jnp.minimum(s + 1, GLA_STEPS - 1))

    def kqa(blk):
        return [pl.BlockSpec((GLA_BLK, KEY_W), lambda b, d, s: (blk(b, d, s), 0)),
                pl.BlockSpec((GLA_BLK, KEY_W), lambda b, d, s: (blk(b, d, s), qblk)),
                pl.BlockSpec((GLA_BLK, 2 * LOWRANK), lambda b, d, s: (blk(b, d, s), 0))]

    return pl.pallas_call(
        _gla_body,
        grid=(NB, 2, GLA_STEPS),
        in_specs=[
            pl.BlockSpec((pl.Element(GLA_BLK), pl.Element(VAL_W)),
                         lambda b, d, s: (_gla_row_block(b, d, s) * GLA_BLK, KEY_W)),
        ] + kqa(first) + kqa(nxt) + [
            pl.BlockSpec((None, 6 * LOWRANK, KEY_W), lambda b, d, s: (d, 0, 0)),
            pl.BlockSpec((None, 1, KEY_W), lambda b, d, s: (d, 0, 0)),
            pl.BlockSpec((None, GLA_BLK, GLA_BLK), lambda b, d, s: (d, 0, 0)),
        ],
        out_specs=pl.BlockSpec((None, GLA_BLK, VAL_W), lambda b, d, s: (d, _gla_row_block(b, d, s), 0)),
        out_shape=jax.ShapeDtypeStruct((2, R_ALL, VAL_W), F32),
        scratch_shapes=[
            pltpu.VMEM((HEADS, DV, DK), F32),
        ] + [pltpu.VMEM((GLA_BLK, KEY_W), BF16)] * 6 + [pltpu.VMEM((GLA_BLK, KEY_W), F32)] * 2,
        compiler_params=_cparams(("arbitrary", "arbitrary", "arbitrary")),
        name="gla",
    )(kvq, kvq, kvq, a, kvq, kvq, a, wa_cat, ba_cat, tri)


def _mix_kernel(x_ref, m_ref, of_ref, ob_ref, r_ref, gt_ref, ng_ref, wgo_ref, who_ref, wo_ref, *rest, tm):
    zs, o_ref = rest[:-1], rest[-1]
    x = x_ref[...]
    gate = m_ref[...][:, 2 * D:]
    o = of_ref[...] + ob_ref[...]
    ng = ng_ref[...]
    parts = []
    for h in range(HEADS):
        sl = slice(h * DV, (h + 1) * DV)
        oh = o[:, sl]
        ms = jnp.mean(oh * oh, axis=-1, keepdims=True)
        parts.append(oh * lax.rsqrt(ms + EPS) * ng[:, sl])
    on = jnp.concatenate(parts, axis=-1)
    r = r_ref[...].astype(F32)
    y_gla = _dot((on * (r * _sigmoid(r))).astype(BF16), wgo_ref[...])
    y_hy = _dot(_row_tile(zs, tm), who_ref[...])
    gt = gt_ref[...].astype(F32)
    mix = _sigmoid(gt[:, :D]) * y_gla + _sigmoid(gt[:, D:]) * y_hy
    o_ref[...] = x + gate * _dot(mix.astype(BF16), wo_ref[...])


def _mix(x, mods, layer, o2, r, zs, gt, norm_g, wgo, who, wo, rows):
    tm = 512
    row = lambda w: pl.BlockSpec((tm, w), lambda t: (t, 0))
    return pl.pallas_call(
        functools.partial(_mix_kernel, tm=tm),
        grid=(rows // tm,),
        in_specs=[
            row(D),
            pl.BlockSpec((None, None, 1, 3 * D), lambda t: (layer, _mod_row(t, tm), 0, 1)),
            pl.BlockSpec((None, tm, VAL_W), lambda t: (0, t, 0)),
            pl.BlockSpec((None, tm, VAL_W), lambda t: (1, t, 0)),
            row(VAL_W),
            row(2 * D),
            _const_spec((1, VAL_W)),
            _layer_spec((VAL_W, D), layer),
            _layer_spec((HY_W, D), layer),
            _layer_spec((D, D), layer),
        ] + _row_specs(tm, HY_W, len(zs) == 2),
        out_specs=row(D),
        out_shape=jax.ShapeDtypeStruct((rows, D), F32),
        compiler_params=_cparams(("arbitrary",)),
        name="mix",
    )(x, mods, o2, o2, r, gt, norm_g.reshape(1, VAL_W), wgo, who, wo, *zs)


def _stack3_lhs(a):
    hi, lo = _split(a)
    return jnp.concatenate([hi, lo, hi], axis=1)


def _stack3_rhs(w):
    hi, lo = _split(w)
    return jnp.concatenate([hi, hi, lo], axis=0)


def _filt_mlp_kernel(z_ref, w1_ref, b1_ref, f1_ref, w2_ref, b2_ref, f2_ref, o_ref):
    h = jnp.sin(f1_ref[...] * (_dot(_stack3_lhs(z_ref[...]), w1_ref[...]) + b1_ref[...]))
    h = jnp.sin(f2_ref[...] * (_dot(_stack3_lhs(h), w2_ref[...]) + b2_ref[...]))
    o_ref[...] = _stack3_lhs(h)


def _filt_mlp(feats, w1, b1, f1, w2, b2, f2):
    rows = feats.shape[0]
    tm = 512
    vec = lambda: _const_spec((1, HY_HID))
    return pl.pallas_call(
        _filt_mlp_kernel,
        grid=(rows // tm,),
        in_specs=[pl.BlockSpec((tm, HY_HID), lambda t: (t, 0)), _const_spec((3 * HY_HID, HY_HID)), vec(), vec(),
                  _const_spec((3 * HY_HID, HY_HID)), vec(), vec()],
        out_specs=pl.BlockSpec((tm, 3 * HY_HID), lambda t: (t, 0)),
        out_shape=jax.ShapeDtypeStruct((rows, 3 * HY_HID), BF16),
        compiler_params=_cparams(("arbitrary",)),
        name="filt_mlp",
    )(feats, _stack3_rhs(w1), b1.reshape(1, HY_HID), f1.reshape(1, HY_HID), _stack3_rhs(w2),
      b2.reshape(1, HY_HID), f2.reshape(1, HY_HID))


def _taps_chunk(h2s, t, w3f, w3b, delta, fwd, zero):
    win = jnp.exp(-t * delta)
    h3 = _dot(h2s, jnp.concatenate([w3f, w3b], axis=1))
    return jnp.where(zero, 0.0, win * jnp.where(fwd, h3[:, :LANES], h3[:, LANES:]))


def _stage_b_fwd(buf_a, j, e_ref):
    cols = []
    for k1 in (2 * j, 2 * j + 1):
        re = buf_a[pl.ds(k1, N2, stride=PITCH_A), :]
        im = buf_a[pl.ds(N1 + k1, N2, stride=PITCH_A), :]
        cols.append(jnp.concatenate([re, im], axis=0).astype(BF16))
    return _dot(e_ref[...], jnp.concatenate(cols, axis=1))


def _spec_lat_kernel(h2_ref, rc_ref, w3f_ref, w3b_ref, dl_ref, wt_ref, e_ref, o_ref, taps, buf_a):
    tb = 512
    delta = dl_ref[...]

    ridx = lax.broadcasted_iota(jnp.int32, (tb, LANES), 0)
    fwd = (ridx & (N1 - 1)) < N1 // 2

    def fill(i, acc):
        rows = pl.ds(pl.multiple_of(i * tb, tb), tb)
        zero = ridx == jnp.where(i == 0, N1 // 2, -1)
        tp = _taps_chunk(h2_ref[rows, :], rc_ref[rows, 0:1], w3f_ref[...], w3b_ref[...], delta, fwd, zero)
        taps[rows, :] = tp
        return acc + jnp.sum(jnp.abs(tp), axis=0, keepdims=True)

    l1 = lax.fori_loop(0, FFT_N // tb, fill, jnp.zeros((1, LANES), F32), unroll=4)
    inv = 1.0 / (l1 + EPS)

    def stage_a(n2, c):
        rhs = taps[pl.ds(pl.multiple_of(n2 * N1, N1), N1), :]
        buf_a[pl.ds(pl.multiple_of(n2 * PITCH_A, 8), 2 * N1), :] = _dot(wt_ref[n2], rhs.astype(BF16))
        return c

    lax.fori_loop(0, N2, stage_a, 0, unroll=UNROLL)

    def stage_b(j, c):
        y = _stage_b_fwd(buf_a, j, e_ref)
        rows = 2 * N2
        o_ref[pl.ds(pl.multiple_of(2 * j * rows, 2 * rows), rows), :] = (y[:, :LANES] * inv).astype(o_ref.dtype)
        o_ref[pl.ds(pl.multiple_of(2 * j * rows + rows, rows), rows), :] = (y[:, LANES:] * inv).astype(o_ref.dtype)
        return c

    lax.fori_loop(0, N1 // 2, stage_b, 0, unroll=2 * UNROLL)


def _spec_lat(h2p, rcp, w3, delta, cst):
    nct = HY_W // LANES
    return pl.pallas_call(
        _spec_lat_kernel,
        grid=(2, nct),
        in_specs=[
            _const_spec((FFT_N, 3 * HY_HID)),
            _const_spec((FFT_N, 8)),
            pl.BlockSpec((3 * HY_HID, LANES), lambda o, c: (0, o * nct + c)),
            pl.BlockSpec((3 * HY_HID, LANES), lambda o, c: (0, 2 * nct + o * nct + c)),
            pl.BlockSpec((1, LANES), lambda o, c: (0, c)),
            _const_spec((N2, 2 * N1, N1)),
            _const_spec((2 * N2, 2 * N2)),
        ],
        out_specs=pl.BlockSpec((None, None, 2 * FFT_N, LANES), lambda o, c: (o, c, 0, 0)),
        out_shape=jax.ShapeDtypeStruct((2, nct, 2 * FFT_N, LANES), BF16),
        scratch_shapes=[pltpu.VMEM((FFT_N, LANES), F32), pltpu.VMEM((N2 * PITCH_A, LANES), F32)],
        compiler_params=_cparams(("arbitrary", "arbitrary")),
        name="spec_lat",
    )(h2p, rcp, w3, w3, delta, cst["wt"], cst["e"])


def _hy_lat_kernel(ua_ref, ub_ref, h_ref, bias_ref, wa_ref, e_ref, g_ref, mi_ref,
                   o_ref, zin, buf_a, buf_d, upad):
    s = pl.program_id(2)

    def zslab(n2):
        return zin[pl.ds(pl.multiple_of(n2 * PITCH_Z, 8), 2 * N2), :]

    def repitch(n1, c):
        src = pl.ds(pl.multiple_of(n1 * GRID_W, GRID_W), GRID_W)
        dst = pl.ds(pl.multiple_of(n1 * PITCH_U, 8), GRID_W)
        upad[0, dst, :] = ua_ref[src, :].astype(F32)
        upad[1, dst, :] = ub_ref[src, :].astype(F32)
        return c
    lax.fori_loop(0, N2, repitch, 0, unroll=UNROLL)

    def col_slices(n2):
        return jnp.concatenate([upad[j, pl.ds(n2, N2, stride=PITCH_U), :] for j in range(2)], axis=0)

    def signal_in(n2, z):
        zin[pl.ds(pl.multiple_of(n2 * PITCH_Z, 8), 2 * N2), :] = z
        buf_a[pl.ds(pl.multiple_of(n2 * PITCH_A, 8), 2 * N1), :] = _dot(wa_ref[n2], z.astype(BF16))

    @pl.when(s == 0)
    def _():
        def body(n2, c):
            signal_in(n2, col_slices(n2))
            return c
        lax.fori_loop(0, N2, body, 0, unroll=2 * UNROLL)

    @pl.when(s > 0)
    def _():
        def stage_b(j, c):
            y = _stage_b_fwd(buf_a, j, e_ref)
            prods = []
            for i in range(2):
                yk = y[:, i * LANES:(i + 1) * LANES]
                hb = h_ref[pl.ds(pl.multiple_of((2 * j + i) * 2 * N2, 2 * N2), 2 * N2), :].astype(F32)
                yr, yi = yk[:N2], yk[N2:]
                hr, hi = hb[:N2], hb[N2:]
                prods.append(jnp.concatenate([yr * hr - yi * hi, yr * hi + yi * hr], axis=0).astype(BF16))
            d = _dot(g_ref[...], jnp.concatenate(prods, axis=1))
            for i in range(2):
                rows = pl.ds(pl.multiple_of((2 * j + i) * PITCH_D, 8), 2 * N2)
                buf_d[rows, :] = d[:, i * LANES:(i + 1) * LANES]
            return c
        lax.fori_loop(0, N1 // 2, stage_b, 0, unroll=4 * UNROLL)

    def gated_conv(n2):
        re = buf_d[pl.ds(n2, N1, stride=PITCH_D), :]
        im = buf_d[pl.ds(N2 + n2, N1, stride=PITCH_D), :]
        x = jnp.concatenate([re, im], axis=0).astype(BF16)
        y = _dot(mi_ref[n2], x)
        return col_slices(n2) * (y + bias_ref[...] * zslab(n2))

    @pl.when(s == 1)
    def _():
        def body(n2, c):
            signal_in(n2, gated_conv(n2))
            return c
        lax.fori_loop(0, N2, body, 0, unroll=2 * UNROLL)

    @pl.when(s == 2)
    def _():
        def body(n2, c):
            zin[pl.ds(pl.multiple_of(n2 * PITCH_Z, 8), 2 * N2), :] = gated_conv(n2)
            return c
        lax.fori_loop(0, N2, body, 0, unroll=2 * UNROLL)

        def emit(n1, c):
            dst = pl.ds(pl.multiple_of(n1 * GRID_W, GRID_W), GRID_W)
            o_ref[0, dst, :] = zin[pl.ds(n1, GRID_W, stride=PITCH_Z), :].astype(o_ref.dtype)
            o_ref[1, dst, :] = zin[pl.ds(N2 + n1, GRID_W, stride=PITCH_Z), :].astype(o_ref.dtype)
            return c
        lax.fori_loop(0, N2, emit, 0, unroll=UNROLL)


def _hy_lat(hy, hspec, bias, cst):
    nct = HY_W // LANES
    part = lambda c, p, s: s * nct + c
    order = lambda s: jnp.maximum(s - 1, 0)
    consts = [cst[n] for n in ("wa", "e", "g", "mi")]
    return pl.pallas_call(
        _hy_lat_kernel,
        grid=(nct, NB // 2, 3),
        in_specs=[
            pl.BlockSpec((SEQ, LANES), lambda c, p, s: (2 * p, part(c, p, s))),
            pl.BlockSpec((SEQ, LANES), lambda c, p, s: (2 * p + 1, part(c, p, s))),
            pl.BlockSpec((None, None, 2 * FFT_N, LANES), lambda c, p, s: (order(s), c, 0, 0)),
            pl.BlockSpec((None, 1, LANES), lambda c, p, s: (order(s), 0, c)),
        ] + [_const_spec(x.shape) for x in consts],
        out_specs=pl.BlockSpec((2, SEQ, LANES), lambda c, p, s: (p, 0, c)),
        out_shape=jax.ShapeDtypeStruct((NB, SEQ, HY_W), BF16),
        scratch_shapes=[
            pltpu.VMEM((N2 * PITCH_Z, LANES), F32),
            pltpu.VMEM((N2 * PITCH_A, LANES), F32),
            pltpu.VMEM((N1 * PITCH_D, LANES), F32),
            pltpu.VMEM((2, N2 * PITCH_U, LANES), F32),
        ],
        compiler_params=_cparams(("arbitrary", "arbitrary", "arbitrary")),
        name="hy_lat",
    )(hy, hy, hspec, bias, *consts)


def _spec_ctx_kernel(h2_ref, rc_ref, w3f_ref, w3b_ref, dl_ref, wt_ref, o_ref):
    ridx = lax.broadcasted_iota(jnp.int32, (CTX_N, LANES), 0)
    tp = _taps_chunk(h2_ref[...], rc_ref[:, 0:1], w3f_ref[...], w3b_ref[...], dl_ref[...], ridx < CTX, ridx == CTX)
    l1 = jnp.sum(jnp.abs(tp), axis=0, keepdims=True)
    o_ref[...] = _dot(wt_ref[...], (tp * (1.0 / (l1 + EPS))).astype(BF16))


def _spec_ctx(h2, rc, w3, delta, cst):
    nct = HY_W // LANES
    return pl.pallas_call(
        _spec_ctx_kernel,
        grid=(2, nct),
        in_specs=[
            _const_spec((CTX_N, 3 * HY_HID)),
            _const_spec((CTX_N, 8)),
            pl.BlockSpec((3 * HY_HID, LANES), lambda o, c: (0, o * nct + c)),
            pl.BlockSpec((3 * HY_HID, LANES), lambda o, c: (0, 2 * nct + o * nct + c)),
            pl.BlockSpec((1, LANES), lambda o, c: (0, c)),
            _const_spec((2 * CTX_N, CTX_N)),
        ],
        out_specs=pl.BlockSpec((None, None, 2 * CTX_N, LANES), lambda o, c: (o, c, 0, 0)),
        out_shape=jax.ShapeDtypeStruct((2, nct, 2 * CTX_N, LANES), F32),
        compiler_params=_cparams(("arbitrary", "arbitrary")),
        name="spec_ctx",
    )(h2, rc, w3, w3, delta, cst["cwt"])


def _hy_ctx_kernel(va, vb, x1a, x1b, x2a, x2b, h_ref, bias_ref, wf_ref, wi_ref, o_ref):
    def pair(ra, rb):
        return jnp.concatenate([ra[...], rb[...]], axis=0).astype(F32)

    z = pair(va, vb)
    gates = (pair(x1a, x1b), pair(x2a, x2b))
    for o in range(2):
        y = _dot(wf_ref[...], z.astype(BF16))
        hb = h_ref[o]
        yr, yi = y[:CTX_N], y[CTX_N:]
        hr, hi = hb[:CTX_N], hb[CTX_N:]
        prod = jnp.concatenate([yr * hr - yi * hi, yr * hi + yi * hr], axis=0)
        conv = _dot(wi_ref[...], prod.astype(BF16))
        z = gates[o] * (conv + bias_ref[o] * z)
    o_ref[0] = z[:CTX].astype(o_ref.dtype)
    o_ref[1] = z[CTX:].astype(o_ref.dtype)


def _hy_ctx(hy, hspec, bias, cst):
    nct = HY_W // LANES
    base = R_LAT // CTX

    def blk(j, part):
        return pl.BlockSpec((CTX, LANES), lambda c, p: (base + 2 * p + j, part * nct + c))

    consts = [cst["cwf"], cst["cwi"]]
    return pl.pallas_call(
        _hy_ctx_kernel,
        grid=(nct, NB // 2),
        in_specs=[blk(0, 0), blk(1, 0), blk(0, 1), blk(1, 1), blk(0, 2), blk(1, 2),
                  pl.BlockSpec((2, None, 2 * CTX_N, LANES), lambda c, p: (0, c, 0, 0)),
                  pl.BlockSpec((2, 1, LANES), lambda c, p: (0, 0, c))]
        + [_const_spec(x.shape) for x in consts],
        out_specs=pl.BlockSpec((2, CTX, LANES), lambda c, p: (p, 0, c)),
        out_shape=jax.ShapeDtypeStruct((NB, CTX, HY_W), BF16),
        compiler_params=_cparams(("arbitrary", "arbitrary")),
        name="hy_ctx",
    )(hy, hy, hy, hy, hy, hy, hspec, bias, *consts)


def _stack(c):
    return np.block([[c.real, -c.imag], [c.imag, c.real]])


def _fft_constants():
    cst = {}
    k1 = np.arange(N1)[:, None]
    ca = np.exp(-2j * np.pi * k1 * np.arange(N1 // 2)[None, :] / N1)
    tw = np.exp(-2j * np.pi * k1 * np.arange(N2)[None, :] / FFT_N)
    ct = np.exp(-2j * np.pi * k1 * np.arange(N1)[None, :] / N1)
    cst["wa"] = jnp.asarray(np.stack([_stack(tw[:, n2:n2 + 1] * ca) for n2 in range(N2)]), F32).astype(BF16)
    wt = [tw[:, n2:n2 + 1] * ct for n2 in range(N2)]
    cst["wt"] = jnp.asarray(np.stack([np.concatenate([m.real, m.imag], axis=0) for m in wt]), F32).astype(BF16)
    e = np.exp(-2j * np.pi * np.arange(N2)[:, None] * np.arange(N2)[None, :] / N2)
    cst["e"] = jnp.asarray(_stack(e), F32).astype(BF16)
    cst["g"] = jnp.asarray(_stack(np.conj(e).T), F32).astype(BF16)
    bi = np.exp(2j * np.pi * np.arange(N1 // 2)[:, None] * np.arange(N1)[None, :] / N1) / FFT_N
    cst["mi"] = jnp.asarray(np.stack([_stack(bi * np.conj(tw[:, n2])[None, :]) for n2 in range(N2)]),
                            F32).astype(BF16)
    kk = np.arange(CTX_N)[:, None]
    cf = np.exp(-2j * np.pi * kk * np.arange(CTX)[None, :] / CTX_N)
    cst["cwf"] = jnp.asarray(_stack(cf), F32).astype(BF16)
    cfull = np.exp(-2j * np.pi * kk * np.arange(CTX_N)[None, :] / CTX_N)
    cst["cwt"] = jnp.asarray(np.concatenate([cfull.real, cfull.imag], axis=0), F32).astype(BF16)
    ci = np.exp(2j * np.pi * np.arange(CTX)[:, None] * np.arange(CTX_N)[None, :] / CTX_N) / CTX_N
    cst["cwi"] = jnp.asarray(_stack(ci), F32).astype(BF16)
    return cst


def _filter_positions(L, permute):
    bands = (HY_EMB - 1) // 2
    t = np.linspace(0.0, 1.0, L)[:, None]
    w = (2.0 * math.pi / L) * np.arange(L)[:, None]
    f = np.linspace(1e-4, bands - 1, bands)[None, :]
    z = np.concatenate([t, np.cos(f * w), -np.sin(f * w)], axis=-1)
    n = np.arange(2 * L)
    pos = np.where(n < L, n, 2 * L - n)
    pos = np.where(n == L, 0, pos)
    if permute:
        pos = pos[(np.arange(N1)[None, :] * N2 + np.arange(N2)[:, None]).reshape(-1)]
    feats = np.pad(z[pos], ((0, 0), (0, HY_HID - HY_EMB)))
    rc = np.pad(t[pos], ((0, 0), (0, 7)))
    return jnp.asarray(feats, F32), jnp.asarray(rc, F32)


def _decay_rates():
    max_decay = math.log(HY_TARGET) / HY_FAST
    min_decay = math.log(HY_TARGET) / HY_SLOW
    return jnp.asarray(np.abs(np.linspace(min_decay, max_decay, HY_W))[None, :], F32)


def _gla_constants():
    r = np.arange(GLA_BLK)
    same = (r[:, None] // CHUNK) == (r[None, :] // CHUNK)
    fwd = same & (r[:, None] >= r[None, :])
    bwd = same & (r[:, None] <= r[None, :])
    return jnp.asarray(np.stack([fwd, bwd]).astype(np.float32)).astype(BF16)


def kernel(x, c, ctx, c_ctx, ada_w, ada_b, ffn1_norm_g, ffn1_w_gate, ffn1_w_up, ffn1_w_down, mix_norm_g, w_in,
           gla_wa_f, gla_ba_f, gla_wa_b, gla_ba_b, gla_norm_g, w_gla_out, hy_conv_w, hy_conv_b, hy_f_w1, hy_f_b1,
           hy_f_freq1, hy_f_w2, hy_f_b2, hy_f_freq2, hy_f_w3, hy_bias, w_hy_out, w_out, ffn2_norm_g, ffn2_w_gate,
           ffn2_w_up, ffn2_w_down, final_norm_g):
    cst = _fft_constants()
    tri = _gla_constants()
    delta = _decay_rates()
    feats_lat, rc_lat = _filter_positions(SEQ, True)
    feats_ctx, rc_ctx = _filter_positions(CTX, False)

    cvec = jnp.concatenate([c, c_ctx[None, :], jnp.zeros((3, D), F32)], axis=0)
    mods = _mods(cvec, ada_w, ada_b).reshape(DEPTH, 8, 1, N_MOD * D)
    xs = (x.reshape(R_LAT, D), ctx.reshape(R_CTX, D))

    bf = lambda w: w.astype(BF16)
    ffn1_w = (bf(ffn1_w_gate), bf(ffn1_w_up), bf(ffn1_w_down))
    ffn2_w = (bf(ffn2_w_gate), bf(ffn2_w_up), bf(ffn2_w_down))
    out_w = (bf(w_gla_out), bf(w_hy_out), bf(w_out))
    w_t = jnp.swapaxes(w_in, 1, 2)
    wp = bf(w_t)
    wa_hi, wa_lo = _split(w_t[:, OFF_A:OFF_Q])
    conv_b = hy_conv_b[:, None, :]

    zpad = jnp.zeros((LOWRANK, KEY_W), F32)
    for i in range(DEPTH):
        last = i == DEPTH - 1
        xa = _ffn(xs, mods, i, 0, ffn1_norm_g[i], *ffn1_w, R_ALL)
        kvq, a, r, hy, gt = _proj(xa, mods, i, mix_norm_g[i], wp, wa_hi, wa_lo, hy_conv_w, conv_b)

        wa_cat = jnp.stack([jnp.concatenate([gla_wa_f[i], zpad], axis=0),
                            jnp.concatenate([zpad, gla_wa_b[i]], axis=0)])
        w_hi, w_lo = _split(wa_cat)
        ba_cat = jnp.stack([gla_ba_f[i], gla_ba_b[i]])[:, None, :]
        o2 = _gla(kvq, a, jnp.concatenate([w_hi, w_hi, w_lo], axis=1), ba_cat, tri)

        w1p = jnp.pad(hy_f_w1[i], ((0, HY_HID - HY_EMB), (0, 0)))
        mlp = (w1p, hy_f_b1[i], hy_f_freq1[i], hy_f_w2[i], hy_f_b2[i], hy_f_freq2[i])
        bias = hy_bias[i][:, None, :]
        w3s = _stack3_rhs(hy_f_w3[i])
        spec_lat = _spec_lat(_filt_mlp(feats_lat, *mlp), rc_lat, w3s, delta, cst)
        zs = (_hy_lat(hy, spec_lat, bias, cst).reshape(R_LAT, HY_W),)

        rows = R_LAT if last else R_ALL
        if not last:
            spec_ctx = _spec_ctx(_filt_mlp(feats_ctx, *mlp), rc_ctx, w3s, delta, cst)
            zs += (_hy_ctx(hy, spec_ctx, bias, cst).reshape(R_CTX, HY_W),)
        xa = _mix(xa, mods, i, o2, r, zs, gt, gla_norm_g[i], *out_w, rows)
        xs = (_ffn((xa,), mods, i, 2, ffn2_norm_g[i], *ffn2_w, rows, final_g=final_norm_g if last else None),)
    return xs[0].reshape(NB, SEQ, D)
```

```python
import functools
import math

import numpy as np
import jax
import jax.numpy as jnp
from jax import lax
from jax.experimental import pallas as pl
from jax.experimental.pallas import tpu as pltpu

F32 = jnp.float32
BF16 = jnp.bfloat16

D = 1024
NB = 4
SEQ = 4096
DEPTH = 2
CTX = 256
GRID_W = 64
D_FF = 2816
HALF_STEP = 0.5
N_MOD = 9
EPS = 1e-6

HEADS = 4
DK = 128
DV = 256
LOWRANK = 16
TAU = 16.0
CHUNK = 64
KEY_W = HEADS * DK
VAL_W = HEADS * DV

HY_W = 1024
HY_EMB = 33
HY_HID = 64
HY_FAST = 0.3
HY_SLOW = 1.5
HY_TARGET = 1e-2

OFF_K = 0
OFF_V = OFF_K + KEY_W
OFF_A = OFF_V + VAL_W
OFF_Q = OFF_A + 2 * LOWRANK
OFF_R = OFF_Q + KEY_W
OFF_H = OFF_R + VAL_W
OFF_G = OFF_H + 3 * HY_W
IN_W = OFF_G + 2 * D

R_LAT = NB * SEQ
R_CTX = NB * CTX
R_ALL = R_LAT + R_CTX

LANES = 128
VMEM_BYTES_V7X = 64 * 1024 * 1024
VMEM_LIMIT = VMEM_BYTES_V7X * 7 // 8

FFT_N = 2 * SEQ
N1 = 128
N2 = 64
PITCH_A = 264
PITCH_D = 136
PITCH_U = 72
PITCH_Z = 136
UNROLL = 8
CTX_N = 2 * CTX


def _cparams(sem):
    return pltpu.CompilerParams(dimension_semantics=sem, vmem_limit_bytes=VMEM_LIMIT)


def _const_spec(shape):
    nd = len(shape)
    return pl.BlockSpec(shape, lambda *_: (0,) * nd, pipeline_mode=pl.Buffered(1))


def _layer_spec(shape, layer):
    nd = len(shape)
    return pl.BlockSpec((None,) + tuple(shape), lambda *_: (layer,) + (0,) * nd, pipeline_mode=pl.Buffered(1))


def _split(x):
    hi = x.astype(BF16)
    lo = (x - hi.astype(F32)).astype(BF16)
    return hi, lo


def _dot(a, b):
    return jnp.dot(a, b, preferred_element_type=F32)


def _sigmoid(x):
    return 1.0 / (1.0 + jnp.exp(-x))


def _rms(x, g):
    ms = jnp.mean(x * x, axis=-1, keepdims=True)
    return x * lax.rsqrt(ms + EPS) * g


def _mods_kernel(c_ref, w_ref, b_ref, o_ref):
    c = c_ref[...]
    s = (c * _sigmoid(c)).astype(BF16)
    o_ref[...] = _dot(s, w_ref[...].astype(BF16)) + b_ref[...]


def _mods(cvec, ada_w, ada_b):
    tn = 1024
    return pl.pallas_call(
        _mods_kernel,
        grid=(DEPTH, N_MOD * D // tn),
        in_specs=[
            pl.BlockSpec((8, D), lambda l, j: (0, 0)),
            pl.BlockSpec((None, D, tn), lambda l, j: (l, 0, j)),
            pl.BlockSpec((None, 1, tn), lambda l, j: (l, 0, j)),
        ],
        out_specs=pl.BlockSpec((None, 8, tn), lambda l, j: (l, 0, j)),
        out_shape=jax.ShapeDtypeStruct((DEPTH, 8, N_MOD * D), F32),
        compiler_params=_cparams(("arbitrary", "arbitrary")),
        name="mods",
    )(cvec, ada_w, ada_b.reshape(DEPTH, 1, N_MOD * D))


def _mod_row(t, tm):
    lat_tiles = R_LAT // tm
    per_batch = SEQ // tm
    return jnp.where(t < lat_tiles, t // per_batch, NB)


def _row_specs(tm, width, split):
    if not split:
        return [pl.BlockSpec((tm, width), lambda t: (t, 0))]
    lat_tiles = R_LAT // tm
    return [pl.BlockSpec((tm, width), lambda t: (jnp.minimum(t, lat_tiles - 1), 0)),
            pl.BlockSpec((tm, width), lambda t: (jnp.maximum(t - lat_tiles, 0), 0))]


def _row_tile(refs, tm):
    if len(refs) == 1:
        return refs[0][...]
    return jnp.where(pl.program_id(0) < R_LAT // tm, refs[0][...], refs[1][...])


def _ffn_kernel(*refs, final, nx, tm):
    xs, (m_ref, g_ref, wg_ref, wu_ref, wd_ref), rest = refs[:nx], refs[nx:nx + 5], refs[nx + 5:]
    if final:
        fg_ref, o_ref = rest
    else:
        (o_ref,) = rest
    x = _row_tile(xs, tm)
    m = m_ref[...]
    shift, scale, gate = m[:, :D], m[:, D:2 * D], m[:, 2 * D:]
    h = (_rms(x, g_ref[...]) * (1.0 + scale) + shift).astype(BF16)
    a = _dot(h, wg_ref[...])
    u = _dot(h, wu_ref[...])
    act = (a * _sigmoid(a) * u).astype(BF16)
    y = _dot(act, wd_ref[...])
    out = x + (HALF_STEP * gate) * y
    if final:
        out = _rms(out, fg_ref[...])
    o_ref[...] = out


def _ffn(xs, mods, layer, slot, norm_g, wg, wu, wd, rows, final_g=None):
    tm = 512
    final = final_g is not None
    in_specs = _row_specs(tm, D, len(xs) == 2) + [
        pl.BlockSpec((None, None, 1, 3 * D), lambda t: (layer, _mod_row(t, tm), 0, slot)),
        _const_spec((1, D)),
        _layer_spec((D, D_FF), layer),
        _layer_spec((D, D_FF), layer),
        _layer_spec((D_FF, D), layer),
    ]
    args = list(xs) + [mods, norm_g.reshape(1, D), wg, wu, wd]
    if final:
        in_specs.append(_const_spec((1, D)))
        args.append(final_g.reshape(1, D))
    return pl.pallas_call(
        functools.partial(_ffn_kernel, final=final, nx=len(xs), tm=tm),
        grid=(rows // tm,),
        in_specs=in_specs,
        out_specs=pl.BlockSpec((tm, D), lambda t: (t, 0)),
        out_shape=jax.ShapeDtypeStruct((rows, D), F32),
        compiler_params=pltpu.CompilerParams(
            dimension_semantics=("arbitrary",), vmem_limit_bytes=VMEM_LIMIT,
            allow_input_fusion=[False] * (len(xs) + 2) + [True] * 3 + [False] * int(final)),
        name="ffn",
    )(*args)


def _proj_kernel(x_ref, m_ref, g_ref, wkv_ref, wq_ref, wahi_ref, walo_ref, wr_ref, wh_ref, wg_ref, cw_ref, cb_ref,
                 kvq_ref, a_ref, r_ref, hy_ref, gt_ref, *, tm):
    x = x_ref[...]
    m = m_ref[...]
    shift, scale = m[:, :D], m[:, D:2 * D]
    h = _rms(x, g_ref[...]) * (1.0 + scale) + shift
    hb = h.astype(BF16)
    period = jnp.where(pl.program_id(0) < R_LAT // tm, GRID_W, CTX)
    pos = lax.broadcasted_iota(jnp.int32, (tm, 1), 0) & (period - 1)
    first, last = pos == 0, pos == period - 1

    def hyena_part(part):
        cols = slice(part * HY_W, (part + 1) * HY_W)
        u = _dot_nt(hb, wh_ref[0, cols, :])
        prv = jnp.where(first, 0.0, pltpu.roll(u, 1, 0))
        nxt = jnp.where(last, 0.0, pltpu.roll(u, tm - 1, 0))
        conv = cw_ref[0:1, cols] * prv + cw_ref[1:2, cols] * u + cw_ref[2:3, cols] * nxt + cb_ref[:, cols]
        hy_ref[:, cols] = conv.astype(hy_ref.dtype)

    hyena_part(0)
    kvq_ref[:, :KEY_W + VAL_W] = _dot_nt(hb, wkv_ref[0]).astype(kvq_ref.dtype)
    kvq_ref[:, KEY_W + VAL_W:] = _dot_nt(hb, wq_ref[0]).astype(kvq_ref.dtype)
    hyena_part(1)
    gt_ref[...] = _dot_nt(hb, wg_ref[0]).astype(gt_ref.dtype)
    hyena_part(2)
    r_ref[...] = _dot_nt(hb, wr_ref[0]).astype(r_ref.dtype)
    hl = (h - hb.astype(F32)).astype(BF16)
    a_ref[...] = _dot_nt(hb, wahi_ref[...]) + _dot_nt(hl, wahi_ref[...]) + _dot_nt(hb, walo_ref[...])


def _proj(x, mods, layer, norm_g, wp, wa_hi, wa_lo, conv_w, conv_b):
    tm = 512
    nk = 2 * KEY_W + VAL_W
    na = 2 * LOWRANK
    row = lambda w: pl.BlockSpec((tm, w), lambda t: (t, 0))

    def wrows(start, stop):
        return pl.BlockSpec((pl.Element(1), pl.Element(stop - start), pl.Element(D)),
                            lambda t: (layer, start, 0), pipeline_mode=pl.Buffered(1))

    def wcols(width, blk):
        return pl.BlockSpec((None, width, D), lambda t: (layer, blk, 0), pipeline_mode=pl.Buffered(1))

    return pl.pallas_call(
        functools.partial(_proj_kernel, tm=tm),
        grid=(R_ALL // tm,),
        in_specs=[
            row(D),
            pl.BlockSpec((None, None, 1, 3 * D), lambda t: (layer, _mod_row(t, tm), 0, 1)),
            _const_spec((1, D)),
            wrows(OFF_K, OFF_A),
            wrows(OFF_Q, OFF_R),
            wcols(na, 0),
            wcols(na, 0),
            wrows(OFF_R, OFF_H),
            wrows(OFF_H, OFF_G),
            wrows(OFF_G, IN_W),
            pl.BlockSpec((None, 3, 3 * HY_W), lambda t: (layer, 0, 0)),
            pl.BlockSpec((None, 1, 3 * HY_W), lambda t: (layer, 0, 0)),
        ],
        out_specs=[row(nk), row(na), row(VAL_W), row(3 * HY_W), row(2 * D)],
        out_shape=[
            jax.ShapeDtypeStruct((R_ALL, nk), BF16),
            jax.ShapeDtypeStruct((R_ALL, na), F32),
            jax.ShapeDtypeStruct((R_ALL, VAL_W), BF16),
            jax.ShapeDtypeStruct((R_ALL, 3 * HY_W), BF16),
            jax.ShapeDtypeStruct((R_ALL, 2 * D), BF16),
        ],
        compiler_params=_cparams(("arbitrary",)),
        name="proj",
    )(x, mods, norm_g.reshape(1, D), wp, wp, wa_hi, wa_lo, wp, wp, wp, conv_w, conv_b)


GLA_BLK = 256
GLA_STEPS = 1 + SEQ // GLA_BLK


def _gla_row_block(b, d, s):
    lat = jnp.where(d == 0, s - 1, SEQ // GLA_BLK - s)
    return jnp.where(s == 0, R_LAT // GLA_BLK + b, b * (SEQ // GLA_BLK) + lat)


def _dot_nt(a, b):
    return lax.dot_general(a, b, (((1,), (1,)), ((), ())), preferred_element_type=F32)


def _dot_tn(a, b):
    return lax.dot_general(a, b, (((0,), (0,)), ((), ())), preferred_element_type=F32)


def _gla_body(v_ref, k0_ref, q0_ref, a0_ref, kn_ref, qn_ref, an_ref, wa_ref, ba_ref, tri_ref, o_ref,
              st_ref, qd0, qd1, kn0, kn1, ke0, ke1, bc0, bc1):
    qd_s, kn_s, ke_s, bc_s = (qd0, qd1), (kn0, kn1), (ke0, ke1), (bc0, bc1)
    d = pl.program_id(1)
    s = pl.program_id(2)
    nchunk = GLA_BLK // CHUNK
    tot_row = jnp.where(d == 0, CHUNK - 1, 0)

    def log_gates(a_blk):
        a_hi, a_lo = _split(a_blk)
        pre = _dot(jnp.concatenate([a_hi, a_lo, a_hi], axis=1), wa_ref[...]) + ba_ref[...]
        la = (jnp.minimum(pre, 0.0) - jnp.log(1.0 + jnp.exp(-jnp.abs(pre)))) * (1.0 / TAU)
        return _split(la)

    def cum_decay(slot, la_hi, la_lo):
        tri = tri_ref[...]
        bcum = _dot(tri, la_hi) + _dot(tri, la_lo)
        bc_s[slot][...] = bcum
        return bcum

    def decay_weight(slot, bcum, k_blk, q_blk):
        btot = jnp.concatenate(
            [jnp.broadcast_to(bc_s[slot][pl.ds(j * CHUNK + tot_row, 1), :], (CHUNK, KEY_W)) for j in range(nchunk)],
            axis=0)
        k = k_blk.astype(F32)
        q = q_blk.astype(F32)
        qd_s[slot][...] = (q * (DK ** -0.5) * jnp.exp(bcum)).astype(BF16)
        kn_s[slot][...] = (k * jnp.exp(-bcum)).astype(BF16)
        ke_s[slot][...] = (k * jnp.exp(btot - bcum)).astype(BF16)

    @pl.when(s == 0)
    def _():
        st_ref[...] = jnp.zeros_like(st_ref)
        decay_weight(0, cum_decay(0, *log_gates(a0_ref[...])), k0_ref[...], q0_ref[...])

    def scan_block(cur):
        seen = tri_ref[...].astype(F32) > 0.5

        def intra(h):
            ksl = slice(h * DK, (h + 1) * DK)
            att = jnp.where(seen, _dot_nt(qd_s[cur][:, ksl], kn_s[cur][:, ksl]), 0.0).astype(BF16)
            o_ref[:, h * DV:(h + 1) * DV] = _dot(att, v_ref[:, h * DV:(h + 1) * DV])

        def inter(j):
            r0 = pl.multiple_of(jnp.where(d == 0, j, nchunk - 1 - j) * CHUNK, CHUNK)
            rows = pl.ds(r0, CHUNK)
            dec = jnp.exp(bc_s[cur][pl.ds(r0 + tot_row, 1), :])
            for h in range(HEADS):
                ksl = slice(h * DK, (h + 1) * DK)
                vsl = slice(h * DV, (h + 1) * DV)
                st = st_ref[h]
                o_ref[rows, vsl] += _dot_nt(qd_s[cur][rows, ksl], st.astype(BF16))
                upd = _dot_tn(v_ref[rows, vsl], ke_s[cur][rows, ksl])
                st_ref[h] = st * dec[:, ksl] + upd

        intra(0)
        la_hi, la_lo = log_gates(an_ref[...])
        for h in range(1, HEADS):
            intra(h)
        bcum = cum_decay(1 - cur, la_hi, la_lo)
        inter(0)
        decay_weight(1 - cur, bcum, kn_ref[...], qn_ref[...])
        for j in range(1, nchunk):
            inter(j)

    for parity in range(2):
        pl.when(s % 2 == parity)(functools.partial(scan_block, parity))


def _gla(kvq, a, wa_cat, ba_cat, tri):
    qblk = (KEY_W + VAL_W) // KEY_W
    first = lambda b, d, s: _gla_row_block(b, d, 0)
    nxt = lambda b, d, s: _gla_row_block(b, d, jnp.minimum(s + 1, GLA_STEPS - 1))

    def kqa(blk):
        return [pl.BlockSpec((GLA_BLK, KEY_W), lambda b, d, s: (blk(b, d, s), 0)),
                pl.BlockSpec((GLA_BLK, KEY_W), lambda b, d, s: (blk(b, d, s), qblk)),
                pl.BlockSpec((GLA_BLK, 2 * LOWRANK), lambda b, d, s: (blk(b, d, s), 0))]

    return pl.pallas_call(
        _gla_body,
        grid=(NB, 2, GLA_STEPS),
        in_specs=[
            pl.BlockSpec((pl.Element(GLA_BLK), pl.Element(VAL_W)),
                         lambda b, d, s: (_gla_row_block(b, d, s) * GLA_BLK, KEY_W)),
        ] + kqa(first) + kqa(nxt) + [
            pl.BlockSpec((None, 6 * LOWRANK, KEY_W), lambda b, d, s: (d, 0, 0)),
            pl.BlockSpec((None, 1, KEY_W), lambda b, d, s: (d, 0, 0)),
            pl.BlockSpec((None, GLA_BLK, GLA_BLK), lambda b, d, s: (d, 0, 0)),
        ],
        out_specs=pl.BlockSpec((None, GLA_BLK, VAL_W), lambda b, d, s: (d, _gla_row_block(b, d, s), 0)),
        out_shape=jax.ShapeDtypeStruct((2, R_ALL, VAL_W), F32),
        scratch_shapes=[
            pltpu.VMEM((HEADS, DV, DK), F32),
        ] + [pltpu.VMEM((GLA_BLK, KEY_W), BF16)] * 6 + [pltpu.VMEM((GLA_BLK, KEY_W), F32)] * 2,
        compiler_params=_cparams(("arbitrary", "arbitrary", "arbitrary")),
        name="gla",
    )(kvq, kvq, kvq, a, kvq, kvq, a, wa_cat, ba_cat, tri)


def _mix_kernel(x_ref, m_ref, of_ref, ob_ref, r_ref, gt_ref, ng_ref, wgo_ref, who_ref, wo_ref, *rest, tm):
    zs, o_ref = rest[:-1], rest[-1]
    x = x_ref[...]
    gate = m_ref[...][:, 2 * D:]
    o = of_ref[...] + ob_ref[...]
    ng = ng_ref[...]
    parts = []
    for h in range(HEADS):
        sl = slice(h * DV, (h + 1) * DV)
        oh = o[:, sl]
        ms = jnp.mean(oh * oh, axis=-1, keepdims=True)
        parts.append(oh * lax.rsqrt(ms + EPS) * ng[:, sl])
    on = jnp.concatenate(parts, axis=-1)
    r = r_ref[...].astype(F32)
    y_gla = _dot((on * (r * _sigmoid(r))).astype(BF16), wgo_ref[...])
    y_hy = _dot(_row_tile(zs, tm), who_ref[...])
    gt = gt_ref[...].astype(F32)
    mix = _sigmoid(gt[:, :D]) * y_gla + _sigmoid(gt[:, D:]) * y_hy
    o_ref[...] = x + gate * _dot(mix.astype(BF16), wo_ref[...])


def _mix(x, mods, layer, o2, r, zs, gt, norm_g, wgo, who, wo, rows):
    tm = 512
    row = lambda w: pl.BlockSpec((tm, w), lambda t: (t, 0))
    return pl.pallas_call(
        functools.partial(_mix_kernel, tm=tm),
        grid=(rows // tm,),
        in_specs=[
            row(D),
            pl.BlockSpec((None, None, 1, 3 * D), lambda t: (layer, _mod_row(t, tm), 0, 1)),
            pl.BlockSpec((None, tm, VAL_W), lambda t: (0, t, 0)),
            pl.BlockSpec((None, tm, VAL_W), lambda t: (1, t, 0)),
            row(VAL_W),
            row(2 * D),
            _const_spec((1, VAL_W)),
            _layer_spec((VAL_W, D), layer),
            _layer_spec((HY_W, D), layer),
            _layer_spec((D, D), layer),
        ] + _row_specs(tm, HY_W, len(zs) == 2),
        out_specs=row(D),
        out_shape=jax.ShapeDtypeStruct((rows, D), F32),
        compiler_params=_cparams(("arbitrary",)),
        name="mix",
    )(x, mods, o2, o2, r, gt, norm_g.reshape(1, VAL_W), wgo, who, wo, *zs)


def _stack3_lhs(a):
    hi, lo = _split(a)
    return jnp.concatenate([hi, lo, hi], axis=1)


def _stack3_rhs(w):
    hi, lo = _split(w)
    return jnp.concatenate([hi, hi, lo], axis=0)


def _filt_mlp_kernel(z_ref, w1_ref, b1_ref, f1_ref, w2_ref, b2_ref, f2_ref, o_ref):
    h = jnp.sin(f1_ref[...] * (_dot(_stack3_lhs(z_ref[...]), w1_ref[...]) + b1_ref[...]))
    h = jnp.sin(f2_ref[...] * (_dot(_stack3_lhs(h), w2_ref[...]) + b2_ref[...]))
    o_ref[...] = _stack3_lhs(h)


def _filt_mlp(feats, w1, b1, f1, w2, b2, f2):
    rows = feats.shape[0]
    tm = 512
    vec = lambda: _const_spec((1, HY_HID))
    return pl.pallas_call(
        _filt_mlp_kernel,
        grid=(rows // tm,),
        in_specs=[pl.BlockSpec((tm, HY_HID), lambda t: (t, 0)), _const_spec((3 * HY_HID, HY_HID)), vec(), vec(),
                  _const_spec((3 * HY_HID, HY_HID)), vec(), vec()],
        out_specs=pl.BlockSpec((tm, 3 * HY_HID), lambda t: (t, 0)),
        out_shape=jax.ShapeDtypeStruct((rows, 3 * HY_HID), BF16),
        compiler_params=_cparams(("arbitrary",)),
        name="filt_mlp",
    )(feats, _stack3_rhs(w1), b1.reshape(1, HY_HID), f1.reshape(1, HY_HID), _stack3_rhs(w2),
      b2.reshape(1, HY_HID), f2.reshape(1, HY_HID))


def _taps_chunk(h2s, t, w3f, w3b, delta, fwd, zero):
    win = jnp.exp(-t * delta)
    h3 = _dot(h2s, jnp.concatenate([w3f, w3b], axis=1))
    return jnp.where(zero, 0.0, win * jnp.where(fwd, h3[:, :LANES], h3[:, LANES:]))


def _stage_b_fwd(buf_a, j, e_ref):
    cols = []
    for k1 in (2 * j, 2 * j + 1):
        re = buf_a[pl.ds(k1, N2, stride=PITCH_A), :]
        im = buf_a[pl.ds(N1 + k1, N2, stride=PITCH_A), :]
        cols.append(jnp.concatenate([re, im], axis=0).astype(BF16))
    return _dot(e_ref[...], jnp.concatenate(cols, axis=1))


def _spec_lat_kernel(h2_ref, rc_ref, w3f_ref, w3b_ref, dl_ref, wt_ref, e_ref, o_ref, taps, buf_a):
    tb = 512
    delta = dl_ref[...]

    ridx = lax.broadcasted_iota(jnp.int32, (tb, LANES), 0)
    fwd = (ridx & (N1 - 1)) < N1 // 2

    def fill(i, acc):
        rows = pl.ds(pl.multiple_of(i * tb, tb), tb)
        zero = ridx == jnp.where(i == 0, N1 // 2, -1)
        tp = _taps_chunk(h2_ref[rows, :], rc_ref[rows, 0:1], w3f_ref[...], w3b_ref[...], delta, fwd, zero)
        taps[rows, :] = tp
        return acc + jnp.sum(jnp.abs(tp), axis=0, keepdims=True)

    l1 = lax.fori_loop(0, FFT_N // tb, fill, jnp.zeros((1, LANES), F32), unroll=4)
    inv = 1.0 / (l1 + EPS)

    def stage_a(n2, c):
        rhs = taps[pl.ds(pl.multiple_of(n2 * N1, N1), N1), :]
        buf_a[pl.ds(pl.multiple_of(n2 * PITCH_A, 8), 2 * N1), :] = _dot(wt_ref[n2], rhs.astype(BF16))
        return c

    lax.fori_loop(0, N2, stage_a, 0, unroll=UNROLL)

    def stage_b(j, c):
        y = _stage_b_fwd(buf_a, j, e_ref)
        rows = 2 * N2
        o_ref[pl.ds(pl.multiple_of(2 * j * rows, 2 * rows), rows), :] = (y[:, :LANES] * inv).astype(o_ref.dtype)
        o_ref[pl.ds(pl.multiple_of(2 * j * rows + rows, rows), rows), :] = (y[:, LANES:] * inv).astype(o_ref.dtype)
        return c

    lax.fori_loop(0, N1 // 2, stage_b, 0, unroll=2 * UNROLL)


def _spec_lat(h2p, rcp, w3, delta, cst):
    nct = HY_W // LANES
    return pl.pallas_call(
        _spec_lat_kernel,
        grid=(2, nct),
        in_specs=[
            _const_spec((FFT_N, 3 * HY_HID)),
            _const_spec((FFT_N, 8)),
            pl.BlockSpec((3 * HY_HID, LANES), lambda o, c: (0, o * nct + c)),
            pl.BlockSpec((3 * HY_HID, LANES), lambda o, c: (0, 2 * nct + o * nct + c)),
            pl.BlockSpec((1, LANES), lambda o, c: (0, c)),
            _const_spec((N2, 2 * N1, N1)),
            _const_spec((2 * N2, 2 * N2)),
        ],
        out_specs=pl.BlockSpec((None, None, 2 * FFT_N, LANES), lambda o, c: (o, c, 0, 0)),
        out_shape=jax.ShapeDtypeStruct((2, nct, 2 * FFT_N, LANES), BF16),
        scratch_shapes=[pltpu.VMEM((FFT_N, LANES), F32), pltpu.VMEM((N2 * PITCH_A, LANES), F32)],
        compiler_params=_cparams(("arbitrary", "arbitrary")),
        name="spec_lat",
    )(h2p, rcp, w3, w3, delta, cst["wt"], cst["e"])


def _hy_lat_kernel(ua_ref, ub_ref, h_ref, bias_ref, wa_ref, e_ref, g_ref, mi_ref,
                   o_ref, zin, buf_a, buf_d, upad):
    s = pl.program_id(2)

    def zslab(n2):
        return zin[pl.ds(pl.multiple_of(n2 * PITCH_Z, 8), 2 * N2), :]

    def repitch(n1, c):
        src = pl.ds(pl.multiple_of(n1 * GRID_W, GRID_W), GRID_W)
        dst = pl.ds(pl.multiple_of(n1 * PITCH_U, 8), GRID_W)
        upad[0, dst, :] = ua_ref[src, :].astype(F32)
        upad[1, dst, :] = ub_ref[src, :].astype(F32)
        return c
    lax.fori_loop(0, N2, repitch, 0, unroll=UNROLL)

    def col_slices(n2):
        return jnp.concatenate([upad[j, pl.ds(n2, N2, stride=PITCH_U), :] for j in range(2)], axis=0)

    def signal_in(n2, z):
        zin[pl.ds(pl.multiple_of(n2 * PITCH_Z, 8), 2 * N2), :] = z
        buf_a[pl.ds(pl.multiple_of(n2 * PITCH_A, 8), 2 * N1), :] = _dot(wa_ref[n2], z.astype(BF16))

    @pl.when(s == 0)
    def _():
        def body(n2, c):
            signal_in(n2, col_slices(n2))
            return c
        lax.fori_loop(0, N2, body, 0, unroll=2 * UNROLL)

    @pl.when(s > 0)
    def _():
        def stage_b(j, c):
            y = _stage_b_fwd(buf_a, j, e_ref)
            prods = []
            for i in range(2):
                yk = y[:, i * LANES:(i + 1) * LANES]
                hb = h_ref[pl.ds(pl.multiple_of((2 * j + i) * 2 * N2, 2 * N2), 2 * N2), :].astype(F32)
                yr, yi = yk[:N2], yk[N2:]
                hr, hi = hb[:N2], hb[N2:]
                prods.append(jnp.concatenate([yr * hr - yi * hi, yr * hi + yi * hr], axis=0).astype(BF16))
            d = _dot(g_ref[...], jnp.concatenate(prods, axis=1))
            for i in range(2):
                rows = pl.ds(pl.multiple_of((2 * j + i) * PITCH_D, 8), 2 * N2)
                buf_d[rows, :] = d[:, i * LANES:(i + 1) * LANES]
            return c
        lax.fori_loop(0, N1 // 2, stage_b, 0, unroll=4 * UNROLL)

    def gated_conv(n2):
        re = buf_d[pl.ds(n2, N1, stride=PITCH_D), :]
        im = buf_d[pl.ds(N2 + n2, N1, stride=PITCH_D), :]
        x = jnp.concatenate([re, im], axis=0).astype(BF16)
        y = _dot(mi_ref[n2], x)
        return col_slices(n2) * (y + bias_ref[...] * zslab(n2))

    @pl.when(s == 1)
    def _():
        def body(n2, c):
            signal_in(n2, gated_conv(n2))
            return c
        lax.fori_loop(0, N2, body, 0, unroll=2 * UNROLL)

    @pl.when(s == 2)
    def _():
        def body(n2, c):
            zin[pl.ds(pl.multiple_of(n2 * PITCH_Z, 8), 2 * N2), :] = gated_conv(n2)
            return c
        lax.fori_loop(0, N2, body, 0, unroll=2 * UNROLL)

        def emit(n1, c):
            dst = pl.ds(pl.multiple_of(n1 * GRID_W, GRID_W), GRID_W)
            o_ref[0, dst, :] = zin[pl.ds(n1, GRID_W, stride=PITCH_Z), :].astype(o_ref.dtype)
            o_ref[1, dst, :] = zin[pl.ds(N2 + n1, GRID_W, stride=PITCH_Z), :].astype(o_ref.dtype)
            return c
        lax.fori_loop(0, N2, emit, 0, unroll=UNROLL)


def _hy_lat(hy, hspec, bias, cst):
    nct = HY_W // LANES
    part = lambda c, p, s: s * nct + c
    order = lambda s: jnp.maximum(s - 1, 0)
    consts = [cst[n] for n in ("wa", "e", "g", "mi")]
    return pl.pallas_call(
        _hy_lat_kernel,
        grid=(nct, NB // 2, 3),
        in_specs=[
            pl.BlockSpec((SEQ, LANES), lambda c, p, s: (2 * p, part(c, p, s))),
            pl.BlockSpec((SEQ, LANES), lambda c, p, s: (2 * p + 1, part(c, p, s))),
            pl.BlockSpec((None, None, 2 * FFT_N, LANES), lambda c, p, s: (order(s), c, 0, 0)),
            pl.BlockSpec((None, 1, LANES), lambda c, p, s: (order(s), 0, c)),
        ] + [_const_spec(x.shape) for x in consts],
        out_specs=pl.BlockSpec((2, SEQ, LANES), lambda c, p, s: (p, 0, c)),
        out_shape=jax.ShapeDtypeStruct((NB, SEQ, HY_W), BF16),
        scratch_shapes=[
            pltpu.VMEM((N2 * PITCH_Z, LANES), F32),
            pltpu.VMEM((N2 * PITCH_A, LANES), F32),
            pltpu.VMEM((N1 * PITCH_D, LANES), F32),
            pltpu.VMEM((2, N2 * PITCH_U, LANES), F32),
        ],
        compiler_params=_cparams(("arbitrary", "arbitrary", "arbitrary")),
        name="hy_lat",
    )(hy, hy, hspec, bias, *consts)


def _spec_ctx_kernel(h2_ref, rc_ref, w3f_ref, w3b_ref, dl_ref, wt_ref, o_ref):
    ridx = lax.broadcasted_iota(jnp.int32, (CTX_N, LANES), 0)
    tp = _taps_chunk(h2_ref[...], rc_ref[:, 0:1], w3f_ref[...], w3b_ref[...], dl_ref[...], ridx < CTX, ridx == CTX)
    l1 = jnp.sum(jnp.abs(tp), axis=0, keepdims=True)
    o_ref[...] = _dot(wt_ref[...], (tp * (1.0 / (l1 + EPS))).astype(BF16))


def _spec_ctx(h2, rc, w3, delta, cst):
    nct = HY_W // LANES
    return pl.pallas_call(
        _spec_ctx_kernel,
        grid=(2, nct),
        in_specs=[
            _const_spec((CTX_N, 3 * HY_HID)),
            _const_spec((CTX_N, 8)),
            pl.BlockSpec((3 * HY_HID, LANES), lambda o, c: (0, o * nct + c)),
            pl.BlockSpec((3 * HY_HID, LANES), lambda o, c: (0, 2 * nct + o * nct + c)),
            pl.BlockSpec((1, LANES), lambda o, c: (0, c)),
            _const_spec((2 * CTX_N, CTX_N)),
        ],
        out_specs=pl.BlockSpec((None, None, 2 * CTX_N, LANES), lambda o, c: (o, c, 0, 0)),
        out_shape=jax.ShapeDtypeStruct((2, nct, 2 * CTX_N, LANES), F32),
        compiler_params=_cparams(("arbitrary", "arbitrary")),
        name="spec_ctx",
    )(h2, rc, w3, w3, delta, cst["cwt"])


def _hy_ctx_kernel(va, vb, x1a, x1b, x2a, x2b, h_ref, bias_ref, wf_ref, wi_ref, o_ref):
    def pair(ra, rb):
        return jnp.concatenate([ra[...], rb[...]], axis=0).astype(F32)

    z = pair(va, vb)
    gates = (pair(x1a, x1b), pair(x2a, x2b))
    for o in range(2):
        y = _dot(wf_ref[...], z.astype(BF16))
        hb = h_ref[o]
        yr, yi = y[:CTX_N], y[CTX_N:]
        hr, hi = hb[:CTX_N], hb[CTX_N:]
        prod = jnp.concatenate([yr * hr - yi * hi, yr * hi + yi * hr], axis=0)
        conv = _dot(wi_ref[...], prod.astype(BF16))
        z = gates[o] * (conv + bias_ref[o] * z)
    o_ref[0] = z[:CTX].astype(o_ref.dtype)
    o_ref[1] = z[CTX:].astype(o_ref.dtype)


def _hy_ctx(hy, hspec, bias, cst):
    nct = HY_W // LANES
    base = R_LAT // CTX

    def blk(j, part):
        return pl.BlockSpec((CTX, LANES), lambda c, p: (base + 2 * p + j, part * nct + c))

    consts = [cst["cwf"], cst["cwi"]]
    return pl.pallas_call(
        _hy_ctx_kernel,
        grid=(nct, NB // 2),
        in_specs=[blk(0, 0), blk(1, 0), blk(0, 1), blk(1, 1), blk(0, 2), blk(1, 2),
                  pl.BlockSpec((2, None, 2 * CTX_N, LANES), lambda c, p: (0, c, 0, 0)),
                  pl.BlockSpec((2, 1, LANES), lambda c, p: (0, 0, c))]
        + [_const_spec(x.shape) for x in consts],
        out_specs=pl.BlockSpec((2, CTX, LANES), lambda c, p: (p, 0, c)),
        out_shape=jax.ShapeDtypeStruct((NB, CTX, HY_W), BF16),
        compiler_params=_cparams(("arbitrary", "arbitrary")),
        name="hy_ctx",
    )(hy, hy, hy, hy, hy, hy, hspec, bias, *consts)


def _stack(c):
    return np.block([[c.real, -c.imag], [c.imag, c.real]])


def _fft_constants():
    cst = {}
    k1 = np.arange(N1)[:, None]
    ca = np.exp(-2j * np.pi * k1 * np.arange(N1 // 2)[None, :] / N1)
    tw = np.exp(-2j * np.pi * k1 * np.arange(N2)[None, :] / FFT_N)
    ct = np.exp(-2j * np.pi * k1 * np.arange(N1)[None, :] / N1)
    cst["wa"] = jnp.asarray(np.stack([_stack(tw[:, n2:n2 + 1] * ca) for n2 in range(N2)]), F32).astype(BF16)
    wt = [tw[:, n2:n2 + 1] * ct for n2 in range(N2)]
    cst["wt"] = jnp.asarray(np.stack([np.concatenate([m.real, m.imag], axis=0) for m in wt]), F32).astype(BF16)
    e = np.exp(-2j * np.pi * np.arange(N2)[:, None] * np.arange(N2)[None, :] / N2)
    cst["e"] = jnp.asarray(_stack(e), F32).astype(BF16)
    cst["g"] = jnp.asarray(_stack(np.conj(e).T), F32).astype(BF16)
    bi = np.exp(2j * np.pi * np.arange(N1 // 2)[:, None] * np.arange(N1)[None, :] / N1) / FFT_N
    cst["mi"] = jnp.asarray(np.stack([_stack(bi * np.conj(tw[:, n2])[None, :]) for n2 in range(N2)]),
                            F32).astype(BF16)
    kk = np.arange(CTX_N)[:, None]
    cf = np.exp(-2j * np.pi * kk * np.arange(CTX)[None, :] / CTX_N)
    cst["cwf"] = jnp.asarray(_stack(cf), F32).astype(BF16)
    cfull = np.exp(-2j * np.pi * kk * np.arange(CTX_N)[None, :] / CTX_N)
    cst["cwt"] = jnp.asarray(np.concatenate([cfull.real, cfull.imag], axis=0), F32).astype(BF16)
    ci = np.exp(2j * np.pi * np.arange(CTX)[:, None] * np.arange(CTX_N)[None, :] / CTX_N) / CTX_N
    cst["cwi"] = jnp.asarray(_stack(ci), F32).astype(BF16)
    return cst


def _filter_positions(L, permute):
    bands = (HY_EMB - 1) // 2
    t = np.linspace(0.0, 1.0, L)[:, None]
    w = (2.0 * math.pi / L) * np.arange(L)[:, None]
    f = np.linspace(1e-4, bands - 1, bands)[None, :]
    z = np.concatenate([t, np.cos(f * w), -np.sin(f * w)], axis=-1)
    n = np.arange(2 * L)
    pos = np.where(n < L, n, 2 * L - n)
    pos = np.where(n == L, 0, pos)
    if permute:
        pos = pos[(np.arange(N1)[None, :] * N2 + np.arange(N2)[:, None]).reshape(-1)]
    feats = np.pad(z[pos], ((0, 0), (0, HY_HID - HY_EMB)))
    rc = np.pad(t[pos], ((0, 0), (0, 7)))
    return jnp.asarray(feats, F32), jnp.asarray(rc, F32)


def _decay_rates():
    max_decay = math.log(HY_TARGET) / HY_FAST
    min_decay = math.log(HY_TARGET) / HY_SLOW
    return jnp.asarray(np.abs(np.linspace(min_decay, max_decay, HY_W))[None, :], F32)


def _gla_constants():
    r = np.arange(GLA_BLK)
    same = (r[:, None] // CHUNK) == (r[None, :] // CHUNK)
    fwd = same & (r[:, None] >= r[None, :])
    bwd = same & (r[:, None] <= r[None, :])
    return jnp.asarray(np.stack([fwd, bwd]).astype(np.float32)).astype(BF16)


def kernel(x, c, ctx, c_ctx, ada_w, ada_b, ffn1_norm_g, ffn1_w_gate, ffn1_w_up, ffn1_w_down, mix_norm_g, w_in,
           gla_wa_f, gla_ba_f, gla_wa_b, gla_ba_b, gla_norm_g, w_gla_out, hy_conv_w, hy_conv_b, hy_f_w1, hy_f_b1,
           hy_f_freq1, hy_f_w2, hy_f_b2, hy_f_freq2, hy_f_w3, hy_bias, w_hy_out, w_out, ffn2_norm_g, ffn2_w_gate,
           ffn2_w_up, ffn2_w_down, final_norm_g):
    cst = _fft_constants()
    tri = _gla_constants()
    delta = _decay_rates()
    feats_lat, rc_lat = _filter_positions(SEQ, True)
    feats_ctx, rc_ctx = _filter_positions(CTX, False)

    cvec = jnp.concatenate([c, c_ctx[None, :], jnp.zeros((3, D), F32)], axis=0)
    mods = _mods(cvec, ada_w, ada_b).reshape(DEPTH, 8, 1, N_MOD * D)
    xs = (x.reshape(R_LAT, D), ctx.reshape(R_CTX, D))

    bf = lambda w: w.astype(BF16)
    ffn1_w = (bf(ffn1_w_gate), bf(ffn1_w_up), bf(ffn1_w_down))
    ffn2_w = (bf(ffn2_w_gate), bf(ffn2_w_up), bf(ffn2_w_down))
    out_w = (bf(w_gla_out), bf(w_hy_out), bf(w_out))
    w_t = jnp.swapaxes(w_in, 1, 2)
    wp = bf(w_t)
    wa_hi, wa_lo = _split(w_t[:, OFF_A:OFF_Q])
    conv_b = hy_conv_b[:, None, :]

    zpad = jnp.zeros((LOWRANK, KEY_W), F32)
    for i in range(DEPTH):
        last = i == DEPTH - 1
        xa = _ffn(xs, mods, i, 0, ffn1_norm_g[i], *ffn1_w, R_ALL)
        kvq, a, r, hy, gt = _proj(xa, mods, i, mix_norm_g[i], wp, wa_hi, wa_lo, hy_conv_w, conv_b)

        wa_cat = jnp.stack([jnp.concatenate([gla_wa_f[i], zpad], axis=0),
                            jnp.concatenate([zpad, gla_wa_b[i]], axis=0)])
        w_hi, w_lo = _split(wa_cat)
        ba_cat = jnp.stack([gla_ba_f[i], gla_ba_b[i]])[:, None, :]
        o2 = _gla(kvq, a, jnp.concatenate([w_hi, w_hi, w_lo], axis=1), ba_cat, tri)

        w1p = jnp.pad(hy_f_w1[i], ((0, HY_HID - HY_EMB), (0, 0)))
        mlp = (w1p, hy_f_b1[i], hy_f_freq1[i], hy_f_w2[i], hy_f_b2[i], hy_f_freq2[i])
        bias = hy_bias[i][:, None, :]
        w3s = _stack3_rhs(hy_f_w3[i])
        spec_lat = _spec_lat(_filt_mlp(feats_lat, *mlp), rc_lat, w3s, delta, cst)
        zs = (_hy_lat(hy, spec_lat, bias, cst).reshape(R_LAT, HY_W),)

        rows = R_LAT if last else R_ALL
        if not last:
            spec_ctx = _spec_ctx(_filt_mlp(feats_ctx, *mlp), rc_ctx, w3s, delta, cst)
            zs += (_hy_ctx(hy, spec_ctx, bias, cst).reshape(R_CTX, HY_W),)
        xa = _mix(xa, mods, i, o2, r, zs, gt, gla_norm_g[i], *out_w, rows)
        xs = (_ffn((xa,), mods, i, 2, ffn2_norm_g[i], *ffn2_w, rows, final_g=final_norm_g if last else None),)
    return xs[0].reshape(NB, SEQ, D)
```
